```python
import jax, jax.numpy as jnp
from jax import lax
import numpy as np

D_MODEL = 2048
BATCH = 4
SEQ = 2048
DEPTH = 1

N_DN_HEADS = 8
DN_HEAD_DIM = 128
DN_WIDTH = N_DN_HEADS * DN_HEAD_DIM
N_FOURIER_GROUPS = 8
FOURIER_GROUP_DIM = 128
FOURIER_WIDTH = N_FOURIER_GROUPS * FOURIER_GROUP_DIM
CONV_WIDTH = 5
CHUNK = 64
D_FF = 4 * D_MODEL
N_BRANCHES = 2
EPS = 1e-6

SPLIT_SIZES = (DN_WIDTH, DN_WIDTH, DN_WIDTH, DN_WIDTH, FOURIER_WIDTH, 4 * N_DN_HEADS, N_BRANCHES * D_MODEL)
IN_WIDTH = int(sum(SPLIT_SIZES))
SPLIT_POINTS = tuple(int(s) for s in np.cumsum(SPLIT_SIZES)[:-1])

kernel_name = "hybrid_gdn_fnet_gated_encoder_block"


def rms_norm(x, g):
    xf = x.astype(jnp.float32)
    y = xf * lax.rsqrt(jnp.mean(xf * xf, axis=-1, keepdims=True) + EPS)
    return (y * g.astype(jnp.float32)).astype(x.dtype)


def l2_norm(x):
    return x * lax.rsqrt(jnp.sum(x * x, axis=-1, keepdims=True) + EPS)


def centred_short_conv(u, w):
    c = u.shape[-1]
    pad = (CONV_WIDTH - 1) // 2
    return lax.conv_general_dilated(u, w[:, None, :].astype(u.dtype), window_strides=(1,),
                                    padding=[(pad, pad)], dimension_numbers=("NWC", "WIO", "NWC"),
                                    feature_group_count=c)


def gated_delta_rule_chunked(q, k, v, g, beta):
    b, h, t, dk = q.shape
    dv = v.shape[-1]
    n = t // CHUNK
    q = q.reshape(b, h, n, CHUNK, dk)
    k = k.reshape(b, h, n, CHUNK, dk)
    v = v.reshape(b, h, n, CHUNK, dv)
    gc = jnp.cumsum(g.reshape(b, h, n, CHUNK), axis=-1)
    beta = beta.reshape(b, h, n, CHUNK, 1)
    idx = jnp.arange(CHUNK)
    incl = idx[:, None] >= idx[None, :]
    strict = idx[:, None] > idx[None, :]
    diff = gc[..., :, None] - gc[..., None, :]
    decay = jnp.where(incl, jnp.exp(jnp.where(incl, diff, 0.0)), 0.0)
    kb = k * beta
    m = jnp.where(strict, jnp.einsum("bhnid,bhnjd->bhnij", kb, k) * decay, 0.0)
    a_mat = m + jnp.eye(CHUNK, dtype=m.dtype)
    rhs = jnp.concatenate([v * beta, kb * jnp.exp(gc)[..., None]], axis=-1)
    sol = lax.linalg.triangular_solve(a_mat, rhs, left_side=True, lower=True, unit_diagonal=True)
    u, w = sol[..., :dv], sol[..., dv:]
    qk = jnp.where(incl, jnp.einsum("bhnid,bhnjd->bhnij", q, k) * decay, 0.0)
    q_dec = q * jnp.exp(gc)[..., None]
    k_dec = k * jnp.exp(gc[..., -1:] - gc)[..., None]
    g_last = jnp.exp(gc[..., -1])
    xs = tuple(jnp.moveaxis(a, 2, 0) for a in (q_dec, k_dec, u, w, qk, g_last))

    def step(s, inp):
        qd, kd, uc, wc, qkc, gl = inp
        v_new = uc - jnp.einsum("bhck,bhkv->bhcv", wc, s)
        o = jnp.einsum("bhck,bhkv->bhcv", qd, s) + jnp.einsum("bhij,bhjv->bhiv", qkc, v_new)
        s = s * gl[..., None, None] + jnp.einsum("bhck,bhcv->bhkv", kd, v_new)
        return s, o

    s0 = jnp.zeros((b, h, dk, dv), jnp.float32)
    _, o = lax.scan(step, s0, xs)
    return jnp.moveaxis(o, 0, 2).reshape(b, h, t, dv)


def gated_deltanet_branch(q, k, v, z, scal, conv_w, a_log_f, a_log_b, dt_b_f, dt_b_b, g_head, w_up):
    b, t, _ = q.shape
    dtype = q.dtype
    qkv = jax.nn.silu(centred_short_conv(jnp.concatenate([q, k, v], axis=-1), conv_w)).astype(jnp.float32)
    q, k, v = jnp.split(qkv, 3, axis=-1)
    heads = lambda a: a.reshape(b, t, N_DN_HEADS, DN_HEAD_DIM).transpose(0, 2, 1, 3)
    q = l2_norm(heads(q)) * (DN_HEAD_DIM ** -0.5)
    k = l2_norm(heads(k))
    v = heads(v)
    scal = scal.astype(jnp.float32).transpose(0, 2, 1)
    beta_f, beta_b, a_f, a_b = jnp.split(scal, 4, axis=1)
    beta_f, beta_b = jax.nn.sigmoid(beta_f), jax.nn.sigmoid(beta_b)
    g_f = -jnp.exp(a_log_f.astype(jnp.float32))[:, None] * jax.nn.softplus(a_f + dt_b_f.astype(jnp.float32)[:, None])
    g_b = -jnp.exp(a_log_b.astype(jnp.float32))[:, None] * jax.nn.softplus(a_b + dt_b_b.astype(jnp.float32)[:, None])
    o_f = gated_delta_rule_chunked(q, k, v, g_f, beta_f)
    flip = lambda a: jnp.flip(a, axis=2)
    o_b = flip(gated_delta_rule_chunked(flip(q), flip(k), flip(v), flip(g_b), flip(beta_b)))
    o = (o_f + o_b).transpose(0, 2, 1, 3)
    o = o * lax.rsqrt(jnp.mean(o * o, axis=-1, keepdims=True) + EPS) * g_head.astype(jnp.float32)
    o = o * jax.nn.silu(z.astype(jnp.float32).reshape(b, t, N_DN_HEADS, DN_HEAD_DIM))
    return o.reshape(b, t, DN_WIDTH).astype(dtype) @ w_up


def fourier_branch(f, w_lin):
    b, t, _ = f.shape
    fg = f.astype(jnp.float32).reshape(b, t, N_FOURIER_GROUPS, FOURIER_GROUP_DIM)
    fr = jnp.real(jnp.fft.fft2(fg, axes=(1, 3), norm="ortho"))
    return fr.reshape(b, t, FOURIER_WIDTH).astype(f.dtype) @ w_lin


def setup_inputs(seed: int = 0) -> dict:
    key = jax.random.key(seed)
    ks = jax.random.split(key, 20)
    nrm = lambda k, shape, fan_in: jax.random.normal(k, shape, jnp.float32) * (fan_in ** -0.5)
    gain = lambda k, shape: 1.0 + 0.02 * jax.random.normal(k, shape, jnp.float32)
    a_log = lambda k: jnp.log(jax.random.uniform(k, (DEPTH, N_DN_HEADS), jnp.float32, 1.0, 16.0))

    def dt_bias(k):
        dt = jnp.exp(jax.random.uniform(k, (DEPTH, N_DN_HEADS), jnp.float32, np.log(1e-3), np.log(1e-1)))
        return dt + jnp.log(-jnp.expm1(-dt))

    return {
        "x": jax.random.normal(ks[0], (BATCH, SEQ, D_MODEL), jnp.float32),
        "g_mix": gain(ks[1], (DEPTH, D_MODEL)),
        "w_in": nrm(ks[2], (DEPTH, D_MODEL, IN_WIDTH), D_MODEL),
        "conv_w": nrm(ks[3], (DEPTH, CONV_WIDTH, 3 * DN_WIDTH), CONV_WIDTH),
        "a_log_fwd": a_log(ks[4]),
        "a_log_bwd": a_log(ks[5]),
        "dt_bias_fwd": dt_bias(ks[6]),
        "dt_bias_bwd": dt_bias(ks[7]),
        "g_dn_head": gain(ks[8], (DEPTH, DN_HEAD_DIM)),
        "w_dn_up": nrm(ks[9], (DEPTH, DN_WIDTH, D_MODEL), DN_WIDTH),
        "w_fourier": nrm(ks[10], (DEPTH, FOURIER_WIDTH, D_MODEL), FOURIER_WIDTH),
        "w_o": nrm(ks[11], (DEPTH, D_MODEL, D_MODEL), D_MODEL),
        "g_mlp": gain(ks[12], (DEPTH, D_MODEL)),
        "w_mlp_up": nrm(ks[13], (DEPTH, D_MODEL, D_FF), D_MODEL),
        "w_mlp_down": nrm(ks[14], (DEPTH, D_FF, D_MODEL), D_FF),
        "g_final": gain(ks[15], (D_MODEL,)),
    }


def reference(x, g_mix, w_in, conv_w, a_log_fwd, a_log_bwd, dt_bias_fwd, dt_bias_bwd, g_dn_head,
              w_dn_up, w_fourier, w_o, g_mlp, w_mlp_up, w_mlp_down, g_final):
    for l in range(DEPTH):
        h = rms_norm(x, g_mix[l])
        proj = h @ w_in[l]
        q, k, v, z, f, scal, gates = jnp.split(proj, SPLIT_POINTS, axis=-1)
        y_a = gated_deltanet_branch(q, k, v, z, scal, conv_w[l], a_log_fwd[l], a_log_bwd[l],
                                    dt_bias_fwd[l], dt_bias_bwd[l], g_dn_head[l], w_dn_up[l])
        y_f = fourier_branch(f, w_fourier[l])
        gate_a, gate_f = jnp.split(jax.nn.sigmoid(gates), N_BRANCHES, axis=-1)
        x = x + (gate_a * y_a + gate_f * y_f) @ w_o[l]
        h = rms_norm(x, g_mlp[l])
        x = x + jnp.square(jax.nn.relu(h @ w_mlp_up[l])) @ w_mlp_down[l]
    return rms_norm(x, g_final)
```

```python
import functools

import jax
import jax.numpy as jnp
import numpy as np
from jax import lax
from jax.experimental import pallas as pl
from jax.experimental.pallas import tpu as pltpu

F32 = jnp.float32
BF16 = jnp.bfloat16

EPS = 1e-6
N_HEADS = 8
HEAD_DIM = 128
CHUNK = 64
CONV_WIDTH = 5
N_GROUPS = 8
GROUP_DIM = 128
LANES = 128
SUBLANES = 8
VMEM_LIMIT = 56 * 1024 * 1024

_NT = (((1,), (1,)), ((), ()))


def _dot(a, b):
    return jnp.dot(a, b, preferred_element_type=F32)


def _dot_nt(a, b):
    return lax.dot_general(a, b, _NT, preferred_element_type=F32)


def _split3(x):
    hi = x.astype(BF16)
    r = x - hi.astype(F32)
    mid = r.astype(BF16)
    r = r - mid.astype(F32)
    return hi, mid, r.astype(BF16)


def _sigmoid(x):
    return 1.0 / (1.0 + jnp.exp(-x))


def _silu(x):
    return x * _sigmoid(x)


def _softplus(x):
    return jnp.maximum(x, 0.0) + jnp.log1p(jnp.exp(-jnp.abs(x)))


def _in_proj_kernel(x_ref, g_ref, w_ref, ws_ref, o_ref, scal_ref, h_scr, *, tm, rb):
    @pl.when(pl.program_id(1) == 0)
    def _():
        def body(r, c):
            rows = pl.ds(pl.multiple_of(r * rb, rb), rb)
            xf = x_ref[rows, :]
            ms = jnp.mean(xf * xf, axis=-1, keepdims=True)
            h_scr[rows, :] = (xf * lax.rsqrt(ms + EPS) * g_ref[...]).astype(BF16)
            return c
        lax.fori_loop(0, tm // rb, body, 0)
        scal_ref[...] = _dot(h_scr[...], ws_ref[...])

    o_ref[...] = _dot(h_scr[...], w_ref[...])


def _in_proj(x2d, g, w_main, w_scal, *, tm, tn):
    m, d = x2d.shape
    n = w_main.shape[1]
    return pl.pallas_call(
        functools.partial(_in_proj_kernel, tm=tm, rb=min(tm, 128)),
        grid=(m // tm, n // tn),
        in_specs=[
            pl.BlockSpec((tm, d), lambda i, j: (i, 0)),
            pl.BlockSpec((1, d), lambda i, j: (0, 0)),
            pl.BlockSpec((d, tn), lambda i, j: (0, j)),
            pl.BlockSpec((d, LANES), lambda i, j: (0, 0)),
        ],
        out_specs=[
            pl.BlockSpec((tm, tn), lambda i, j: (i, j)),
            pl.BlockSpec((tm, LANES), lambda i, j: (i, 0)),
        ],
        out_shape=[jax.ShapeDtypeStruct((m, n), F32), jax.ShapeDtypeStruct((m, LANES), F32)],
        scratch_shapes=[pltpu.VMEM((tm, d), BF16)],
        compiler_params=pltpu.CompilerParams(
            dimension_semantics=("parallel", "arbitrary"), vmem_limit_bytes=VMEM_LIMIT),
        name="in_proj",
    )(x2d, g, w_main, w_scal)


def _delta_kernel(hp_ref, q_ref, k_ref, v_ref, z_ref, scal_ref, cwq_ref, cwk_ref, cwv_ref, gh_ref,
                  out_ref,
                  pad_scr, qkv_scr, bt_scr, gc_scr, kw_scr, r_scr, qp_scr, op_scr, gl_scr, o_scr,
                  *, t, rb):
    nc = t // CHUNK
    nrb = t // rb
    h = pl.program_id(1)

    zero8 = jnp.zeros((SUBLANES, LANES), F32)
    pad_scr[0:SUBLANES, :] = zero8
    pad_scr[t + SUBLANES:t + 2 * SUBLANES, :] = zero8
    half = (CONV_WIDTH - 1) // 2

    for idx, (src_ref, cw_ref) in enumerate(((q_ref, cwq_ref), (k_ref, cwk_ref), (v_ref, cwv_ref))):
        def copy_in(r, c, src_ref=src_ref):
            r0 = pl.multiple_of(r * rb, rb)
            pad_scr[pl.ds(pl.multiple_of(r0 + SUBLANES, SUBLANES), rb), :] = src_ref[pl.ds(r0, rb), :]
            return c
        lax.fori_loop(0, nrb, copy_in, 0)

        def conv(r, c, idx=idx, cw_ref=cw_ref):
            r0 = pl.multiple_of(r * rb, rb)
            win = pad_scr[pl.ds(r0, rb + 2 * SUBLANES), :]
            acc = None
            for i in range(CONV_WIDTH):
                off = SUBLANES - half + i
                term = cw_ref[i:i + 1, :] * win[off:off + rb, :]
                acc = term if acc is None else acc + term
            y = _silu(acc)
            if idx < 2:
                y = y * lax.rsqrt(jnp.sum(y * y, axis=-1, keepdims=True) + EPS)
            if idx == 0:
                y = y * (HEAD_DIM ** -0.5)
            qkv_scr[idx, pl.ds(r0, rb), :] = y
            return c
        lax.fori_loop(0, nrb, conv, 0)

    row128 = lax.broadcasted_iota(jnp.int32, (LANES, LANES), 0)

    def selector(col):
        return jnp.where(row128 == col, 1.0, 0.0).astype(BF16)

    sel_bf, sel_bb = selector(h), selector(h + N_HEADS)
    sel_af, sel_ab = selector(h + 2 * N_HEADS), selector(h + 3 * N_HEADS)
    neg_a_f = -jnp.exp(jnp.full((1, LANES), hp_ref[0, h], F32))
    neg_a_b = -jnp.exp(jnp.full((1, LANES), hp_ref[1, h], F32))
    dt_f = hp_ref[2, h]
    dt_b = hp_ref[3, h]
    ri = lax.broadcasted_iota(jnp.int32, (CHUNK, CHUNK), 0)
    ci = lax.broadcasted_iota(jnp.int32, (CHUNK, CHUNK), 1)
    ltri = jnp.where(ri >= ci, 1.0, 0.0).astype(BF16)
    utri = jnp.where(ri <= ci, 1.0, 0.0).astype(BF16)

    def gates(r, c):
        r0 = pl.multiple_of(r * rb, rb)
        s3 = _split3(scal_ref[pl.ds(r0, rb), :])

        def column(sel):
            return _dot(s3[0], sel) + _dot(s3[1], sel) + _dot(s3[2], sel)

        bt_scr[0, pl.ds(r0, rb), :] = _sigmoid(column(sel_bf))
        bt_scr[1, pl.ds(r0, rb), :] = _sigmoid(column(sel_bb))
        g_f = neg_a_f * _softplus(column(sel_af) + dt_f)
        g_b = neg_a_b * _softplus(column(sel_ab) + dt_b)
        for cc in range(rb // CHUNK):
            rows = pl.ds(pl.multiple_of(r0 + cc * CHUNK, CHUNK), CHUNK)
            f3 = _split3(g_f[cc * CHUNK:(cc + 1) * CHUNK, :])
            b3 = _split3(g_b[cc * CHUNK:(cc + 1) * CHUNK, :])
            gc_scr[0, rows, :] = _dot(ltri, f3[0]) + _dot(ltri, f3[1]) + _dot(ltri, f3[2])
            gc_scr[1, rows, :] = _dot(utri, b3[0]) + _dot(utri, b3[1]) + _dot(utri, b3[2])
        return c
    lax.fori_loop(0, nrb, gates, 0)

    lane = lax.broadcasted_iota(jnp.int32, (CHUNK, LANES), 1)
    eye = jnp.where(ri == ci, 1.0, 0.0)

    def chunk(c, carry):
        rows = pl.ds(pl.multiple_of(c * CHUNK, CHUNK), CHUNK)
        srows = pl.ds(pl.multiple_of(c * HEAD_DIM, HEAD_DIM), HEAD_DIM)
        q = qkv_scr[0, rows, :]
        k = qkv_scr[1, rows, :]
        v = qkv_scr[2, rows, :]
        k16 = k.astype(BF16)
        qk_raw = _dot_nt(q.astype(BF16), k16)
        for d in range(2):
            beta = bt_scr[d, rows, :]
            g = gc_scr[d, rows, :]
            gl = g[CHUNK - 1:CHUNK, :] if d == 0 else g[0:1, :]
            eg = jnp.exp(g)
            kb = k * beta
            kbe = kb * eg
            qd = q * eg
            kd = k * jnp.exp(gl - g)
            hi, mid, lo = (p.astype(F32) for p in _split3(g))
            a_mat = jnp.where(lane == 0, hi, jnp.where(lane == 1, mid, jnp.where(lane == 2, lo,
                              jnp.where(lane < 6, 1.0, 0.0))))
            b_mat = jnp.where(lane < 3, 1.0, jnp.where(lane == 3, -hi, jnp.where(lane == 4, -mid,
                              jnp.where(lane == 5, -lo, 0.0))))
            diff = _dot_nt(a_mat.astype(BF16), b_mat.astype(BF16))
            incl = (ri >= ci) if d == 0 else (ri <= ci)
            strict = (ri > ci) if d == 0 else (ri < ci)
            decay = jnp.where(incl, jnp.exp(jnp.where(incl, diff, 0.0)), 0.0)
            m = jnp.where(strict, _dot_nt(kb.astype(BF16), k16) * decay, 0.0)
            qk = qk_raw * decay
            p = -m
            x = eye + p
            for _ in range(5):
                p16 = p.astype(BF16)
                p = _dot(p16, p16)
                x = x + _dot(x.astype(BF16), p.astype(BF16))
            rhs = jnp.concatenate([v * beta, kbe], axis=1).astype(BF16)
            sol = _dot(x.astype(BF16), rhs).astype(BF16)
            kdu_kdw = _dot(kd.T.astype(BF16), sol)
            qku_qkw = _dot(qk.astype(BF16), sol)
            r_scr[d, srows, :] = kdu_kdw[:, :HEAD_DIM]
            kw_scr[d, srows, :] = kdu_kdw[:, HEAD_DIM:].astype(BF16)
            op_scr[d, rows, :] = qku_qkw[:, :HEAD_DIM]
            qp_scr[d, rows, :] = (qd - qku_qkw[:, HEAD_DIM:]).astype(BF16)
            gl_scr[d, pl.ds(pl.multiple_of(c * SUBLANES, SUBLANES), SUBLANES), :] = jnp.broadcast_to(
                jnp.exp(gl), (SUBLANES, LANES))
        return carry
    lax.fori_loop(0, nc, chunk, 0)

    def step(s, carry):
        new = []
        for d in range(2):
            c = s if d == 0 else nc - 1 - s
            rows = pl.ds(pl.multiple_of(c * CHUNK, CHUNK), CHUNK)
            srows = pl.ds(pl.multiple_of(c * HEAD_DIM, HEAD_DIM), HEAD_DIM)
            st = carry[d]
            st16 = st.astype(BF16)
            o_scr[d, rows, :] = _dot(qp_scr[d, rows, :], st16) + op_scr[d, rows, :]
            gl = gl_scr[d, pl.ds(pl.multiple_of(c * SUBLANES, SUBLANES), SUBLANES), :][0:1, :]
            new.append(gl * st + (r_scr[d, srows, :] - _dot(kw_scr[d, srows, :], st16)))
        return tuple(new)
    s0 = jnp.zeros((HEAD_DIM, HEAD_DIM), F32)
    lax.fori_loop(0, nc, step, (s0, s0))

    def finish(r, c):
        rows = pl.ds(pl.multiple_of(r * rb, rb), rb)
        o = o_scr[0, rows, :] + o_scr[1, rows, :]
        y = o * lax.rsqrt(jnp.mean(o * o, axis=-1, keepdims=True) + EPS) * gh_ref[...]
        out_ref[rows, :] = (y * _silu(z_ref[rows, :])).astype(BF16)
        return c
    lax.fori_loop(0, nrb, finish, 0)


def _delta(proj3, scal3, conv_w, head_params, g_head, *, q_blk, z_blk):
    b, t, _ = proj3.shape
    rb = min(t, 256)

    def col(off):
        return pl.BlockSpec((None, t, HEAD_DIM), lambda bi, hi: (bi, 0, off + hi))

    def cw(off):
        return pl.BlockSpec((CONV_WIDTH, HEAD_DIM), lambda bi, hi: (0, off + hi))

    return pl.pallas_call(
        functools.partial(_delta_kernel, t=t, rb=rb),
        grid=(b, N_HEADS),
        in_specs=[
            pl.BlockSpec(memory_space=pltpu.SMEM),
            col(q_blk), col(q_blk + N_HEADS), col(q_blk + 2 * N_HEADS), col(z_blk),
            pl.BlockSpec((None, t, LANES), lambda bi, hi: (bi, 0, 0)),
            cw(0), cw(N_HEADS), cw(2 * N_HEADS),
            pl.BlockSpec((1, HEAD_DIM), lambda bi, hi: (0, 0)),
        ],
        out_specs=pl.BlockSpec((None, t, HEAD_DIM), lambda bi, hi: (bi, 0, hi)),
        out_shape=jax.ShapeDtypeStruct((b, t, N_HEADS * HEAD_DIM), BF16),
        scratch_shapes=[
            pltpu.VMEM((t + 2 * SUBLANES, LANES), F32),
            pltpu.VMEM((3, t, HEAD_DIM), F32),
            pltpu.VMEM((2, t, LANES), F32),
            pltpu.VMEM((2, t, LANES), F32),
            pltpu.VMEM((2, (t // CHUNK) * HEAD_DIM, HEAD_DIM), BF16),
            pltpu.VMEM((2, (t // CHUNK) * HEAD_DIM, HEAD_DIM), F32),
            pltpu.VMEM((2, t, HEAD_DIM), BF16),
            pltpu.VMEM((2, t, HEAD_DIM), F32),
            pltpu.VMEM((2, (t // CHUNK) * SUBLANES, LANES), F32),
            pltpu.VMEM((2, t, HEAD_DIM), F32),
        ],
        compiler_params=pltpu.CompilerParams(
            dimension_semantics=("parallel", "arbitrary"), vmem_limit_bytes=VMEM_LIMIT),
        name="delta",
    )(head_params, proj3, proj3, proj3, proj3, scal3, conv_w, conv_w, conv_w, g_head)


def _dft_chan_kernel(f_ref, wd_ref, o_ref):
    for g in range(N_GROUPS):
        cols = slice(g * GROUP_DIM, (g + 1) * GROUP_DIM)
        y = _dot(f_ref[:, cols].astype(BF16), wd_ref[...])
        o_ref[0, :, cols] = y[:, :GROUP_DIM].astype(BF16)
        o_ref[1, :, cols] = y[:, GROUP_DIM:].astype(BF16)


def _dft_chan(proj3, wd, *, f_blk, tt):
    b, t, _ = proj3.shape
    fw = N_GROUPS * GROUP_DIM
    return pl.pallas_call(
        _dft_chan_kernel,
        grid=(b, t // tt),
        in_specs=[
            pl.BlockSpec((None, tt, fw), lambda bi, ti: (bi, ti, f_blk)),
            pl.BlockSpec((GROUP_DIM, 2 * GROUP_DIM), lambda bi, ti: (0, 0)),
        ],
        out_specs=pl.BlockSpec((2, tt, fw), lambda bi, ti: (0, ti, bi)),
        out_shape=jax.ShapeDtypeStruct((2, t, b * fw), BF16),
        compiler_params=pltpu.CompilerParams(
            dimension_semantics=("parallel", "parallel"), vmem_limit_bytes=VMEM_LIMIT),
        name="dft_chan",
    )(proj3, wd)


def _scaled_matmul_kernel(a_ref, b_ref, o_ref, *, scale):
    o_ref[...] = (_dot(a_ref[...], b_ref[...]) * scale).astype(o_ref.dtype)


def _scaled_matmul(a, b, *, scale, tm, tn, name):
    m, k = a.shape
    n = b.shape[1]
    return pl.pallas_call(
        functools.partial(_scaled_matmul_kernel, scale=scale),
        grid=(n // tn, m // tm),
        in_specs=[
            pl.BlockSpec((tm, k), lambda j, i: (i, 0)),
            pl.BlockSpec((k, tn), lambda j, i: (0, j)),
        ],
        out_specs=pl.BlockSpec((tm, tn), lambda j, i: (i, j)),
        out_shape=jax.ShapeDtypeStruct((m, n), BF16),
        compiler_params=pltpu.CompilerParams(
            dimension_semantics=("parallel", "parallel"), vmem_limit_bytes=VMEM_LIMIT),
        name=name,
    )(a, b)


@functools.lru_cache(maxsize=None)
def _dft_tables(t, d):
    n = np.arange(t, dtype=np.int64)
    ang = 2.0 * np.pi * ((n[:, None] * n[None, :]) % t) / t
    wt = np.concatenate([np.cos(ang), np.sin(ang)], axis=1).astype(np.float32)
    c = np.arange(d, dtype=np.int64)
    angd = 2.0 * np.pi * ((c[:, None] * c[None, :]) % d) / d
    wd = np.concatenate([np.cos(angd), -np.sin(angd)], axis=1).astype(np.float32)
    return wt, wd


def _merge_kernel(og_ref, fr_ref, ga_ref, gf_ref, x_ref, wdn_ref, wf_ref, wo_ref, g_ref,
                  x1_ref, h2_ref):
    ya = _dot(og_ref[...], wdn_ref[...])
    yf = _dot(fr_ref[...], wf_ref[...])
    merged = _sigmoid(ga_ref[...]) * ya + _sigmoid(gf_ref[...]) * yf
    x1 = x_ref[...] + _dot(merged.astype(BF16), wo_ref[...])
    x1_ref[...] = x1
    ms = jnp.mean(x1 * x1, axis=-1, keepdims=True)
    h2_ref[...] = (x1 * lax.rsqrt(ms + EPS) * g_ref[...]).astype(BF16)


def _merge(og2, fr, proj2, x2d, w_dn, w_f, w_o, g_mlp, *, t, tm):
    m, d = x2d.shape
    bw = og2.shape[1]
    tpb = t // tm

    def const(shape):
        return pl.BlockSpec(shape, lambda i: (0, 0), pipeline_mode=pl.Buffered(1))

    return pl.pallas_call(
        _merge_kernel,
        grid=(m // tm,),
        in_specs=[
            pl.BlockSpec((tm, bw), lambda i: (i, 0)),
            pl.BlockSpec((tm, bw), lambda i: (i % tpb, i // tpb)),
            pl.BlockSpec((tm, d), lambda i: (i, 0)),
            pl.BlockSpec((tm, d), lambda i: (i, 1)),
            pl.BlockSpec((tm, d), lambda i: (i, 0)),
            const((bw, d)), const((bw, d)), const((d, d)), const((1, d)),
        ],
        out_specs=[pl.BlockSpec((tm, d), lambda i: (i, 0)), pl.BlockSpec((tm, d), lambda i: (i, 0))],
        out_shape=[jax.ShapeDtypeStruct((m, d), F32), jax.ShapeDtypeStruct((m, d), BF16)],
        compiler_params=pltpu.CompilerParams(
            dimension_semantics=("parallel",), vmem_limit_bytes=VMEM_LIMIT),
        name="merge",
    )(og2, fr, proj2, proj2, x2d, w_dn, w_f, w_o, g_mlp)


def _mlp_kernel(h_ref, wu_ref, wd_ref, x1_ref, g_ref, o_ref, acc_scr):
    j = pl.program_id(1)
    a = jnp.maximum(_dot(h_ref[...], wu_ref[...]), 0.0)
    contrib = _dot((a * a).astype(BF16), wd_ref[...])

    @pl.when(j == 0)
    def _():
        acc_scr[...] = contrib

    @pl.when(j > 0)
    def _():
        acc_scr[...] += contrib

    @pl.when(j == pl.num_programs(1) - 1)
    def _():
        y = x1_ref[...] + acc_scr[...]
        ms = jnp.mean(y * y, axis=-1, keepdims=True)
        o_ref[...] = y * lax.rsqrt(ms + EPS) * g_ref[...]


def _mlp(h2, w_up, w_down, x1, g_final, *, tm, tf):
    m, d = x1.shape
    ff = w_up.shape[1]
    return pl.pallas_call(
        _mlp_kernel,
        grid=(m // tm, ff // tf),
        in_specs=[
            pl.BlockSpec((tm, d), lambda i, j: (i, 0)),
            pl.BlockSpec((d, tf), lambda i, j: (0, j)),
            pl.BlockSpec((tf, d), lambda i, j: (j, 0)),
            pl.BlockSpec((tm, d), lambda i, j: (i, 0)),
            pl.BlockSpec((1, d), lambda i, j: (0, 0)),
        ],
        out_specs=pl.BlockSpec((tm, d), lambda i, j: (i, 0)),
        out_shape=jax.ShapeDtypeStruct((m, d), F32),
        scratch_shapes=[pltpu.VMEM((tm, d), F32)],
        compiler_params=pltpu.CompilerParams(
            dimension_semantics=("parallel", "arbitrary"), vmem_limit_bytes=VMEM_LIMIT),
        name="mlp",
    )(h2, w_up, w_down, x1, g_final)


def _layer(x, g_mix, w_in, conv_w, a_log_f, a_log_b, dt_f, dt_b, g_head, w_dn_up, w_fourier, w_o,
           g_mlp, w_mlp_up, w_mlp_down, g_out):
    b, t, d = x.shape
    m = b * t
    dn = N_HEADS * HEAD_DIM
    fw = N_GROUPS * GROUP_DIM
    o_q, o_k, o_v, o_z, o_f, o_s, o_g = np.cumsum([0, dn, dn, dn, dn, fw, 4 * N_HEADS])
    w_main = jnp.concatenate([w_in[:, o_g:], w_in[:, o_q:o_s]], axis=1).astype(BF16)
    w_scal = jnp.pad(w_in[:, o_s:o_g], ((0, 0), (0, LANES - 4 * N_HEADS))).astype(BF16)
    q_blk = (2 * d) // HEAD_DIM
    z_blk = q_blk + 3 * N_HEADS
    f_blk = (2 * d + 4 * dn) // fw

    x2d = x.reshape(m, d)
    proj, scal = _in_proj(x2d, g_mix.reshape(1, d), w_main, w_scal, tm=min(m, 1024), tn=512)
    proj3 = proj.reshape(b, t, proj.shape[1])

    head_params = jnp.stack([a_log_f, a_log_b, dt_f, dt_b]).astype(F32)
    og = _delta(proj3, scal.reshape(b, t, LANES), conv_w, head_params, g_head.reshape(1, HEAD_DIM),
                q_blk=q_blk, z_blk=z_blk)

    wt, wd = _dft_tables(t, GROUP_DIM)
    g2 = _dft_chan(proj3, jnp.asarray(wd).astype(BF16), f_blk=f_blk, tt=min(t, 512))
    fr = _scaled_matmul(jnp.asarray(wt).astype(BF16), g2.reshape(2 * t, b * fw),
                        scale=float((t * GROUP_DIM) ** -0.5), tm=min(t, 512), tn=fw, name="dft_seq")

    x1, h2 = _merge(og.reshape(m, dn), fr, proj, x2d, w_dn_up.astype(BF16), w_fourier.astype(BF16),
                    w_o.astype(BF16), g_mlp.reshape(1, d), t=t, tm=min(t, 256))
    return _mlp(h2, w_mlp_up.astype(BF16), w_mlp_down.astype(BF16), x1, g_out.reshape(1, d),
                tm=min(m, 512), tf=512)


def kernel(x, g_mix, w_in, conv_w, a_log_fwd, a_log_bwd, dt_bias_fwd, dt_bias_bwd, g_dn_head, w_dn_up,
           w_fourier, w_o, g_mlp, w_mlp_up, w_mlp_down, g_final):
    depth = g_mix.shape[0]
    assert depth == 1, "the final rmsnorm is fused into the last block's MLP kernel"
    b, t, d = x.shape
    out = _layer(x, g_mix[0], w_in[0], conv_w[0], a_log_fwd[0], a_log_bwd[0], dt_bias_fwd[0],
                 dt_bias_bwd[0], g_dn_head[0], w_dn_up[0], w_fourier[0], w_o[0], g_mlp[0],
                 w_mlp_up[0], w_mlp_down[0], g_final)
    return out.reshape(b, t, d)
```

```python
import functools

import jax
import jax.numpy as jnp
import numpy as np
from jax import lax
from jax.experimental import pallas as pl
from jax.experimental.pallas import tpu as pltpu

F32 = jnp.float32
BF16 = jnp.bfloat16

EPS = 1e-6
N_HEADS = 8
HEAD_DIM = 128
CHUNK = 64
CONV_WIDTH = 5
N_GROUPS = 8
GROUP_DIM = 128
LANES = 128
SUBLANES = 8
VMEM_LIMIT = 56 * 1024 * 1024

_NT = (((1,), (1,)), ((), ()))


def _dot(a, b):
    return jnp.dot(a, b, preferred_element_type=F32)


def _dot_nt(a, b):
    return lax.dot_general(a, b, _NT, preferred_element_type=F32)


def _split3(x):
    hi = x.astype(BF16)
    r = x - hi.astype(F32)
    mid = r.astype(BF16)
    r = r - mid.astype(F32)
    return hi, mid, r.astype(BF16)


def _sigmoid(x):
    return 1.0 / (1.0 + jnp.exp(-x))


def _silu(x):
    return x * _sigmoid(x)


def _softplus(x):
    return jnp.maximum(x, 0.0) + jnp.log1p(jnp.exp(-jnp.abs(x)))


GATE_ROWS = 256


def _in_proj_kernel(x_ref, g_ref, w_ref, ws_ref, alog_ref, dt_ref, o_ref, gates_ref, h_scr, *, tm, rb):
    @pl.when(pl.program_id(1) == 0)
    def _():
        def body(r, c):
            rows = pl.ds(pl.multiple_of(r * rb, rb), rb)
            xf = x_ref[rows, :]
            ms = jnp.mean(xf * xf, axis=-1, keepdims=True)
            h_scr[rows, :] = (xf * lax.rsqrt(ms + EPS) * g_ref[...]).astype(BF16)
            return c
        lax.fori_loop(0, tm // rb, body, 0)

        gr = min(tm, GATE_ROWS)
        ri = lax.broadcasted_iota(jnp.int32, (gr, gr), 0)
        ci = lax.broadcasted_iota(jnp.int32, (gr, gr), 1)
        same = (ri // CHUNK) == (ci // CHUNK)
        ltri = jnp.where(same & (ri >= ci), 1.0, 0.0).astype(BF16)
        utri = jnp.where(same & (ri <= ci), 1.0, 0.0).astype(BF16)
        lane = lax.broadcasted_iota(jnp.int32, (gr, LANES), 1)
        neg_a = -jnp.exp(alog_ref[...])

        def gates(r, c):
            rows = pl.ds(pl.multiple_of(r * gr, gr), gr)
            s = _dot(h_scr[rows, :], ws_ref[...])
            g3 = _split3(neg_a * _softplus(s + dt_ref[...]))
            gc_f = _dot(ltri, g3[0]) + _dot(ltri, g3[1]) + _dot(ltri, g3[2])
            gc_b = _dot(utri, g3[0]) + _dot(utri, g3[1]) + _dot(utri, g3[2])
            gates_ref[rows, :] = jnp.where(lane < 2 * N_HEADS, _sigmoid(s),
                                           jnp.where(lane < 3 * N_HEADS, gc_f, gc_b))
            return c
        lax.fori_loop(0, tm // gr, gates, 0)

    o_ref[...] = _dot(h_scr[...], w_ref[...])


def _in_proj(x2d, g, w_main, w_scal, alog_row, dt_row, *, tm, tn):
    m, d = x2d.shape
    n = w_main.shape[1]
    return pl.pallas_call(
        functools.partial(_in_proj_kernel, tm=tm, rb=min(tm, 128)),
        grid=(m // tm, n // tn),
        in_specs=[
            pl.BlockSpec((tm, d), lambda i, j: (i, 0)),
            pl.BlockSpec((1, d), lambda i, j: (0, 0)),
            pl.BlockSpec((d, tn), lambda i, j: (0, j)),
            pl.BlockSpec((d, LANES), lambda i, j: (0, 0)),
            pl.BlockSpec((1, LANES), lambda i, j: (0, 0)),
            pl.BlockSpec((1, LANES), lambda i, j: (0, 0)),
        ],
        out_specs=[
            pl.BlockSpec((tm, tn), lambda i, j: (i, j)),
            pl.BlockSpec((tm, LANES), lambda i, j: (i, 0)),
        ],
        out_shape=[jax.ShapeDtypeStruct((m, n), F32), jax.ShapeDtypeStruct((m, LANES), F32)],
        scratch_shapes=[pltpu.VMEM((tm, d), BF16)],
        compiler_params=pltpu.CompilerParams(
            dimension_semantics=("parallel", "arbitrary"), vmem_limit_bytes=VMEM_LIMIT),
        name="in_proj",
    )(x2d, g, w_main, w_scal, alog_row, dt_row)


def _delta_kernel(q_ref, k_ref, v_ref, z_ref, gates_ref, cwq_ref, cwk_ref, cwv_ref, gh_ref,
                  out_ref,
                  pad_scr, qkv_scr, o_scr,
                  kw0, r0, qp0, op0, gl0, kw1, r1, qp1, op1, gl1,
                  *, t, rb, gsz):
    nc = t // CHUNK
    ng = nc // gsz
    nrb = t // rb
    h = pl.program_id(1)

    zero8 = jnp.zeros((SUBLANES, LANES), F32)
    pad_scr[0:SUBLANES, :] = zero8
    pad_scr[t + SUBLANES:t + 2 * SUBLANES, :] = zero8
    half = (CONV_WIDTH - 1) // 2

    for idx, (src_ref, cw_ref) in enumerate(((q_ref, cwq_ref), (k_ref, cwk_ref), (v_ref, cwv_ref))):
        def copy_in(r, c, src_ref=src_ref):
            r0_ = pl.multiple_of(r * rb, rb)
            pad_scr[pl.ds(pl.multiple_of(r0_ + SUBLANES, SUBLANES), rb), :] = src_ref[pl.ds(r0_, rb), :]
            return c
        lax.fori_loop(0, nrb, copy_in, 0)

        def conv(r, c, idx=idx, cw_ref=cw_ref):
            r0_ = pl.multiple_of(r * rb, rb)
            win = pad_scr[pl.ds(r0_, rb + 2 * SUBLANES), :]
            acc = None
            for i in range(CONV_WIDTH):
                off = SUBLANES - half + i
                term = cw_ref[i:i + 1, :] * win[off:off + rb, :]
                acc = term if acc is None else acc + term
            y = _silu(acc)
            if idx < 2:
                y = y * lax.rsqrt(jnp.sum(y * y, axis=-1, keepdims=True) + EPS)
            if idx == 0:
                y = y * (HEAD_DIM ** -0.5)
            qkv_scr[idx, pl.ds(r0_, rb), :] = y
            return c
        lax.fori_loop(0, nrb, conv, 0)

    ri = lax.broadcasted_iota(jnp.int32, (CHUNK, CHUNK), 0)
    ci = lax.broadcasted_iota(jnp.int32, (CHUNK, CHUNK), 1)
    lane = lax.broadcasted_iota(jnp.int32, (CHUNK, LANES), 1)
    eye = jnp.where(ri == ci, 1.0, 0.0)
    bufs = ((kw0, r0, qp0, op0, gl0), (kw1, r1, qp1, op1, gl1))

    def chunk_of(gi, j, d):
        return gi * gsz + j if d == 0 else nc - 1 - (gi * gsz + j)

    def column(tile, c):
        return jnp.sum(jnp.where(lane == c, tile, 0.0), axis=-1, keepdims=True)

    def prep(gi, buf):
        kw_s, r_s, qp_s, op_s, gl_s = buf
        ch = [(j, d) for j in range(gsz) for d in range(2)]
        rows = [pl.ds(pl.multiple_of(chunk_of(gi, j, d) * CHUNK, CHUNK), CHUNK) for j, d in ch]
        q = [qkv_scr[0, r, :] for r in rows]
        k = [qkv_scr[1, r, :] for r in rows]
        v = [qkv_scr[2, r, :] for r in rows]
        gt = [gates_ref[r, :] for r in rows]
        beta = [column(a, d * N_HEADS + h) for a, (j, d) in zip(gt, ch)]
        g = [column(a, (2 + d) * N_HEADS + h) for a, (j, d) in zip(gt, ch)]
        gl = [a[CHUNK - 1:CHUNK, :] if d == 0 else a[0:1, :] for a, (j, d) in zip(g, ch)]
        k16 = [a.astype(BF16) for a in k]
        qk_raw = [_dot_nt(a.astype(BF16), b) for a, b in zip(q, k16)]
        g3 = [tuple(p.astype(F32) for p in _split3(a)) for a in g]
        a_mat = [jnp.where(lane == 0, hi, jnp.where(lane == 1, mid, jnp.where(lane == 2, lo,
                           jnp.where(lane < 6, 1.0, 0.0)))).astype(BF16) for hi, mid, lo in g3]
        b_mat = [jnp.where(lane < 3, 1.0, jnp.where(lane == 3, -hi, jnp.where(lane == 4, -mid,
                           jnp.where(lane == 5, -lo, 0.0)))).astype(BF16) for hi, mid, lo in g3]
        diff = [_dot_nt(a, b) for a, b in zip(a_mat, b_mat)]
        kb = [a * bt for a, bt in zip(k, beta)]
        kk = [_dot_nt(a.astype(BF16), b) for a, b in zip(kb, k16)]
        incl = [(ri >= ci) if d == 0 else (ri <= ci) for j, d in ch]
        strict = [(ri > ci) if d == 0 else (ri < ci) for j, d in ch]
        decay = [jnp.where(m, jnp.exp(jnp.where(m, a, 0.0)), 0.0) for a, m in zip(diff, incl)]
        p = [jnp.where(m, -(a * dc), 0.0) for a, dc, m in zip(kk, decay, strict)]
        x = [eye + a for a in p]
        for _ in range(5):
            p16 = [a.astype(BF16) for a in p]
            p = [_dot(a, a) for a in p16]
            upd = [_dot(a.astype(BF16), b.astype(BF16)) for a, b in zip(x, p)]
            x = [a + b for a, b in zip(x, upd)]
        eg = [jnp.exp(a) for a in g]
        rhs = [jnp.concatenate([a * bt, b * e], axis=1).astype(BF16)
               for a, bt, b, e in zip(v, beta, kb, eg)]
        sol = [_dot(a.astype(BF16), b).astype(BF16) for a, b in zip(x, rhs)]
        kd = [(a * jnp.exp(l - b)).T.astype(BF16) for a, b, l in zip(k, g, gl)]
        qk = [(a * dc).astype(BF16) for a, dc in zip(qk_raw, decay)]
        kdu_kdw = [_dot(a, b) for a, b in zip(kd, sol)]
        qku_qkw = [_dot(a, b) for a, b in zip(qk, sol)]
        for i, (j, d) in enumerate(ch):
            srows = slice(j * HEAD_DIM, (j + 1) * HEAD_DIM)
            crows = slice(j * CHUNK, (j + 1) * CHUNK)
            r_s[d, srows, :] = kdu_kdw[i][:, :HEAD_DIM]
            kw_s[d, srows, :] = kdu_kdw[i][:, HEAD_DIM:].astype(BF16)
            op_s[d, crows, :] = qku_qkw[i][:, :HEAD_DIM]
            qp_s[d, crows, :] = (q[i] * eg[i] - qku_qkw[i][:, HEAD_DIM:]).astype(BF16)
            gl_s[d, j * SUBLANES:(j + 1) * SUBLANES, :] = jnp.broadcast_to(jnp.exp(gl[i]), (SUBLANES, LANES))

    def scan(gi, buf, carry):
        kw_s, r_s, qp_s, op_s, gl_s = buf
        st = list(carry)
        for j in range(gsz):
            srows = slice(j * HEAD_DIM, (j + 1) * HEAD_DIM)
            crows = slice(j * CHUNK, (j + 1) * CHUNK)
            for d in range(2):
                rows = pl.ds(pl.multiple_of(chunk_of(gi, j, d) * CHUNK, CHUNK), CHUNK)
                st16 = st[d].astype(BF16)
                o_scr[d, rows, :] = _dot(qp_s[d, crows, :], st16) + op_s[d, crows, :]
                gl = gl_s[d, j * SUBLANES:j * SUBLANES + 1, :]
                st[d] = gl * st[d] + (r_s[d, srows, :] - _dot(kw_s[d, srows, :], st16))
        return tuple(st)

    s0 = jnp.zeros((HEAD_DIM, HEAD_DIM), F32)
    prep(0, bufs[0])

    def pair(pi, carry):
        prep(2 * pi + 1, bufs[1])
        carry = scan(2 * pi, bufs[0], carry)
        prep(2 * pi + 2, bufs[0])
        return scan(2 * pi + 1, bufs[1], carry)
    carry = lax.fori_loop(0, ng // 2 - 1, pair, (s0, s0))
    prep(ng - 1, bufs[1])
    carry = scan(ng - 2, bufs[0], carry)
    scan(ng - 1, bufs[1], carry)

    def finish(r, c):
        rows = pl.ds(pl.multiple_of(r * rb, rb), rb)
        o = o_scr[0, rows, :] + o_scr[1, rows, :]
        y = o * lax.rsqrt(jnp.mean(o * o, axis=-1, keepdims=True) + EPS) * gh_ref[...]
        out_ref[rows, :] = (y * _silu(z_ref[rows, :])).astype(BF16)
        return c
    lax.fori_loop(0, nrb, finish, 0)


def _delta(proj3, gates3, conv_w, g_head, *, q_blk, z_blk):
    b, t, _ = proj3.shape
    rb = min(t, 256)
    gsz = 4
    assert (t // CHUNK) % (2 * gsz) == 0

    def col(off):
        return pl.BlockSpec((None, t, HEAD_DIM), lambda bi, hi: (bi, 0, off + hi))

    def cw(off):
        return pl.BlockSpec((CONV_WIDTH, HEAD_DIM), lambda bi, hi: (0, off + hi))

    operands = [
        pltpu.VMEM((2, gsz * HEAD_DIM, HEAD_DIM), BF16),
        pltpu.VMEM((2, gsz * HEAD_DIM, HEAD_DIM), F32),
        pltpu.VMEM((2, gsz * CHUNK, HEAD_DIM), BF16),
        pltpu.VMEM((2, gsz * CHUNK, HEAD_DIM), F32),
        pltpu.VMEM((2, gsz * SUBLANES, LANES), F32),
    ]
    return pl.pallas_call(
        functools.partial(_delta_kernel, t=t, rb=rb, gsz=gsz),
        grid=(b, N_HEADS),
        in_specs=[
            col(q_blk), col(q_blk + N_HEADS), col(q_blk + 2 * N_HEADS), col(z_blk),
            pl.BlockSpec((None, t, LANES), lambda bi, hi: (bi, 0, 0)),
            cw(0), cw(N_HEADS), cw(2 * N_HEADS),
            pl.BlockSpec((1, HEAD_DIM), lambda bi, hi: (0, 0)),
        ],
        out_specs=pl.BlockSpec((None, t, HEAD_DIM), lambda bi, hi: (bi, 0, hi)),
        out_shape=jax.ShapeDtypeStruct((b, t, N_HEADS * HEAD_DIM), BF16),
        scratch_shapes=[
            pltpu.VMEM((t + 2 * SUBLANES, LANES), F32),
            pltpu.VMEM((3, t, HEAD_DIM), F32),
            pltpu.VMEM((2, t, HEAD_DIM), F32),
        ] + operands + operands,
        compiler_params=pltpu.CompilerParams(
            dimension_semantics=("parallel", "arbitrary"), vmem_limit_bytes=VMEM_LIMIT),
        name="delta",
    )(proj3, proj3, proj3, proj3, gates3, conv_w, conv_w, conv_w, g_head)


def _dft_chan_kernel(f_ref, wd_ref, o_ref):
    for g in range(N_GROUPS):
        cols = slice(g * GROUP_DIM, (g + 1) * GROUP_DIM)
        y = _dot(f_ref[:, cols].astype(BF16), wd_ref[...])
        o_ref[0, :, cols] = y[:, :GROUP_DIM].astype(BF16)
        o_ref[1, :, cols] = y[:, GROUP_DIM:].astype(BF16)


def _dft_chan(proj3, wd, *, f_blk, tt):
    b, t, _ = proj3.shape
    fw = N_GROUPS * GROUP_DIM
    return pl.pallas_call(
        _dft_chan_kernel,
        grid=(b, t // tt),
        in_specs=[
            pl.BlockSpec((None, tt, fw), lambda bi, ti: (bi, ti, f_blk)),
            pl.BlockSpec((GROUP_DIM, 2 * GROUP_DIM), lambda bi, ti: (0, 0)),
        ],
        out_specs=pl.BlockSpec((2, tt, fw), lambda bi, ti: (0, ti, bi)),
        out_shape=jax.ShapeDtypeStruct((2, t, b * fw), BF16),
        compiler_params=pltpu.CompilerParams(
            dimension_semantics=("parallel", "parallel"), vmem_limit_bytes=VMEM_LIMIT),
        name="dft_chan",
    )(proj3, wd)


def _scaled_matmul_kernel(a_ref, b_ref, o_ref, *, scale):
    o_ref[...] = (_dot(a_ref[...], b_ref[...]) * scale).astype(o_ref.dtype)


def _scaled_matmul(a, b, *, scale, tm, tn, name):
    m, k = a.shape
    n = b.shape[1]
    return pl.pallas_call(
        functools.partial(_scaled_matmul_kernel, scale=scale),
        grid=(n // tn, m // tm),
        in_specs=[
            pl.BlockSpec((tm, k), lambda j, i: (i, 0)),
            pl.BlockSpec((k, tn), lambda j, i: (0, j)),
        ],
        out_specs=pl.BlockSpec((tm, tn), lambda j, i: (i, j)),
        out_shape=jax.ShapeDtypeStruct((m, n), BF16),
        compiler_params=pltpu.CompilerParams(
            dimension_semantics=("parallel", "parallel"), vmem_limit_bytes=VMEM_LIMIT),
        name=name,
    )(a, b)


@functools.lru_cache(maxsize=None)
def _dft_tables(t, d):
    n = np.arange(t, dtype=np.int64)
    ang = 2.0 * np.pi * ((n[:, None] * n[None, :]) % t) / t
    wt = np.concatenate([np.cos(ang), np.sin(ang)], axis=1).astype(np.float32)
    c = np.arange(d, dtype=np.int64)
    angd = 2.0 * np.pi * ((c[:, None] * c[None, :]) % d) / d
    wd = np.concatenate([np.cos(angd), -np.sin(angd)], axis=1).astype(np.float32)
    return wt, wd


def _merge_kernel(og_ref, fr_ref, ga_ref, gf_ref, x_ref, wdn_ref, wf_ref, wo_ref, g_ref,
                  x1_ref, h2_ref):
    ya = _dot(og_ref[...], wdn_ref[...])
    yf = _dot(fr_ref[...], wf_ref[...])
    merged = _sigmoid(ga_ref[...]) * ya + _sigmoid(gf_ref[...]) * yf
    x1 = x_ref[...] + _dot(merged.astype(BF16), wo_ref[...])
    x1_ref[...] = x1
    ms = jnp.mean(x1 * x1, axis=-1, keepdims=True)
    h2_ref[...] = (x1 * lax.rsqrt(ms + EPS) * g_ref[...]).astype(BF16)


def _merge(og2, fr, proj2, x2d, w_dn, w_f, w_o, g_mlp, *, t, tm):
    m, d = x2d.shape
    bw = og2.shape[1]
    tpb = t // tm

    def const(shape):
        return pl.BlockSpec(shape, lambda i: (0, 0), pipeline_mode=pl.Buffered(1))

    return pl.pallas_call(
        _merge_kernel,
        grid=(m // tm,),
        in_specs=[
            pl.BlockSpec((tm, bw), lambda i: (i, 0)),
            pl.BlockSpec((tm, bw), lambda i: (i % tpb, i // tpb)),
            pl.BlockSpec((tm, d), lambda i: (i, 0)),
            pl.BlockSpec((tm, d), lambda i: (i, 1)),
            pl.BlockSpec((tm, d), lambda i: (i, 0)),
            const((bw, d)), const((bw, d)), const((d, d)), const((1, d)),
        ],
        out_specs=[pl.BlockSpec((tm, d), lambda i: (i, 0)), pl.BlockSpec((tm, d), lambda i: (i, 0))],
        out_shape=[jax.ShapeDtypeStruct((m, d), F32), jax.ShapeDtypeStruct((m, d), BF16)],
        compiler_params=pltpu.CompilerParams(
            dimension_semantics=("parallel",), vmem_limit_bytes=VMEM_LIMIT),
        name="merge",
    )(og2, fr, proj2, proj2, x2d, w_dn, w_f, w_o, g_mlp)


def _mlp_kernel(h_ref, wu_ref, wd_ref, x1_ref, g_ref, o_ref, acc_scr):
    j = pl.program_id(1)
    a = jnp.maximum(_dot(h_ref[...], wu_ref[...]), 0.0)
    contrib = _dot((a * a).astype(BF16), wd_ref[...])

    @pl.when(j == 0)
    def _():
        acc_scr[...] = contrib

    @pl.when(j > 0)
    def _():
        acc_scr[...] += contrib

    @pl.when(j == pl.num_programs(1) - 1)
    def _():
        y = x1_ref[...] + acc_scr[...]
        ms = jnp.mean(y * y, axis=-1, keepdims=True)
        o_ref[...] = y * lax.rsqrt(ms + EPS) * g_ref[...]


def _mlp(h2, w_up, w_down, x1, g_final, *, tm, tf):
    m, d = x1.shape
    ff = w_up.shape[1]
    return pl.pallas_call(
        _mlp_kernel,
        grid=(m // tm, ff // tf),
        in_specs=[
            pl.BlockSpec((tm, d), lambda i, j: (i, 0)),
            pl.BlockSpec((d, tf), lambda i, j: (0, j)),
            pl.BlockSpec((tf, d), lambda i, j: (j, 0)),
            pl.BlockSpec((tm, d), lambda i, j: (i, 0)),
            pl.BlockSpec((1, d), lambda i, j: (0, 0)),
        ],
        out_specs=pl.BlockSpec((tm, d), lambda i, j: (i, 0)),
        out_shape=jax.ShapeDtypeStruct((m, d), F32),
        scratch_shapes=[pltpu.VMEM((tm, d), F32)],
        compiler_params=pltpu.CompilerParams(
            dimension_semantics=("parallel", "arbitrary"), vmem_limit_bytes=VMEM_LIMIT),
        name="mlp",
    )(h2, w_up, w_down, x1, g_final)


def _layer(x, g_mix, w_in, conv_w, a_log_f, a_log_b, dt_f, dt_b, g_head, w_dn_up, w_fourier, w_o,
           g_mlp, w_mlp_up, w_mlp_down, g_out):
    b, t, d = x.shape
    m = b * t
    dn = N_HEADS * HEAD_DIM
    fw = N_GROUPS * GROUP_DIM
    o_q, o_k, o_v, o_z, o_f, o_s, o_g = np.cumsum([0, dn, dn, dn, dn, fw, 4 * N_HEADS])
    w_main = jnp.concatenate([w_in[:, o_g:], w_in[:, o_q:o_s]], axis=1).astype(BF16)
    w_scal = jnp.pad(w_in[:, o_s:o_g], ((0, 0), (0, LANES - 4 * N_HEADS))).astype(BF16)
    q_blk = (2 * d) // HEAD_DIM
    z_blk = q_blk + 3 * N_HEADS
    f_blk = (2 * d + 4 * dn) // fw

    def gate_row(fwd, bwd):
        return jnp.pad(jnp.concatenate([fwd, bwd]).astype(F32),
                       (2 * N_HEADS, LANES - 4 * N_HEADS)).reshape(1, LANES)

    x2d = x.reshape(m, d)
    proj, gates = _in_proj(x2d, g_mix.reshape(1, d), w_main, w_scal, gate_row(a_log_f, a_log_b),
                           gate_row(dt_f, dt_b), tm=min(m, 1024), tn=512)
    proj3 = proj.reshape(b, t, proj.shape[1])

    og = _delta(proj3, gates.reshape(b, t, LANES), conv_w, g_head.reshape(1, HEAD_DIM),
                q_blk=q_blk, z_blk=z_blk)

    wt, wd = _dft_tables(t, GROUP_DIM)
    g2 = _dft_chan(proj3, jnp.asarray(wd).astype(BF16), f_blk=f_blk, tt=min(t, 512))
    fr = _scaled_matmul(jnp.asarray(wt).astype(BF16), g2.reshape(2 * t, b * fw),
                        scale=float((t * GROUP_DIM) ** -0.5), tm=min(t, 512), tn=fw, name="dft_seq")

    x1, h2 = _merge(og.reshape(m, dn), fr, proj, x2d, w_dn_up.astype(BF16), w_fourier.astype(BF16),
                    w_o.astype(BF16), g_mlp.reshape(1, d), t=t, tm=min(t, 256))
    return _mlp(h2, w_mlp_up.astype(BF16), w_mlp_down.astype(BF16), x1, g_out.reshape(1, d),
                tm=min(m, 512), tf=512)


def kernel(x, g_mix, w_in, conv_w, a_log_fwd, a_log_bwd, dt_bias_fwd, dt_bias_bwd, g_dn_head, w_dn_up,
           w_fourier, w_o, g_mlp, w_mlp_up, w_mlp_down, g_final):
    depth = g_mix.shape[0]
    assert depth == 1, "the final rmsnorm is fused into the last block's MLP kernel"
    b, t, d = x.shape
    out = _layer(x, g_mix[0], w_in[0], conv_w[0], a_log_fwd[0], a_log_bwd[0], dt_bias_fwd[0],
                 dt_bias_bwd[0], g_dn_head[0], w_dn_up[0], w_fourier[0], w_o[0], g_mlp[0],
                 w_mlp_up[0], w_mlp_down[0], g_final)
    return out.reshape(b, t, d)
```

```python
import functools

import jax
import jax.numpy as jnp
import numpy as np
from jax import lax
from jax.experimental import pallas as pl
from jax.experimental.pallas import tpu as pltpu

F32 = jnp.float32
BF16 = jnp.bfloat16

EPS = 1e-6
N_HEADS = 8
HEAD_DIM = 128
CHUNK = 64
CONV_WIDTH = 5
N_GROUPS = 8
GROUP_DIM = 128
LANES = 128
SUBLANES = 8
VMEM_LIMIT = 56 * 1024 * 1024

_NT = (((1,), (1,)), ((), ()))


def _dot(a, b):
    return jnp.dot(a, b, preferred_element_type=F32)


def _dot_nt(a, b):
    return lax.dot_general(a, b, _NT, preferred_element_type=F32)


def _split3(x):
    hi = x.astype(BF16)
    r = x - hi.astype(F32)
    mid = r.astype(BF16)
    r = r - mid.astype(F32)
    return hi, mid, r.astype(BF16)


def _sigmoid(x):
    return 1.0 / (1.0 + jnp.exp(-x))


def _silu(x):
    return x * _sigmoid(x)


def _softplus(x):
    return jnp.maximum(x, 0.0) + jnp.log1p(jnp.exp(-jnp.abs(x)))


GATE_ROWS = 256


def _in_proj_kernel(x_ref, g_ref, wg_ref, wm_ref, ws_ref, alog_ref, dt_ref, o_ref, gates_ref, h_scr,
                    *, tm, rb, n_gate_steps):
    j = pl.program_id(1)

    @pl.when(j == 0)
    def _():
        def body(r, c):
            rows = pl.ds(pl.multiple_of(r * rb, rb), rb)
            xf = x_ref[rows, :]
            ms = jnp.mean(xf * xf, axis=-1, keepdims=True)
            h_scr[rows, :] = (xf * lax.rsqrt(ms + EPS) * g_ref[...]).astype(BF16)
            return c
        lax.fori_loop(0, tm // rb, body, 0)

        gr = min(tm, GATE_ROWS)
        ri = lax.broadcasted_iota(jnp.int32, (gr, gr), 0)
        ci = lax.broadcasted_iota(jnp.int32, (gr, gr), 1)
        same = (ri // CHUNK) == (ci // CHUNK)
        ltri = jnp.where(same & (ri >= ci), 1.0, 0.0).astype(BF16)
        utri = jnp.where(same & (ri <= ci), 1.0, 0.0).astype(BF16)
        lane = lax.broadcasted_iota(jnp.int32, (gr, LANES), 1)
        neg_a = -jnp.exp(alog_ref[...])

        def gates(r, c):
            rows = pl.ds(pl.multiple_of(r * gr, gr), gr)
            s = _dot(h_scr[rows, :], ws_ref[...])
            g3 = _split3(neg_a * _softplus(s + dt_ref[...]))
            gc_f = _dot(ltri, g3[0]) + _dot(ltri, g3[1]) + _dot(ltri, g3[2])
            gc_b = _dot(utri, g3[0]) + _dot(utri, g3[1]) + _dot(utri, g3[2])
            gates_ref[rows, :] = jnp.where(lane < 2 * N_HEADS, _sigmoid(s),
                                           jnp.where(lane < 3 * N_HEADS, gc_f, gc_b))
            return c
        lax.fori_loop(0, tm // gr, gates, 0)

    @pl.when(j < n_gate_steps)
    def _():
        o_ref[...] = _dot(h_scr[...], wg_ref[...])

    @pl.when(j >= n_gate_steps)
    def _():
        o_ref[...] = _dot(h_scr[...], wm_ref[...])


def _in_proj(x2d, g, w_gates, w_main, w_scal, alog_row, dt_row, *, tm, tn):
    m, d = x2d.shape
    ng, nm = w_gates.shape[1] // tn, w_main.shape[1] // tn
    return pl.pallas_call(
        functools.partial(_in_proj_kernel, tm=tm, rb=min(tm, 128), n_gate_steps=ng),
        grid=(m // tm, ng + nm),
        in_specs=[
            pl.BlockSpec((tm, d), lambda i, j: (i, 0)),
            pl.BlockSpec((1, d), lambda i, j: (0, 0)),
            pl.BlockSpec((d, tn), lambda i, j: (0, jnp.minimum(j, ng - 1))),
            pl.BlockSpec((d, tn), lambda i, j: (0, jnp.maximum(j - ng, 0))),
            pl.BlockSpec((d, LANES), lambda i, j: (0, 0)),
            pl.BlockSpec((1, LANES), lambda i, j: (0, 0)),
            pl.BlockSpec((1, LANES), lambda i, j: (0, 0)),
        ],
        out_specs=[
            pl.BlockSpec((tm, tn), lambda i, j: (i, j)),
            pl.BlockSpec((tm, LANES), lambda i, j: (i, 0)),
        ],
        out_shape=[jax.ShapeDtypeStruct((m, (ng + nm) * tn), F32), jax.ShapeDtypeStruct((m, LANES), F32)],
        scratch_shapes=[pltpu.VMEM((tm, d), BF16)],
        compiler_params=pltpu.CompilerParams(
            dimension_semantics=("parallel", "arbitrary"), vmem_limit_bytes=VMEM_LIMIT),
        name="in_proj",
    )(x2d, g, w_gates, w_main, w_scal, alog_row, dt_row)


def _delta_kernel(q_ref, k_ref, v_ref, z_ref, gates_ref, cwq_ref, cwk_ref, cwv_ref, gh_ref,
                  out_ref,
                  pad_scr, qkv_scr, o_scr,
                  kw0, r0, qp0, op0, gl0, kw1, r1, qp1, op1, gl1,
                  *, t, rb, gsz):
    nc = t // CHUNK
    ng = nc // gsz
    nrb = t // rb
    h = pl.program_id(1)

    zero8 = jnp.zeros((SUBLANES, LANES), F32)
    half = (CONV_WIDTH - 1) // 2
    streams = ((q_ref, cwq_ref), (k_ref, cwk_ref), (v_ref, cwv_ref))
    for idx in range(3):
        pad_scr[idx, 0:SUBLANES, :] = zero8
        pad_scr[idx, t + SUBLANES:t + 2 * SUBLANES, :] = zero8

    def copy_in(r, c):
        r0_ = pl.multiple_of(r * rb, rb)
        for idx, (src_ref, _) in enumerate(streams):
            pad_scr[idx, pl.ds(pl.multiple_of(r0_ + SUBLANES, SUBLANES), rb), :] = src_ref[pl.ds(r0_, rb), :]
        return c
    lax.fori_loop(0, nrb, copy_in, 0)

    def conv(r, c):
        r0_ = pl.multiple_of(r * rb, rb)
        for idx, (_, cw_ref) in enumerate(streams):
            win = pad_scr[idx, pl.ds(r0_, rb + 2 * SUBLANES), :]
            acc = None
            for i in range(CONV_WIDTH):
                off = SUBLANES - half + i
                term = cw_ref[i:i + 1, :] * win[off:off + rb, :]
                acc = term if acc is None else acc + term
            y = _silu(acc)
            if idx < 2:
                y = y * lax.rsqrt(jnp.sum(y * y, axis=-1, keepdims=True) + EPS)
            if idx == 0:
                y = y * (HEAD_DIM ** -0.5)
            qkv_scr[idx, pl.ds(r0_, rb), :] = y
        return c
    lax.fori_loop(0, nrb, conv, 0)

    ri = lax.broadcasted_iota(jnp.int32, (CHUNK, CHUNK), 0)
    ci = lax.broadcasted_iota(jnp.int32, (CHUNK, CHUNK), 1)
    lane = lax.broadcasted_iota(jnp.int32, (CHUNK, LANES), 1)
    eye = jnp.where(ri == ci, 1.0, 0.0)
    bufs = ((kw0, r0, qp0, op0, gl0), (kw1, r1, qp1, op1, gl1))

    def chunk_of(gi, j, d):
        return gi * gsz + j if d == 0 else nc - 1 - (gi * gsz + j)

    def column(tile, c):
        return jnp.sum(jnp.where(lane == c, tile, 0.0), axis=-1, keepdims=True)

    def prep(gi, buf):
        kw_s, r_s, qp_s, op_s, gl_s = buf
        ch = [(j, d) for j in range(gsz) for d in range(2)]
        rows = [pl.ds(pl.multiple_of(chunk_of(gi, j, d) * CHUNK, CHUNK), CHUNK) for j, d in ch]
        q = [qkv_scr[0, r, :] for r in rows]
        k = [qkv_scr[1, r, :] for r in rows]
        v = [qkv_scr[2, r, :] for r in rows]
        gt = [gates_ref[r, :] for r in rows]
        beta = [column(a, d * N_HEADS + h) for a, (j, d) in zip(gt, ch)]
        g = [column(a, (2 + d) * N_HEADS + h) for a, (j, d) in zip(gt, ch)]
        gl = [a[CHUNK - 1:CHUNK, :] if d == 0 else a[0:1, :] for a, (j, d) in zip(g, ch)]
        k16 = [a.astype(BF16) for a in k]
        qk_raw = [_dot_nt(a.astype(BF16), b) for a, b in zip(q, k16)]
        g3 = [tuple(p.astype(F32) for p in _split3(a)) for a in g]
        a_mat = [jnp.where(lane == 0, hi, jnp.where(lane == 1, mid, jnp.where(lane == 2, lo,
                           jnp.where(lane < 6, 1.0, 0.0)))).astype(BF16) for hi, mid, lo in g3]
        b_mat = [jnp.where(lane < 3, 1.0, jnp.where(lane == 3, -hi, jnp.where(lane == 4, -mid,
                           jnp.where(lane == 5, -lo, 0.0)))).astype(BF16) for hi, mid, lo in g3]
        diff = [_dot_nt(a, b) for a, b in zip(a_mat, b_mat)]
        kb = [a * bt for a, bt in zip(k, beta)]
        kk = [_dot_nt(a.astype(BF16), b) for a, b in zip(kb, k16)]
        incl = [(ri >= ci) if d == 0 else (ri <= ci) for j, d in ch]
        strict = [(ri > ci) if d == 0 else (ri < ci) for j, d in ch]
        decay = [jnp.where(m, jnp.exp(jnp.where(m, a, 0.0)), 0.0) for a, m in zip(diff, incl)]
        p = [jnp.where(m, -(a * dc), 0.0) for a, dc, m in zip(kk, decay, strict)]
        x = [eye + a for a in p]
        for _ in range(5):
            p16 = [a.astype(BF16) for a in p]
            p = [_dot(a, a) for a in p16]
            upd = [_dot(a.astype(BF16), b.astype(BF16)) for a, b in zip(x, p)]
            x = [a + b for a, b in zip(x, upd)]
        eg = [jnp.exp(a) for a in g]
        rhs = [jnp.concatenate([a * bt, b * e], axis=1).astype(BF16)
               for a, bt, b, e in zip(v, beta, kb, eg)]
        sol = [_dot(a.astype(BF16), b).astype(BF16) for a, b in zip(x, rhs)]
        kd = [(a * jnp.exp(l - b)).T.astype(BF16) for a, b, l in zip(k, g, gl)]
        qk = [(a * dc).astype(BF16) for a, dc in zip(qk_raw, decay)]
        kdu_kdw = [_dot(a, b) for a, b in zip(kd, sol)]
        qku_qkw = [_dot(a, b) for a, b in zip(qk, sol)]
        for i, (j, d) in enumerate(ch):
            srows = slice(j * HEAD_DIM, (j + 1) * HEAD_DIM)
            crows = slice(j * CHUNK, (j + 1) * CHUNK)
            r_s[d, srows, :] = kdu_kdw[i][:, :HEAD_DIM]
            kw_s[d, srows, :] = kdu_kdw[i][:, HEAD_DIM:].astype(BF16)
            op_s[d, crows, :] = qku_qkw[i][:, :HEAD_DIM]
            qp_s[d, crows, :] = (q[i] * eg[i] - qku_qkw[i][:, HEAD_DIM:]).astype(BF16)
            gl_s[d, j * SUBLANES:(j + 1) * SUBLANES, :] = jnp.broadcast_to(jnp.exp(gl[i]), (SUBLANES, LANES))

    def scan(gi, buf, carry):
        kw_s, r_s, qp_s, op_s, gl_s = buf
        st = list(carry)
        for j in range(gsz):
            srows = slice(j * HEAD_DIM, (j + 1) * HEAD_DIM)
            crows = slice(j * CHUNK, (j + 1) * CHUNK)
            for d in range(2):
                rows = pl.ds(pl.multiple_of(chunk_of(gi, j, d) * CHUNK, CHUNK), CHUNK)
                st16 = st[d].astype(BF16)
                o_scr[d, rows, :] = _dot(qp_s[d, crows, :], st16) + op_s[d, crows, :]
                gl = gl_s[d, j * SUBLANES:j * SUBLANES + 1, :]
                st[d] = gl * st[d] + (r_s[d, srows, :] - _dot(kw_s[d, srows, :], st16))
        return tuple(st)

    s0 = jnp.zeros((HEAD_DIM, HEAD_DIM), F32)
    prep(0, bufs[0])

    def pair(pi, carry):
        prep(2 * pi + 1, bufs[1])
        carry = scan(2 * pi, bufs[0], carry)
        prep(2 * pi + 2, bufs[0])
        return scan(2 * pi + 1, bufs[1], carry)
    carry = lax.fori_loop(0, ng // 2 - 1, pair, (s0, s0))
    prep(ng - 1, bufs[1])
    carry = scan(ng - 2, bufs[0], carry)
    scan(ng - 1, bufs[1], carry)

    def finish(r, c):
        rows = pl.ds(pl.multiple_of(r * rb, rb), rb)
        o = o_scr[0, rows, :] + o_scr[1, rows, :]
        y = o * lax.rsqrt(jnp.mean(o * o, axis=-1, keepdims=True) + EPS) * gh_ref[...]
        out_ref[rows, :] = (y * _silu(z_ref[rows, :])).astype(BF16)
        return c
    lax.fori_loop(0, nrb, finish, 0)


def _delta(proj3, gates3, conv_w, g_head, *, q_blk, z_blk):
    b, t, _ = proj3.shape
    rb = min(t, 256)
    gsz = min(8, t // (2 * CHUNK))
    assert (t // CHUNK) % (2 * gsz) == 0

    def col(off):
        return pl.BlockSpec((None, t, HEAD_DIM), lambda bi, hi: (bi, 0, off + hi))

    def cw(off):
        return pl.BlockSpec((CONV_WIDTH, HEAD_DIM), lambda bi, hi: (0, off + hi))

    operands = [
        pltpu.VMEM((2, gsz * HEAD_DIM, HEAD_DIM), BF16),
        pltpu.VMEM((2, gsz * HEAD_DIM, HEAD_DIM), F32),
        pltpu.VMEM((2, gsz * CHUNK, HEAD_DIM), BF16),
        pltpu.VMEM((2, gsz * CHUNK, HEAD_DIM), F32),
        pltpu.VMEM((2, gsz * SUBLANES, LANES), F32),
    ]
    return pl.pallas_call(
        functools.partial(_delta_kernel, t=t, rb=rb, gsz=gsz),
        grid=(b, N_HEADS),
        in_specs=[
            col(q_blk), col(q_blk + N_HEADS), col(q_blk + 2 * N_HEADS), col(z_blk),
            pl.BlockSpec((None, t, LANES), lambda bi, hi: (bi, 0, 0)),
            cw(0), cw(N_HEADS), cw(2 * N_HEADS),
            pl.BlockSpec((1, HEAD_DIM), lambda bi, hi: (0, 0)),
        ],
        out_specs=pl.BlockSpec((None, t, HEAD_DIM), lambda bi, hi: (bi, 0, hi)),
        out_shape=jax.ShapeDtypeStruct((b, t, N_HEADS * HEAD_DIM), BF16),
        scratch_shapes=[
            pltpu.VMEM((3, t + 2 * SUBLANES, LANES), F32),
            pltpu.VMEM((3, t, HEAD_DIM), F32),
            pltpu.VMEM((2, t, HEAD_DIM), F32),
        ] + operands + operands,
        compiler_params=pltpu.CompilerParams(
            dimension_semantics=("parallel", "arbitrary"), vmem_limit_bytes=VMEM_LIMIT),
        name="delta",
    )(proj3, proj3, proj3, proj3, gates3, conv_w, conv_w, conv_w, g_head)


def _dft_chan_kernel(f_ref, wd_ref, o_ref):
    for g in range(N_GROUPS):
        cols = slice(g * GROUP_DIM, (g + 1) * GROUP_DIM)
        y = _dot(f_ref[:, cols].astype(BF16), wd_ref[...])
        o_ref[0, :, cols] = y[:, :GROUP_DIM].astype(BF16)
        o_ref[1, :, cols] = y[:, GROUP_DIM:].astype(BF16)


def _dft_chan(proj3, wd, *, f_blk, tt):
    b, t, _ = proj3.shape
    fw = N_GROUPS * GROUP_DIM
    return pl.pallas_call(
        _dft_chan_kernel,
        grid=(b, t // tt),
        in_specs=[
            pl.BlockSpec((None, tt, fw), lambda bi, ti: (bi, ti, f_blk)),
            pl.BlockSpec((GROUP_DIM, 2 * GROUP_DIM), lambda bi, ti: (0, 0)),
        ],
        out_specs=pl.BlockSpec((2, tt, fw), lambda bi, ti: (0, ti, bi)),
        out_shape=jax.ShapeDtypeStruct((2, t, b * fw), BF16),
        compiler_params=pltpu.CompilerParams(
            dimension_semantics=("parallel", "parallel"), vmem_limit_bytes=VMEM_LIMIT),
        name="dft_chan",
    )(proj3, wd)


def _scaled_matmul_kernel(a_ref, b_ref, o_ref, *, scale):
    o_ref[...] = (_dot(a_ref[...], b_ref[...]) * scale).astype(o_ref.dtype)


def _scaled_matmul(a, b, *, scale, tm, tn, name):
    m, k = a.shape
    n = b.shape[1]
    return pl.pallas_call(
        functools.partial(_scaled_matmul_kernel, scale=scale),
        grid=(n // tn, m // tm),
        in_specs=[
            pl.BlockSpec((tm, k), lambda j, i: (i, 0)),
            pl.BlockSpec((k, tn), lambda j, i: (0, j)),
        ],
        out_specs=pl.BlockSpec((tm, tn), lambda j, i: (i, j)),
        out_shape=jax.ShapeDtypeStruct((m, n), BF16),
        compiler_params=pltpu.CompilerParams(
            dimension_semantics=("parallel", "parallel"), vmem_limit_bytes=VMEM_LIMIT),
        name=name,
    )(a, b)


@functools.lru_cache(maxsize=None)
def _dft_tables(t, d):
    n = np.arange(t, dtype=np.int64)
    ang = 2.0 * np.pi * ((n[:, None] * n[None, :]) % t) / t
    wt = np.concatenate([np.cos(ang), np.sin(ang)], axis=1).astype(np.float32)
    c = np.arange(d, dtype=np.int64)
    angd = 2.0 * np.pi * ((c[:, None] * c[None, :]) % d) / d
    wd = np.concatenate([np.cos(angd), -np.sin(angd)], axis=1).astype(np.float32)
    return wt, wd


def _merge_kernel(og_ref, fr_ref, ga_ref, gf_ref, x_ref, wdn_ref, wf_ref, wo_ref, g_ref,
                  x1_ref, h2_ref):
    ya = _dot(og_ref[...], wdn_ref[...])
    yf = _dot(fr_ref[...], wf_ref[...])
    merged = _sigmoid(ga_ref[...]) * ya + _sigmoid(gf_ref[...]) * yf
    x1 = x_ref[...] + _dot(merged.astype(BF16), wo_ref[...])
    x1_ref[...] = x1
    ms = jnp.mean(x1 * x1, axis=-1, keepdims=True)
    h2_ref[...] = (x1 * lax.rsqrt(ms + EPS) * g_ref[...]).astype(BF16)


def _merge(og2, fr, proj2, x2d, w_dn, w_f, w_o, g_mlp, *, t, tm):
    m, d = x2d.shape
    bw = og2.shape[1]
    tpb = t // tm

    def const(shape):
        return pl.BlockSpec(shape, lambda i: (0, 0), pipeline_mode=pl.Buffered(1))

    return pl.pallas_call(
        _merge_kernel,
        grid=(m // tm,),
        in_specs=[
            pl.BlockSpec((tm, bw), lambda i: (i, 0)),
            pl.BlockSpec((tm, bw), lambda i: (i % tpb, i // tpb)),
            pl.BlockSpec((tm, d), lambda i: (i, 0)),
            pl.BlockSpec((tm, d), lambda i: (i, 1)),
            pl.BlockSpec((tm, d), lambda i: (i, 0)),
            const((bw, d)), const((bw, d)), const((d, d)), const((1, d)),
        ],
        out_specs=[pl.BlockSpec((tm, d), lambda i: (i, 0)), pl.BlockSpec((tm, d), lambda i: (i, 0))],
        out_shape=[jax.ShapeDtypeStruct((m, d), F32), jax.ShapeDtypeStruct((m, d), BF16)],
        compiler_params=pltpu.CompilerParams(
            dimension_semantics=("parallel",), vmem_limit_bytes=VMEM_LIMIT),
        name="merge",
    )(og2, fr, proj2, proj2, x2d, w_dn, w_f, w_o, g_mlp)


def _mlp_kernel(h_ref, wu_ref, wd_ref, x1_ref, g_ref, o_ref):
    j = pl.program_id(1)

    @pl.when(j == 0)
    def _():
        o_ref[...] = x1_ref[...]

    a = jnp.maximum(_dot(h_ref[...], wu_ref[...]), 0.0)
    o_ref[...] += _dot((a * a).astype(BF16), wd_ref[...])

    @pl.when(j == pl.num_programs(1) - 1)
    def _():
        y = o_ref[...]
        ms = jnp.mean(y * y, axis=-1, keepdims=True)
        o_ref[...] = y * lax.rsqrt(ms + EPS) * g_ref[...]


def _mlp(h2, w_up, w_down, x1, g_final, *, tm, tf):
    m, d = x1.shape
    ff = w_up.shape[1]
    return pl.pallas_call(
        _mlp_kernel,
        grid=(m // tm, ff // tf),
        in_specs=[
            pl.BlockSpec((tm, d), lambda i, j: (i, 0)),
            pl.BlockSpec((d, tf), lambda i, j: (0, j)),
            pl.BlockSpec((tf, d), lambda i, j: (j, 0)),
            pl.BlockSpec((tm, d), lambda i, j: (i, 0)),
            pl.BlockSpec((1, d), lambda i, j: (0, 0)),
        ],
        out_specs=pl.BlockSpec((tm, d), lambda i, j: (i, 0)),
        out_shape=jax.ShapeDtypeStruct((m, d), F32),
        compiler_params=pltpu.CompilerParams(
            dimension_semantics=("parallel", "arbitrary"), vmem_limit_bytes=VMEM_LIMIT),
        name="mlp",
    )(h2, w_up, w_down, x1, g_final)


def _layer(x, g_mix, w_in, conv_w, a_log_f, a_log_b, dt_f, dt_b, g_head, w_dn_up, w_fourier, w_o,
           g_mlp, w_mlp_up, w_mlp_down, g_out):
    b, t, d = x.shape
    m = b * t
    dn = N_HEADS * HEAD_DIM
    fw = N_GROUPS * GROUP_DIM
    o_q, o_k, o_v, o_z, o_f, o_s, o_g = np.cumsum([0, dn, dn, dn, dn, fw, 4 * N_HEADS])
    w_gates = w_in[:, o_g:].astype(BF16)
    w_main = w_in[:, o_q:o_s].astype(BF16)
    w_scal = jnp.pad(w_in[:, o_s:o_g], ((0, 0), (0, LANES - 4 * N_HEADS))).astype(BF16)
    q_blk = (2 * d) // HEAD_DIM
    z_blk = q_blk + 3 * N_HEADS
    f_blk = (2 * d + 4 * dn) // fw

    def gate_row(fwd, bwd):
        return jnp.pad(jnp.concatenate([fwd, bwd]).astype(F32),
                       (2 * N_HEADS, LANES - 4 * N_HEADS)).reshape(1, LANES)

    x2d = x.reshape(m, d)
    proj, gates = _in_proj(x2d, g_mix.reshape(1, d), w_gates, w_main, w_scal, gate_row(a_log_f, a_log_b),
                           gate_row(dt_f, dt_b), tm=min(m, 1024), tn=512)
    proj3 = proj.reshape(b, t, proj.shape[1])

    og = _delta(proj3, gates.reshape(b, t, LANES), conv_w, g_head.reshape(1, HEAD_DIM),
                q_blk=q_blk, z_blk=z_blk)

    wt, wd = _dft_tables(t, GROUP_DIM)
    g2 = _dft_chan(proj3, jnp.asarray(wd).astype(BF16), f_blk=f_blk, tt=min(t, 512))
    fr = _scaled_matmul(jnp.asarray(wt).astype(BF16), g2.reshape(2 * t, b * fw),
                        scale=float((t * GROUP_DIM) ** -0.5), tm=min(t, 512), tn=fw, name="dft_seq")

    x1, h2 = _merge(og.reshape(m, dn), fr, proj, x2d, w_dn_up.astype(BF16), w_fourier.astype(BF16),
                    w_o.astype(BF16), g_mlp.reshape(1, d), t=t, tm=min(t, 256))
    return _mlp(h2, w_mlp_up.astype(BF16), w_mlp_down.astype(BF16), x1, g_out.reshape(1, d),
                tm=min(m, 512), tf=1024)


def kernel(x, g_mix, w_in, conv_w, a_log_fwd, a_log_bwd, dt_bias_fwd, dt_bias_bwd, g_dn_head, w_dn_up,
           w_fourier, w_o, g_mlp, w_mlp_up, w_mlp_down, g_final):
    depth = g_mix.shape[0]
    assert depth == 1, "the final rmsnorm is fused into the last block's MLP kernel"
    b, t, d = x.shape
    out = _layer(x, g_mix[0], w_in[0], conv_w[0], a_log_fwd[0], a_log_bwd[0], dt_bias_fwd[0],
                 dt_bias_bwd[0], g_dn_head[0], w_dn_up[0], w_fourier[0], w_o[0], g_mlp[0],
                 w_mlp_up[0], w_mlp_down[0], g_final)
    return out.reshape(b, t, d)
```

```python
import functools

import jax
import jax.numpy as jnp
import numpy as np
from jax import lax
from jax.experimental import pallas as pl
from jax.experimental.pallas import tpu as pltpu

F32 = jnp.float32
BF16 = jnp.bfloat16

EPS = 1e-6
N_HEADS = 8
HEAD_DIM = 128
CHUNK = 64
CONV_WIDTH = 5
N_GROUPS = 8
GROUP_DIM = 128
LANES = 128
SUBLANES = 8
VMEM_LIMIT = 56 * 1024 * 1024

_NT = (((1,), (1,)), ((), ()))


def _dot(a, b):
    return jnp.dot(a, b, preferred_element_type=F32)


def _dot_nt(a, b):
    return lax.dot_general(a, b, _NT, preferred_element_type=F32)


def _split3(x):
    hi = x.astype(BF16)
    r = x - hi.astype(F32)
    mid = r.astype(BF16)
    r = r - mid.astype(F32)
    return hi, mid, r.astype(BF16)


def _sigmoid(x):
    return 1.0 / (1.0 + jnp.exp(-x))


def _silu(x):
    return x * _sigmoid(x)


def _softplus(x):
    return jnp.maximum(x, 0.0) + jnp.log1p(jnp.exp(-jnp.abs(x)))


def _cast_kernel(w_ref, o_ref):
    o_ref[...] = w_ref[...].astype(o_ref.dtype)


def _cast_rows(w, *, tr):
    rows, cols = w.shape
    return pl.pallas_call(
        _cast_kernel,
        grid=(rows // tr,),
        in_specs=[pl.BlockSpec((tr, cols), lambda i: (i, 0))],
        out_specs=pl.BlockSpec((tr, cols), lambda i: (i, 0)),
        out_shape=jax.ShapeDtypeStruct((rows, cols), BF16),
        compiler_params=pltpu.CompilerParams(dimension_semantics=("parallel",), vmem_limit_bytes=VMEM_LIMIT),
        name="cast_w_in",
    )(w)


GATE_ROWS = 256


def _in_proj_kernel(st_ref, x_ref, g_ref, w_ref, ws_ref, alog_ref, dt_ref, o_ref, gates_ref, gates_t_ref,
                    h_scr, *, tm, rb):
    j = pl.program_id(1)

    @pl.when(j == 0)
    def _():
        def body(r, c):
            rows = pl.ds(pl.multiple_of(r * rb, rb), rb)
            xf = x_ref[rows, :]
            ms = jnp.mean(xf * xf, axis=-1, keepdims=True)
            h_scr[rows, :] = (xf * lax.rsqrt(ms + EPS) * g_ref[...]).astype(BF16)
            return c
        lax.fori_loop(0, tm // rb, body, 0)

        gr = min(tm, GATE_ROWS)
        ri = lax.broadcasted_iota(jnp.int32, (gr, gr), 0)
        ci = lax.broadcasted_iota(jnp.int32, (gr, gr), 1)
        same = (ri // CHUNK) == (ci // CHUNK)
        ltri = jnp.where(same & (ri >= ci), 1.0, 0.0).astype(BF16)
        utri = jnp.where(same & (ri <= ci), 1.0, 0.0).astype(BF16)
        lane = lax.broadcasted_iota(jnp.int32, (gr, LANES), 1)
        neg_a = -jnp.exp(alog_ref[...])

        def gates(r, c):
            rows = pl.ds(pl.multiple_of(r * gr, gr), gr)
            s = _dot(h_scr[rows, :], ws_ref[...])
            g3 = _split3(neg_a * _softplus(s + dt_ref[...]))
            gc_f = _dot(ltri, g3[0]) + _dot(ltri, g3[1]) + _dot(ltri, g3[2])
            gc_b = _dot(utri, g3[0]) + _dot(utri, g3[1]) + _dot(utri, g3[2])
            tile = jnp.where(lane < 2 * N_HEADS, _sigmoid(s), jnp.where(lane < 3 * N_HEADS, gc_f, gc_b))
            gates_ref[rows, :] = tile
            gates_t_ref[:, rows] = tile.T
            return c
        lax.fori_loop(0, tm // gr, gates, 0)

    o_ref[...] = _dot_nt(h_scr[...], w_ref[...])


def _in_proj(x2d, g, w_t, w_scal, alog_row, dt_row, *, tm, tn, row_starts):
    m, d = x2d.shape
    starts = jnp.asarray(row_starts, jnp.int32)
    n_steps = len(row_starts)
    return pl.pallas_call(
        functools.partial(_in_proj_kernel, tm=tm, rb=min(tm, 128)),
        grid_spec=pltpu.PrefetchScalarGridSpec(
            num_scalar_prefetch=1,
            grid=(m // tm, n_steps),
            in_specs=[
                pl.BlockSpec((tm, d), lambda i, j, st: (i, 0)),
                pl.BlockSpec((1, d), lambda i, j, st: (0, 0)),
                pl.BlockSpec((pl.Element(tn), pl.Element(d)), lambda i, j, st: (pl.multiple_of(st[j], 32), 0)),
                pl.BlockSpec((d, LANES), lambda i, j, st: (0, 0)),
                pl.BlockSpec((1, LANES), lambda i, j, st: (0, 0)),
                pl.BlockSpec((1, LANES), lambda i, j, st: (0, 0)),
            ],
            out_specs=[
                pl.BlockSpec((tm, tn), lambda i, j, st: (i, j)),
                pl.BlockSpec((tm, LANES), lambda i, j, st: (i, 0)),
                pl.BlockSpec((LANES, tm), lambda i, j, st: (0, i)),
            ],
            scratch_shapes=[pltpu.VMEM((tm, d), BF16)],
        ),
        out_shape=[jax.ShapeDtypeStruct((m, n_steps * tn), F32), jax.ShapeDtypeStruct((m, LANES), F32),
                   jax.ShapeDtypeStruct((LANES, m), F32)],
        compiler_params=pltpu.CompilerParams(
            dimension_semantics=("parallel", "arbitrary"), vmem_limit_bytes=VMEM_LIMIT),
        name="in_proj",
    )(starts, x2d, g, w_t, w_scal, alog_row, dt_row)


def _delta_kernel(q_ref, k_ref, v_ref, z_ref, gates_ref, gates_t_ref, cwq_ref, cwk_ref, cwv_ref, gh_ref,
                  out_ref,
                  pad_scr, qkv_scr, o_scr,
                  kw0, r0, qp0, op0, gl0, kw1, r1, qp1, op1, gl1,
                  *, t, rb, gsz):
    nc = t // CHUNK
    ng = nc // gsz
    nrb = t // rb
    h = pl.program_id(1)

    zero8 = jnp.zeros((SUBLANES, LANES), F32)
    half = (CONV_WIDTH - 1) // 2
    streams = ((q_ref, cwq_ref), (k_ref, cwk_ref), (v_ref, cwv_ref))
    for idx in range(3):
        pad_scr[idx, 0:SUBLANES, :] = zero8
        pad_scr[idx, t + SUBLANES:t + 2 * SUBLANES, :] = zero8

    def copy_in(r, c):
        r0_ = pl.multiple_of(r * rb, rb)
        for idx, (src_ref, _) in enumerate(streams):
            pad_scr[idx, pl.ds(pl.multiple_of(r0_ + SUBLANES, SUBLANES), rb), :] = src_ref[pl.ds(r0_, rb), :]
        return c
    lax.fori_loop(0, nrb, copy_in, 0)

    def conv(r, c):
        r0_ = pl.multiple_of(r * rb, rb)
        for idx, (_, cw_ref) in enumerate(streams):
            win = pad_scr[idx, pl.ds(r0_, rb + 2 * SUBLANES), :]
            acc = None
            for i in range(CONV_WIDTH):
                off = SUBLANES - half + i
                term = cw_ref[i:i + 1, :] * win[off:off + rb, :]
                acc = term if acc is None else acc + term
            y = _silu(acc)
            if idx < 2:
                y = y * lax.rsqrt(jnp.sum(y * y, axis=-1, keepdims=True) + EPS)
            if idx == 0:
                y = y * (HEAD_DIM ** -0.5)
            qkv_scr[idx, pl.ds(r0_, rb), :] = y
        return c
    lax.fori_loop(0, nrb, conv, 0)

    lane = lax.broadcasted_iota(jnp.int32, (CHUNK, LANES), 1)
    ri = lax.broadcasted_iota(jnp.int32, (CHUNK, LANES), 0)
    lo = lane < CHUNK
    ci = jnp.where(lo, lane, lane - CHUNK)
    ahead = jnp.where(lo, ri - ci, ci - ri)
    incl = ahead >= 0
    strict = ahead > 0
    eye = jnp.where(ri == ci, 1.0, 0.0)
    lane_row = lax.broadcasted_iota(jnp.int32, (1, LANES), 1)
    head_row = lax.broadcasted_iota(jnp.int32, (N_HEADS, LANES), 0)
    bufs = ((kw0, r0, qp0, op0, gl0), (kw1, r1, qp1, op1, gl1))

    def chunk_of(gi, j, d):
        return gi * gsz + j if d == 0 else nc - 1 - (gi * gsz + j)

    def column(tile, c):
        return jnp.sum(jnp.where(lane == c, tile, 0.0), axis=-1, keepdims=True)

    def pack(m):
        return jnp.where(lo, m[:CHUNK], m[CHUNK:])

    def blockdiag(m):
        zero = jnp.zeros_like(m)
        return jnp.concatenate([jnp.where(lo, m, zero), jnp.where(lo, zero, m)], axis=0)

    def prep(gi, buf):
        kw_s, r_s, qp_s, op_s, gl_s = buf
        ch = [(j, d) for j in range(gsz) for d in range(2)]
        rows = [pl.ds(pl.multiple_of(chunk_of(gi, j, d) * CHUNK, CHUNK), CHUNK) for j, d in ch]
        q = [qkv_scr[0, r, :] for r in rows]
        k = [qkv_scr[1, r, :] for r in rows]
        v = [qkv_scr[2, r, :] for r in rows]
        gt = [gates_ref[r, :] for r in rows]
        beta = [column(a, d * N_HEADS + h) for a, (j, d) in zip(gt, ch)]
        g = [column(a, (2 + d) * N_HEADS + h) for a, (j, d) in zip(gt, ch)]
        gl = [a[CHUNK - 1:CHUNK, :] if d == 0 else a[0:1, :] for a, (j, d) in zip(g, ch)]
        kb = [a * bt for a, bt in zip(k, beta)]

        def g_row(j, d):
            c = chunk_of(gi, j, d)
            odd = (j % 2) if d == 0 else ((nc - 1 - j) % 2)
            win = gates_t_ref[(2 + d) * N_HEADS:(3 + d) * N_HEADS,
                              pl.ds(pl.multiple_of((c - odd) * CHUNK, 2 * CHUNK), 2 * CHUNK)]
            win = jnp.sum(jnp.where(head_row == h, win, 0.0), axis=0, keepdims=True)
            return pltpu.roll(win, CHUNK, axis=1) if odd != d else win

        pair = lambda xs, j: jnp.concatenate([xs[2 * j], xs[2 * j + 1]], axis=0)
        k2 = [pair(k, j).astype(BF16) for j in range(gsz)]
        kk = [pack(_dot_nt(pair(kb, j).astype(BF16), k2[j])) for j in range(gsz)]
        qk_raw = [pack(_dot_nt(pair(q, j).astype(BF16), k2[j])) for j in range(gsz)]
        diff = [jnp.where(lo, g[2 * j], g[2 * j + 1])
                - jnp.where(lane_row < CHUNK, g_row(j, 0), g_row(j, 1)) for j in range(gsz)]
        decay = [jnp.where(incl, jnp.exp(jnp.where(incl, a, 0.0)), 0.0) for a in diff]
        pw = [jnp.where(strict, -(a * dc), 0.0) for a, dc in zip(kk, decay)]
        x = [eye + a for a in pw]
        p16 = [a.astype(BF16) for a in pw]
        p16 = [_dot(a, blockdiag(a)).astype(BF16) for a in p16]
        for _ in range(4):
            res = [_dot(jnp.concatenate([a, b.astype(BF16)], axis=0), blockdiag(a)) for a, b in zip(p16, x)]
            p16 = [a[:CHUNK].astype(BF16) for a in res]
            x = [a + b[CHUNK:] for a, b in zip(x, res)]
        x = [a + _dot(a.astype(BF16), blockdiag(b)) for a, b in zip(x, p16)]
        qk2 = [a * dc for a, dc in zip(qk_raw, decay)]
        unpack = lambda xs: [(xs[j][:, :CHUNK] if d == 0 else xs[j][:, CHUNK:]).astype(BF16) for j, d in ch]
        xinv = unpack(x)
        qk = unpack(qk2)
        eg = [jnp.exp(a) for a in g]
        rhs = [jnp.concatenate([a * bt, b * e], axis=1).astype(BF16)
               for a, bt, b, e in zip(v, beta, kb, eg)]
        sol = [_dot(a, b).astype(BF16) for a, b in zip(xinv, rhs)]
        kd = [(a * jnp.exp(l - b)).T.astype(BF16) for a, b, l in zip(k, g, gl)]
        kdu_kdw = [_dot(a, b) for a, b in zip(kd, sol)]
        qku_qkw = [_dot(a, b) for a, b in zip(qk, sol)]
        for i, (j, d) in enumerate(ch):
            srows = slice(j * HEAD_DIM, (j + 1) * HEAD_DIM)
            crows = slice(j * CHUNK, (j + 1) * CHUNK)
            r_s[d, srows, :] = kdu_kdw[i][:, :HEAD_DIM]
            kw_s[d, srows, :] = kdu_kdw[i][:, HEAD_DIM:].astype(BF16)
            op_s[d, crows, :] = qku_qkw[i][:, :HEAD_DIM]
            qp_s[d, crows, :] = (q[i] * eg[i] - qku_qkw[i][:, HEAD_DIM:]).astype(BF16)
            gl_s[d, j * SUBLANES:(j + 1) * SUBLANES, :] = jnp.broadcast_to(jnp.exp(gl[i]), (SUBLANES, LANES))

    def scan(gi, buf, carry):
        kw_s, r_s, qp_s, op_s, gl_s = buf
        st = list(carry)
        for j in range(gsz):
            srows = slice(j * HEAD_DIM, (j + 1) * HEAD_DIM)
            crows = slice(j * CHUNK, (j + 1) * CHUNK)
            for d in range(2):
                rows = pl.ds(pl.multiple_of(chunk_of(gi, j, d) * CHUNK, CHUNK), CHUNK)
                st16 = st[d].astype(BF16)
                o_scr[d, rows, :] = _dot(qp_s[d, crows, :], st16) + op_s[d, crows, :]
                gl = gl_s[d, j * SUBLANES:j * SUBLANES + 1, :]
                st[d] = gl * st[d] + (r_s[d, srows, :] - _dot(kw_s[d, srows, :], st16))
        return tuple(st)

    s0 = jnp.zeros((HEAD_DIM, HEAD_DIM), F32)
    prep(0, bufs[0])

    def pair(pi, carry):
        prep(2 * pi + 1, bufs[1])
        carry = scan(2 * pi, bufs[0], carry)
        prep(2 * pi + 2, bufs[0])
        return scan(2 * pi + 1, bufs[1], carry)
    carry = lax.fori_loop(0, ng // 2 - 1, pair, (s0, s0))
    prep(ng - 1, bufs[1])
    carry = scan(ng - 2, bufs[0], carry)
    scan(ng - 1, bufs[1], carry)

    def finish(r, c):
        rows = pl.ds(pl.multiple_of(r * rb, rb), rb)
        o = o_scr[0, rows, :] + o_scr[1, rows, :]
        y = o * lax.rsqrt(jnp.mean(o * o, axis=-1, keepdims=True) + EPS) * gh_ref[...]
        out_ref[rows, :] = (y * _silu(z_ref[rows, :])).astype(BF16)
        return c
    lax.fori_loop(0, nrb, finish, 0)


def _delta(proj3, gates3, gates_t, conv_w, g_head, *, q_blk, z_blk):
    b, t, _ = proj3.shape
    rb = min(t, 256)
    gsz = min(8, t // (2 * CHUNK))
    assert (t // CHUNK) % (2 * gsz) == 0

    def col(off):
        return pl.BlockSpec((None, t, HEAD_DIM), lambda bi, hi: (bi, 0, off + hi))

    def cw(off):
        return pl.BlockSpec((CONV_WIDTH, HEAD_DIM), lambda bi, hi: (0, off + hi))

    operands = [
        pltpu.VMEM((2, gsz * HEAD_DIM, HEAD_DIM), BF16),
        pltpu.VMEM((2, gsz * HEAD_DIM, HEAD_DIM), F32),
        pltpu.VMEM((2, gsz * CHUNK, HEAD_DIM), BF16),
        pltpu.VMEM((2, gsz * CHUNK, HEAD_DIM), F32),
        pltpu.VMEM((2, gsz * SUBLANES, LANES), F32),
    ]
    return pl.pallas_call(
        functools.partial(_delta_kernel, t=t, rb=rb, gsz=gsz),
        grid=(b, N_HEADS),
        in_specs=[
            col(q_blk), col(q_blk + N_HEADS), col(q_blk + 2 * N_HEADS), col(z_blk),
            pl.BlockSpec((None, t, LANES), lambda bi, hi: (bi, 0, 0)),
            pl.BlockSpec((LANES, t), lambda bi, hi: (0, bi)),
            cw(0), cw(N_HEADS), cw(2 * N_HEADS),
            pl.BlockSpec((1, HEAD_DIM), lambda bi, hi: (0, 0)),
        ],
        out_specs=pl.BlockSpec((None, t, HEAD_DIM), lambda bi, hi: (bi, 0, hi)),
        out_shape=jax.ShapeDtypeStruct((b, t, N_HEADS * HEAD_DIM), BF16),
        scratch_shapes=[
            pltpu.VMEM((3, t + 2 * SUBLANES, LANES), F32),
            pltpu.VMEM((3, t, HEAD_DIM), F32),
            pltpu.VMEM((2, t, HEAD_DIM), F32),
        ] + operands + operands,
        compiler_params=pltpu.CompilerParams(
            dimension_semantics=("parallel", "arbitrary"), vmem_limit_bytes=VMEM_LIMIT),
        name="delta",
    )(proj3, proj3, proj3, proj3, gates3, gates_t, conv_w, conv_w, conv_w, g_head)


def _dft_chan_kernel(f_ref, wd_ref, o_ref):
    for g in range(N_GROUPS):
        cols = slice(g * GROUP_DIM, (g + 1) * GROUP_DIM)
        y = _dot(f_ref[:, cols].astype(BF16), wd_ref[...])
        o_ref[0, :, cols] = y[:, :GROUP_DIM].astype(BF16)
        o_ref[1, :, cols] = y[:, GROUP_DIM:].astype(BF16)


def _dft_chan(proj3, wd, *, f_blk, tt):
    b, t, _ = proj3.shape
    fw = N_GROUPS * GROUP_DIM
    return pl.pallas_call(
        _dft_chan_kernel,
        grid=(b, t // tt),
        in_specs=[
            pl.BlockSpec((None, tt, fw), lambda bi, ti: (bi, ti, f_blk)),
            pl.BlockSpec((GROUP_DIM, 2 * GROUP_DIM), lambda bi, ti: (0, 0)),
        ],
        out_specs=pl.BlockSpec((2, tt, fw), lambda bi, ti: (0, ti, bi)),
        out_shape=jax.ShapeDtypeStruct((2, t, b * fw), BF16),
        compiler_params=pltpu.CompilerParams(
            dimension_semantics=("parallel", "parallel"), vmem_limit_bytes=VMEM_LIMIT),
        name="dft_chan",
    )(proj3, wd)


def _scaled_matmul_kernel(a_ref, b_ref, o_ref, *, scale):
    o_ref[...] = (_dot(a_ref[...], b_ref[...]) * scale).astype(o_ref.dtype)


def _scaled_matmul(a, b, *, scale, tm, tn, name):
    m, k = a.shape
    n = b.shape[1]
    return pl.pallas_call(
        functools.partial(_scaled_matmul_kernel, scale=scale),
        grid=(n // tn, m // tm),
        in_specs=[
            pl.BlockSpec((tm, k), lambda j, i: (i, 0)),
            pl.BlockSpec((k, tn), lambda j, i: (0, j)),
        ],
        out_specs=pl.BlockSpec((tm, tn), lambda j, i: (i, j)),
        out_shape=jax.ShapeDtypeStruct((m, n), BF16),
        compiler_params=pltpu.CompilerParams(
            dimension_semantics=("parallel", "parallel"), vmem_limit_bytes=VMEM_LIMIT),
        name=name,
    )(a, b)


@functools.lru_cache(maxsize=None)
def _dft_tables(t, d):
    n = np.arange(t, dtype=np.int64)
    ang = 2.0 * np.pi * ((n[:, None] * n[None, :]) % t) / t
    wt = np.concatenate([np.cos(ang), np.sin(ang)], axis=1).astype(np.float32)
    c = np.arange(d, dtype=np.int64)
    angd = 2.0 * np.pi * ((c[:, None] * c[None, :]) % d) / d
    wd = np.concatenate([np.cos(angd), -np.sin(angd)], axis=1).astype(np.float32)
    return wt, wd


def _merge_kernel(og_ref, fr_ref, ga_ref, gf_ref, x_ref, wdn_ref, wf_ref, wo_ref, g_ref,
                  x1_ref, h2_ref):
    ya = _dot(og_ref[...], wdn_ref[...])
    yf = _dot(fr_ref[...], wf_ref[...])
    merged = _sigmoid(ga_ref[...]) * ya + _sigmoid(gf_ref[...]) * yf
    x1 = x_ref[...] + _dot(merged.astype(BF16), wo_ref[...])
    x1_ref[...] = x1
    ms = jnp.mean(x1 * x1, axis=-1, keepdims=True)
    h2_ref[...] = (x1 * lax.rsqrt(ms + EPS) * g_ref[...]).astype(BF16)


def _merge(og2, fr, proj2, x2d, w_dn, w_f, w_o, g_mlp, *, t, tm):
    m, d = x2d.shape
    bw = og2.shape[1]
    tpb = t // tm

    def const(shape):
        return pl.BlockSpec(shape, lambda i: (0, 0), pipeline_mode=pl.Buffered(1))

    return pl.pallas_call(
        _merge_kernel,
        grid=(m // tm,),
        in_specs=[
            pl.BlockSpec((tm, bw), lambda i: (i, 0)),
            pl.BlockSpec((tm, bw), lambda i: (i % tpb, i // tpb)),
            pl.BlockSpec((tm, d), lambda i: (i, 0)),
            pl.BlockSpec((tm, d), lambda i: (i, 1)),
            pl.BlockSpec((tm, d), lambda i: (i, 0)),
            const((bw, d)), const((bw, d)), const((d, d)), const((1, d)),
        ],
        out_specs=[pl.BlockSpec((tm, d), lambda i: (i, 0)), pl.BlockSpec((tm, d), lambda i: (i, 0))],
        out_shape=[jax.ShapeDtypeStruct((m, d), F32), jax.ShapeDtypeStruct((m, d), BF16)],
        compiler_params=pltpu.CompilerParams(
            dimension_semantics=("parallel",), vmem_limit_bytes=VMEM_LIMIT),
        name="merge",
    )(og2, fr, proj2, proj2, x2d, w_dn, w_f, w_o, g_mlp)


def _mlp_kernel(h_ref, wu_ref, wd_ref, x1_ref, g_ref, o_ref):
    j = pl.program_id(1)

    @pl.when(j == 0)
    def _():
        o_ref[...] = x1_ref[...]

    a = jnp.maximum(_dot(h_ref[...], wu_ref[...]), 0.0)
    o_ref[...] += _dot((a * a).astype(BF16), wd_ref[...])

    @pl.when(j == pl.num_programs(1) - 1)
    def _():
        y = o_ref[...]
        ms = jnp.mean(y * y, axis=-1, keepdims=True)
        o_ref[...] = y * lax.rsqrt(ms + EPS) * g_ref[...]


def _mlp(h2, w_up, w_down, x1, g_final, *, tm, tf):
    m, d = x1.shape
    ff = w_up.shape[1]
    return pl.pallas_call(
        _mlp_kernel,
        grid=(m // tm, ff // tf),
        in_specs=[
            pl.BlockSpec((tm, d), lambda i, j: (i, 0)),
            pl.BlockSpec((d, tf), lambda i, j: (0, j)),
            pl.BlockSpec((tf, d), lambda i, j: (j, 0)),
            pl.BlockSpec((tm, d), lambda i, j: (i, 0)),
            pl.BlockSpec((1, d), lambda i, j: (0, 0)),
        ],
        out_specs=pl.BlockSpec((tm, d), lambda i, j: (i, 0)),
        out_shape=jax.ShapeDtypeStruct((m, d), F32),
        compiler_params=pltpu.CompilerParams(
            dimension_semantics=("parallel", "arbitrary"), vmem_limit_bytes=VMEM_LIMIT),
        name="mlp",
    )(h2, w_up, w_down, x1, g_final)


def _largest_divisor(n, cap, multiple):
    return max(c for c in range(multiple, cap + 1, multiple) if n % c == 0)


def _layer(x, g_mix, w_in, conv_w, a_log_f, a_log_b, dt_f, dt_b, g_head, w_dn_up, w_fourier, w_o,
           g_mlp, w_mlp_up, w_mlp_down, g_out):
    b, t, d = x.shape
    m = b * t
    dn = N_HEADS * HEAD_DIM
    fw = N_GROUPS * GROUP_DIM
    o_q, o_k, o_v, o_z, o_f, o_s, o_g = np.cumsum([0, dn, dn, dn, dn, fw, 4 * N_HEADS])
    w_t = _cast_rows(jnp.swapaxes(w_in, 0, 1), tr=_largest_divisor(w_in.shape[1], 1024, 32))
    w_scal = jnp.pad(w_t[o_s:o_g].T, ((0, 0), (0, LANES - 4 * N_HEADS)))
    tn = 512
    row_starts = tuple(range(int(o_g), w_t.shape[0], tn)) + tuple(range(int(o_q), int(o_s), tn))
    q_blk = (2 * d) // HEAD_DIM
    z_blk = q_blk + 3 * N_HEADS
    f_blk = (2 * d + 4 * dn) // fw

    def gate_row(fwd, bwd):
        return jnp.pad(jnp.concatenate([fwd, bwd]).astype(F32),
                       (2 * N_HEADS, LANES - 4 * N_HEADS)).reshape(1, LANES)

    x2d = x.reshape(m, d)
    proj, gates, gates_t = _in_proj(x2d, g_mix.reshape(1, d), w_t, w_scal, gate_row(a_log_f, a_log_b),
                                    gate_row(dt_f, dt_b), tm=min(m, 1024), tn=tn, row_starts=row_starts)
    proj3 = proj.reshape(b, t, proj.shape[1])

    og = _delta(proj3, gates.reshape(b, t, LANES), gates_t, conv_w, g_head.reshape(1, HEAD_DIM),
                q_blk=q_blk, z_blk=z_blk)

    wt, wd = _dft_tables(t, GROUP_DIM)
    g2 = _dft_chan(proj3, jnp.asarray(wd).astype(BF16), f_blk=f_blk, tt=min(t, 512))
    fr = _scaled_matmul(jnp.asarray(wt).astype(BF16), g2.reshape(2 * t, b * fw),
                        scale=float((t * GROUP_DIM) ** -0.5), tm=min(t, 512), tn=fw, name="dft_seq")

    x1, h2 = _merge(og.reshape(m, dn), fr, proj, x2d, w_dn_up.astype(BF16), w_fourier.astype(BF16),
                    w_o.astype(BF16), g_mlp.reshape(1, d), t=t, tm=min(t, 256))
    return _mlp(h2, w_mlp_up.astype(BF16), w_mlp_down.astype(BF16), x1, g_out.reshape(1, d),
                tm=min(m, 512), tf=1024)


def kernel(x, g_mix, w_in, conv_w, a_log_fwd, a_log_bwd, dt_bias_fwd, dt_bias_bwd, g_dn_head, w_dn_up,
           w_fourier, w_o, g_mlp, w_mlp_up, w_mlp_down, g_final):
    depth = g_mix.shape[0]
    assert depth == 1, "the final rmsnorm is fused into the last block's MLP kernel"
    b, t, d = x.shape
    out = _layer(x, g_mix[0], w_in[0], conv_w[0], a_log_fwd[0], a_log_bwd[0], dt_bias_fwd[0],
                 dt_bias_bwd[0], g_dn_head[0], w_dn_up[0], w_fourier[0], w_o[0], g_mlp[0],
                 w_mlp_up[0], w_mlp_down[0], g_final)
    return out.reshape(b, t, d)
```

```python
import functools

import jax
import jax.numpy as jnp
import numpy as np
from jax import lax
from jax.experimental import pallas as pl
from jax.experimental.pallas import tpu as pltpu

F32 = jnp.float32
BF16 = jnp.bfloat16

EPS = 1e-6
N_HEADS = 8
HEAD_DIM = 128
CHUNK = 64
CONV_WIDTH = 5
N_GROUPS = 8
GROUP_DIM = 128
LANES = 128
SUBLANES = 8
VMEM_LIMIT = 56 * 1024 * 1024

_NT = (((1,), (1,)), ((), ()))


def _dot(a, b):
    return jnp.dot(a, b, preferred_element_type=F32)


def _dot_nt(a, b):
    return lax.dot_general(a, b, _NT, preferred_element_type=F32)


def _split3(x):
    hi = x.astype(BF16)
    r = x - hi.astype(F32)
    mid = r.astype(BF16)
    r = r - mid.astype(F32)
    return hi, mid, r.astype(BF16)


def _sigmoid(x):
    return 1.0 / (1.0 + jnp.exp(-x))


def _silu(x):
    return x * _sigmoid(x)


def _softplus(x):
    return jnp.maximum(x, 0.0) + jnp.log1p(jnp.exp(-jnp.abs(x)))


def _cast_kernel(w_ref, o_ref):
    o_ref[...] = w_ref[...].astype(o_ref.dtype)


def _cast_rows(w, *, tr):
    rows, cols = w.shape
    return pl.pallas_call(
        _cast_kernel,
        grid=(rows // tr,),
        in_specs=[pl.BlockSpec((tr, cols), lambda i: (i, 0))],
        out_specs=pl.BlockSpec((tr, cols), lambda i: (i, 0)),
        out_shape=jax.ShapeDtypeStruct((rows, cols), BF16),
        compiler_params=pltpu.CompilerParams(dimension_semantics=("parallel",), vmem_limit_bytes=VMEM_LIMIT),
        name="cast_w_in",
    )(w)


GATE_ROWS = 256


def _in_proj_kernel(st_ref, x_ref, g_ref, w_ref, ws_ref, alog_ref, dt_ref, o_ref, gates_ref, gates_t_ref,
                    h_scr, *, tm, rb):
    j = pl.program_id(1)

    @pl.when(j == 0)
    def _():
        def body(r, c):
            rows = pl.ds(pl.multiple_of(r * rb, rb), rb)
            xf = x_ref[rows, :]
            ms = jnp.mean(xf * xf, axis=-1, keepdims=True)
            h_scr[rows, :] = (xf * lax.rsqrt(ms + EPS) * g_ref[...]).astype(BF16)
            return c
        lax.fori_loop(0, tm // rb, body, 0)

        gr = min(tm, GATE_ROWS)
        ri = lax.broadcasted_iota(jnp.int32, (gr, gr), 0)
        ci = lax.broadcasted_iota(jnp.int32, (gr, gr), 1)
        same = (ri // CHUNK) == (ci // CHUNK)
        ltri = jnp.where(same & (ri >= ci), 1.0, 0.0).astype(BF16)
        utri = jnp.where(same & (ri <= ci), 1.0, 0.0).astype(BF16)
        lane = lax.broadcasted_iota(jnp.int32, (gr, LANES), 1)
        neg_a = -jnp.exp(alog_ref[...])

        def gates(r, c):
            rows = pl.ds(pl.multiple_of(r * gr, gr), gr)
            s = _dot(h_scr[rows, :], ws_ref[...])
            g3 = _split3(neg_a * _softplus(s + dt_ref[...]))
            gc_f = _dot(ltri, g3[0]) + _dot(ltri, g3[1]) + _dot(ltri, g3[2])
            gc_b = _dot(utri, g3[0]) + _dot(utri, g3[1]) + _dot(utri, g3[2])
            tile = jnp.where(lane < 2 * N_HEADS, _sigmoid(s), jnp.where(lane < 3 * N_HEADS, gc_f, gc_b))
            gates_ref[rows, :] = tile
            gates_t_ref[:, rows] = tile.T
            return c
        lax.fori_loop(0, tm // gr, gates, 0)

    o_ref[...] = _dot_nt(h_scr[...], w_ref[...])


def _in_proj(x2d, g, w_t, w_scal, alog_row, dt_row, *, tm, tn, row_starts):
    m, d = x2d.shape
    starts = jnp.asarray(row_starts, jnp.int32)
    n_steps = len(row_starts)
    return pl.pallas_call(
        functools.partial(_in_proj_kernel, tm=tm, rb=min(tm, 128)),
        grid_spec=pltpu.PrefetchScalarGridSpec(
            num_scalar_prefetch=1,
            grid=(m // tm, n_steps),
            in_specs=[
                pl.BlockSpec((tm, d), lambda i, j, st: (i, 0)),
                pl.BlockSpec((1, d), lambda i, j, st: (0, 0)),
                pl.BlockSpec((pl.Element(tn), pl.Element(d)), lambda i, j, st: (pl.multiple_of(st[j], 32), 0)),
                pl.BlockSpec((d, LANES), lambda i, j, st: (0, 0)),
                pl.BlockSpec((1, LANES), lambda i, j, st: (0, 0)),
                pl.BlockSpec((1, LANES), lambda i, j, st: (0, 0)),
            ],
            out_specs=[
                pl.BlockSpec((tm, tn), lambda i, j, st: (i, j)),
                pl.BlockSpec((tm, LANES), lambda i, j, st: (i, 0)),
                pl.BlockSpec((LANES, tm), lambda i, j, st: (0, i)),
            ],
            scratch_shapes=[pltpu.VMEM((tm, d), BF16)],
        ),
        out_shape=[jax.ShapeDtypeStruct((m, n_steps * tn), F32), jax.ShapeDtypeStruct((m, LANES), F32),
                   jax.ShapeDtypeStruct((LANES, m), F32)],
        compiler_params=pltpu.CompilerParams(
            dimension_semantics=("parallel", "arbitrary"), vmem_limit_bytes=VMEM_LIMIT),
        name="in_proj",
    )(starts, x2d, g, w_t, w_scal, alog_row, dt_row)


def _delta_kernel(*refs, t, rb, gsz, n_cast):
    (q_ref, k_ref, v_ref, z_ref, gates_ref, gates_t_ref, cwq_ref, cwk_ref, cwv_ref, gh_ref), refs = (
        refs[:10], refs[10:])
    cast_in, (out_ref,), cast_out, refs = (
        refs[:n_cast], refs[n_cast:n_cast + 1], refs[n_cast + 1:2 * n_cast + 1], refs[2 * n_cast + 1:])
    pad_scr, qkv_scr, o_scr, kw0, r0, qp0, op0, gl0, kw1, r1, qp1, op1, gl1 = refs

    for src, dst in zip(cast_in, cast_out):
        dst[...] = src[...].astype(dst.dtype)

    nc = t // CHUNK
    ng = nc // gsz
    nrb = t // rb
    h = pl.program_id(1)

    zero8 = jnp.zeros((SUBLANES, LANES), F32)
    half = (CONV_WIDTH - 1) // 2
    streams = ((q_ref, cwq_ref), (k_ref, cwk_ref), (v_ref, cwv_ref))
    for idx in range(3):
        pad_scr[idx, 0:SUBLANES, :] = zero8
        pad_scr[idx, t + SUBLANES:t + 2 * SUBLANES, :] = zero8

    def copy_in(r, c):
        r0_ = pl.multiple_of(r * rb, rb)
        for idx, (src_ref, _) in enumerate(streams):
            pad_scr[idx, pl.ds(pl.multiple_of(r0_ + SUBLANES, SUBLANES), rb), :] = src_ref[pl.ds(r0_, rb), :]
        return c
    lax.fori_loop(0, nrb, copy_in, 0)

    def conv(r, c):
        r0_ = pl.multiple_of(r * rb, rb)
        for idx, (_, cw_ref) in enumerate(streams):
            win = pad_scr[idx, pl.ds(r0_, rb + 2 * SUBLANES), :]
            acc = None
            for i in range(CONV_WIDTH):
                off = SUBLANES - half + i
                term = cw_ref[i:i + 1, :] * win[off:off + rb, :]
                acc = term if acc is None else acc + term
            y = _silu(acc)
            if idx < 2:
                y = y * lax.rsqrt(jnp.sum(y * y, axis=-1, keepdims=True) + EPS)
            if idx == 0:
                y = y * (HEAD_DIM ** -0.5)
            qkv_scr[idx, pl.ds(r0_, rb), :] = y
        return c
    lax.fori_loop(0, nrb, conv, 0)

    lane = lax.broadcasted_iota(jnp.int32, (CHUNK, LANES), 1)
    ri = lax.broadcasted_iota(jnp.int32, (CHUNK, LANES), 0)
    lo = lane < CHUNK
    ci = jnp.where(lo, lane, lane - CHUNK)
    ahead = jnp.where(lo, ri - ci, ci - ri)
    incl = ahead >= 0
    strict = ahead > 0
    eye = jnp.where(ri == ci, 1.0, 0.0)
    lane_row = lax.broadcasted_iota(jnp.int32, (1, LANES), 1)
    head_row = lax.broadcasted_iota(jnp.int32, (N_HEADS, LANES), 0)
    bufs = ((kw0, r0, qp0, op0, gl0), (kw1, r1, qp1, op1, gl1))

    def chunk_of(gi, j, d):
        return gi * gsz + j if d == 0 else nc - 1 - (gi * gsz + j)

    def column(tile, c):
        return jnp.sum(jnp.where(lane == c, tile, 0.0), axis=-1, keepdims=True)

    def pack(m):
        return jnp.where(lo, m[:CHUNK], m[CHUNK:])

    def blockdiag(m):
        zero = jnp.zeros_like(m)
        return jnp.concatenate([jnp.where(lo, m, zero), jnp.where(lo, zero, m)], axis=0)

    def prep(gi, buf):
        kw_s, r_s, qp_s, op_s, gl_s = buf
        ch = [(j, d) for j in range(gsz) for d in range(2)]
        rows = [pl.ds(pl.multiple_of(chunk_of(gi, j, d) * CHUNK, CHUNK), CHUNK) for j, d in ch]
        q = [qkv_scr[0, r, :] for r in rows]
        k = [qkv_scr[1, r, :] for r in rows]
        v = [qkv_scr[2, r, :] for r in rows]
        gt = [gates_ref[r, :] for r in rows]
        beta = [column(a, d * N_HEADS + h) for a, (j, d) in zip(gt, ch)]
        g = [column(a, (2 + d) * N_HEADS + h) for a, (j, d) in zip(gt, ch)]
        gl = [a[CHUNK - 1:CHUNK, :] if d == 0 else a[0:1, :] for a, (j, d) in zip(g, ch)]
        kb = [a * bt for a, bt in zip(k, beta)]

        def g_row(j, d):
            c = chunk_of(gi, j, d)
            odd = (j % 2) if d == 0 else ((nc - 1 - j) % 2)
            win = gates_t_ref[(2 + d) * N_HEADS:(3 + d) * N_HEADS,
                              pl.ds(pl.multiple_of((c - odd) * CHUNK, 2 * CHUNK), 2 * CHUNK)]
            win = jnp.sum(jnp.where(head_row == h, win, 0.0), axis=0, keepdims=True)
            return pltpu.roll(win, CHUNK, axis=1) if odd != d else win

        pair = lambda xs, j: jnp.concatenate([xs[2 * j], xs[2 * j + 1]], axis=0)
        k2 = [pair(k, j).astype(BF16) for j in range(gsz)]
        kk = [pack(_dot_nt(pair(kb, j).astype(BF16), k2[j])) for j in range(gsz)]
        qk_raw = [pack(_dot_nt(pair(q, j).astype(BF16), k2[j])) for j in range(gsz)]
        diff = [jnp.where(lo, g[2 * j], g[2 * j + 1])
                - jnp.where(lane_row < CHUNK, g_row(j, 0), g_row(j, 1)) for j in range(gsz)]
        decay = [jnp.where(incl, jnp.exp(jnp.where(incl, a, 0.0)), 0.0) for a in diff]
        pw = [jnp.where(strict, -(a * dc), 0.0) for a, dc in zip(kk, decay)]
        x = [eye + a for a in pw]
        p16 = [a.astype(BF16) for a in pw]
        p16 = [_dot(a, blockdiag(a)).astype(BF16) for a in p16]
        for _ in range(4):
            res = [_dot(jnp.concatenate([a, b.astype(BF16)], axis=0), blockdiag(a)) for a, b in zip(p16, x)]
            p16 = [a[:CHUNK].astype(BF16) for a in res]
            x = [a + b[CHUNK:] for a, b in zip(x, res)]
        x = [a + _dot(a.astype(BF16), blockdiag(b)) for a, b in zip(x, p16)]
        qk2 = [a * dc for a, dc in zip(qk_raw, decay)]
        unpack = lambda xs: [(xs[j][:, :CHUNK] if d == 0 else xs[j][:, CHUNK:]).astype(BF16) for j, d in ch]
        xinv = unpack(x)
        qk = unpack(qk2)
        eg = [jnp.exp(a) for a in g]
        rhs = [jnp.concatenate([a * bt, b * e], axis=1).astype(BF16)
               for a, bt, b, e in zip(v, beta, kb, eg)]
        sol = [_dot(a, b).astype(BF16) for a, b in zip(xinv, rhs)]
        kd = [(a * jnp.exp(l - b)).T.astype(BF16) for a, b, l in zip(k, g, gl)]
        kdu_kdw = [_dot(a, b) for a, b in zip(kd, sol)]
        qku_qkw = [_dot(a, b) for a, b in zip(qk, sol)]
        for i, (j, d) in enumerate(ch):
            srows = slice(j * HEAD_DIM, (j + 1) * HEAD_DIM)
            crows = slice(j * CHUNK, (j + 1) * CHUNK)
            r_s[d, srows, :] = kdu_kdw[i][:, :HEAD_DIM]
            kw_s[d, srows, :] = kdu_kdw[i][:, HEAD_DIM:].astype(BF16)
            op_s[d, crows, :] = qku_qkw[i][:, :HEAD_DIM]
            qp_s[d, crows, :] = (q[i] * eg[i] - qku_qkw[i][:, HEAD_DIM:]).astype(BF16)
            gl_s[d, j * SUBLANES:(j + 1) * SUBLANES, :] = jnp.broadcast_to(jnp.exp(gl[i]), (SUBLANES, LANES))

    def scan(gi, buf, carry):
        kw_s, r_s, qp_s, op_s, gl_s = buf
        st = list(carry)
        for j in range(gsz):
            srows = slice(j * HEAD_DIM, (j + 1) * HEAD_DIM)
            crows = slice(j * CHUNK, (j + 1) * CHUNK)
            for d in range(2):
                rows = pl.ds(pl.multiple_of(chunk_of(gi, j, d) * CHUNK, CHUNK), CHUNK)
                st16 = st[d].astype(BF16)
                o_scr[d, rows, :] = _dot(qp_s[d, crows, :], st16) + op_s[d, crows, :]
                gl = gl_s[d, j * SUBLANES:j * SUBLANES + 1, :]
                st[d] = gl * st[d] + (r_s[d, srows, :] - _dot(kw_s[d, srows, :], st16))
        return tuple(st)

    s0 = jnp.zeros((HEAD_DIM, HEAD_DIM), F32)
    prep(0, bufs[0])

    def pair(pi, carry):
        prep(2 * pi + 1, bufs[1])
        carry = scan(2 * pi, bufs[0], carry)
        prep(2 * pi + 2, bufs[0])
        return scan(2 * pi + 1, bufs[1], carry)
    carry = lax.fori_loop(0, ng // 2 - 1, pair, (s0, s0))
    prep(ng - 1, bufs[1])
    carry = scan(ng - 2, bufs[0], carry)
    scan(ng - 1, bufs[1], carry)

    def finish(r, c):
        rows = pl.ds(pl.multiple_of(r * rb, rb), rb)
        o = o_scr[0, rows, :] + o_scr[1, rows, :]
        y = o * lax.rsqrt(jnp.mean(o * o, axis=-1, keepdims=True) + EPS) * gh_ref[...]
        out_ref[rows, :] = (y * _silu(z_ref[rows, :])).astype(BF16)
        return c
    lax.fori_loop(0, nrb, finish, 0)


def _delta(proj3, gates3, gates_t, conv_w, g_head, cast_weights, *, q_blk, z_blk):
    b, t, _ = proj3.shape
    n_steps = b * N_HEADS
    slab_specs = [pl.BlockSpec((w.shape[0] // n_steps, w.shape[1]), lambda bi, hi: (bi * N_HEADS + hi, 0))
                  for w in cast_weights]
    rb = min(t, 256)
    gsz = min(16, t // (2 * CHUNK))
    assert (t // CHUNK) % (2 * gsz) == 0

    def col(off):
        return pl.BlockSpec((None, t, HEAD_DIM), lambda bi, hi: (bi, 0, off + hi))

    def cw(off):
        return pl.BlockSpec((CONV_WIDTH, HEAD_DIM), lambda bi, hi: (0, off + hi))

    operands = [
        pltpu.VMEM((2, gsz * HEAD_DIM, HEAD_DIM), BF16),
        pltpu.VMEM((2, gsz * HEAD_DIM, HEAD_DIM), F32),
        pltpu.VMEM((2, gsz * CHUNK, HEAD_DIM), BF16),
        pltpu.VMEM((2, gsz * CHUNK, HEAD_DIM), F32),
        pltpu.VMEM((2, gsz * SUBLANES, LANES), F32),
    ]
    return pl.pallas_call(
        functools.partial(_delta_kernel, t=t, rb=rb, gsz=gsz, n_cast=len(cast_weights)),
        grid=(b, N_HEADS),
        in_specs=[
            col(q_blk), col(q_blk + N_HEADS), col(q_blk + 2 * N_HEADS), col(z_blk),
            pl.BlockSpec((None, t, LANES), lambda bi, hi: (bi, 0, 0)),
            pl.BlockSpec((LANES, t), lambda bi, hi: (0, bi)),
            cw(0), cw(N_HEADS), cw(2 * N_HEADS),
            pl.BlockSpec((1, HEAD_DIM), lambda bi, hi: (0, 0)),
        ] + slab_specs,
        out_specs=[pl.BlockSpec((None, t, HEAD_DIM), lambda bi, hi: (bi, 0, hi))] + slab_specs,
        out_shape=[jax.ShapeDtypeStruct((b, t, N_HEADS * HEAD_DIM), BF16)]
        + [jax.ShapeDtypeStruct(w.shape, BF16) for w in cast_weights],
        scratch_shapes=[
            pltpu.VMEM((3, t + 2 * SUBLANES, LANES), F32),
            pltpu.VMEM((3, t, HEAD_DIM), F32),
            pltpu.VMEM((2, t, HEAD_DIM), F32),
        ] + operands + operands,
        compiler_params=pltpu.CompilerParams(
            dimension_semantics=("parallel", "arbitrary"), vmem_limit_bytes=VMEM_LIMIT),
        name="delta",
    )(proj3, proj3, proj3, proj3, gates3, gates_t, conv_w, conv_w, conv_w, g_head, *cast_weights)


def _dft_chan_kernel(f_ref, wd_ref, o_ref):
    for g in range(N_GROUPS):
        cols = slice(g * GROUP_DIM, (g + 1) * GROUP_DIM)
        y = _dot(f_ref[:, cols].astype(BF16), wd_ref[...])
        o_ref[0, :, cols] = y[:, :GROUP_DIM].astype(BF16)
        o_ref[1, :, cols] = y[:, GROUP_DIM:].astype(BF16)


def _dft_chan(proj3, wd, *, f_blk, tt):
    b, t, _ = proj3.shape
    fw = N_GROUPS * GROUP_DIM
    return pl.pallas_call(
        _dft_chan_kernel,
        grid=(b, t // tt),
        in_specs=[
            pl.BlockSpec((None, tt, fw), lambda bi, ti: (bi, ti, f_blk)),
            pl.BlockSpec((GROUP_DIM, 2 * GROUP_DIM), lambda bi, ti: (0, 0)),
        ],
        out_specs=pl.BlockSpec((2, tt, fw), lambda bi, ti: (0, ti, bi)),
        out_shape=jax.ShapeDtypeStruct((2, t, b * fw), BF16),
        compiler_params=pltpu.CompilerParams(
            dimension_semantics=("parallel", "parallel"), vmem_limit_bytes=VMEM_LIMIT),
        name="dft_chan",
    )(proj3, wd)


def _scaled_matmul_kernel(a_ref, b_ref, o_ref, *, scale):
    o_ref[...] = (_dot(a_ref[...], b_ref[...]) * scale).astype(o_ref.dtype)


def _scaled_matmul(a, b, *, scale, tm, tn, name):
    m, k = a.shape
    n = b.shape[1]
    return pl.pallas_call(
        functools.partial(_scaled_matmul_kernel, scale=scale),
        grid=(n // tn, m // tm),
        in_specs=[
            pl.BlockSpec((tm, k), lambda j, i: (i, 0)),
            pl.BlockSpec((k, tn), lambda j, i: (0, j)),
        ],
        out_specs=pl.BlockSpec((tm, tn), lambda j, i: (i, j)),
        out_shape=jax.ShapeDtypeStruct((m, n), BF16),
        compiler_params=pltpu.CompilerParams(
            dimension_semantics=("parallel", "parallel"), vmem_limit_bytes=VMEM_LIMIT),
        name=name,
    )(a, b)


@functools.lru_cache(maxsize=None)
def _dft_tables(t, d):
    n = np.arange(t, dtype=np.int64)
    ang = 2.0 * np.pi * ((n[:, None] * n[None, :]) % t) / t
    wt = np.concatenate([np.cos(ang), np.sin(ang)], axis=1).astype(np.float32)
    c = np.arange(d, dtype=np.int64)
    angd = 2.0 * np.pi * ((c[:, None] * c[None, :]) % d) / d
    wd = np.concatenate([np.cos(angd), -np.sin(angd)], axis=1).astype(np.float32)
    return wt, wd


def _merge_kernel(og_ref, fr_ref, ga_ref, gf_ref, x_ref, wdn_ref, wf_ref, wo_ref, g_ref,
                  x1_ref, h2_ref):
    ya = _dot(og_ref[...], wdn_ref[...])
    yf = _dot(fr_ref[...], wf_ref[...])
    merged = _sigmoid(ga_ref[...]) * ya + _sigmoid(gf_ref[...]) * yf
    x1 = x_ref[...] + _dot(merged.astype(BF16), wo_ref[...])
    x1_ref[...] = x1
    ms = jnp.mean(x1 * x1, axis=-1, keepdims=True)
    h2_ref[...] = (x1 * lax.rsqrt(ms + EPS) * g_ref[...]).astype(BF16)


def _merge(og2, fr, proj2, x2d, w_dn, w_f, w_o, g_mlp, *, t, tm):
    m, d = x2d.shape
    bw = og2.shape[1]
    tpb = t // tm

    def const(shape):
        return pl.BlockSpec(shape, lambda i: (0, 0), pipeline_mode=pl.Buffered(1))

    return pl.pallas_call(
        _merge_kernel,
        grid=(m // tm,),
        in_specs=[
            pl.BlockSpec((tm, bw), lambda i: (i, 0)),
            pl.BlockSpec((tm, bw), lambda i: (i % tpb, i // tpb)),
            pl.BlockSpec((tm, d), lambda i: (i, 0)),
            pl.BlockSpec((tm, d), lambda i: (i, 1)),
            pl.BlockSpec((tm, d), lambda i: (i, 0)),
            const((bw, d)), const((bw, d)), const((d, d)), const((1, d)),
        ],
        out_specs=[pl.BlockSpec((tm, d), lambda i: (i, 0)), pl.BlockSpec((tm, d), lambda i: (i, 0))],
        out_shape=[jax.ShapeDtypeStruct((m, d), F32), jax.ShapeDtypeStruct((m, d), BF16)],
        compiler_params=pltpu.CompilerParams(
            dimension_semantics=("parallel",), vmem_limit_bytes=VMEM_LIMIT),
        name="merge",
    )(og2, fr, proj2, proj2, x2d, w_dn, w_f, w_o, g_mlp)


def _mlp_kernel(h_ref, wu_ref, wd_ref, x1_ref, g_ref, o_ref):
    j = pl.program_id(1)

    @pl.when(j == 0)
    def _():
        o_ref[...] = x1_ref[...]

    a = jnp.maximum(_dot(h_ref[...], wu_ref[...]), 0.0)
    o_ref[...] += _dot((a * a).astype(BF16), wd_ref[...])

    @pl.when(j == pl.num_programs(1) - 1)
    def _():
        y = o_ref[...]
        ms = jnp.mean(y * y, axis=-1, keepdims=True)
        o_ref[...] = y * lax.rsqrt(ms + EPS) * g_ref[...]


def _mlp(h2, w_up, w_down, x1, g_final, *, tm, tf):
    m, d = x1.shape
    ff = w_up.shape[1]
    return pl.pallas_call(
        _mlp_kernel,
        grid=(m // tm, ff // tf),
        in_specs=[
            pl.BlockSpec((tm, d), lambda i, j: (i, 0)),
            pl.BlockSpec((d, tf), lambda i, j: (0, j)),
            pl.BlockSpec((tf, d), lambda i, j: (j, 0)),
            pl.BlockSpec((tm, d), lambda i, j: (i, 0)),
            pl.BlockSpec((1, d), lambda i, j: (0, 0)),
        ],
        out_specs=pl.BlockSpec((tm, d), lambda i, j: (i, 0)),
        out_shape=jax.ShapeDtypeStruct((m, d), F32),
        compiler_params=pltpu.CompilerParams(
            dimension_semantics=("parallel", "arbitrary"), vmem_limit_bytes=VMEM_LIMIT),
        name="mlp",
    )(h2, w_up, w_down, x1, g_final)


def _largest_divisor(n, cap, multiple):
    return max(c for c in range(multiple, cap + 1, multiple) if n % c == 0)


def _layer(x, g_mix, w_in, conv_w, a_log_f, a_log_b, dt_f, dt_b, g_head, w_dn_up, w_fourier, w_o,
           g_mlp, w_mlp_up, w_mlp_down, g_out):
    b, t, d = x.shape
    m = b * t
    dn = N_HEADS * HEAD_DIM
    fw = N_GROUPS * GROUP_DIM
    o_q, o_k, o_v, o_z, o_f, o_s, o_g = np.cumsum([0, dn, dn, dn, dn, fw, 4 * N_HEADS])
    w_t = _cast_rows(jnp.swapaxes(w_in, 0, 1), tr=_largest_divisor(w_in.shape[1], 1024, 32))
    w_scal = jnp.pad(w_t[o_s:o_g].T, ((0, 0), (0, LANES - 4 * N_HEADS)))
    tn = 1024
    row_starts = tuple(range(int(o_g), w_t.shape[0], tn)) + tuple(range(int(o_q), int(o_s), tn))
    q_blk = (2 * d) // HEAD_DIM
    z_blk = q_blk + 3 * N_HEADS
    f_blk = (2 * d + 4 * dn) // fw

    def gate_row(fwd, bwd):
        return jnp.pad(jnp.concatenate([fwd, bwd]).astype(F32),
                       (2 * N_HEADS, LANES - 4 * N_HEADS)).reshape(1, LANES)

    x2d = x.reshape(m, d)
    proj, gates, gates_t = _in_proj(x2d, g_mix.reshape(1, d), w_t, w_scal, gate_row(a_log_f, a_log_b),
                                    gate_row(dt_f, dt_b), tm=min(m, 1024), tn=tn, row_starts=row_starts)
    proj3 = proj.reshape(b, t, proj.shape[1])

    og, w_dn16, w_f16, w_o16, w_up16, w_down16 = _delta(
        proj3, gates.reshape(b, t, LANES), gates_t, conv_w, g_head.reshape(1, HEAD_DIM),
        [w_dn_up, w_fourier, w_o, w_mlp_up, w_mlp_down], q_blk=q_blk, z_blk=z_blk)

    wt, wd = _dft_tables(t, GROUP_DIM)
    g2 = _dft_chan(proj3, jnp.asarray(wd).astype(BF16), f_blk=f_blk, tt=min(t, 512))
    fr = _scaled_matmul(jnp.asarray(wt).astype(BF16), g2.reshape(2 * t, b * fw),
                        scale=float((t * GROUP_DIM) ** -0.5), tm=min(t, 512), tn=fw, name="dft_seq")

    x1, h2 = _merge(og.reshape(m, dn), fr, proj, x2d, w_dn16, w_f16, w_o16, g_mlp.reshape(1, d),
                    t=t, tm=min(t, 256))
    return _mlp(h2, w_up16, w_down16, x1, g_out.reshape(1, d), tm=min(m, 512), tf=1024)


def kernel(x, g_mix, w_in, conv_w, a_log_fwd, a_log_bwd, dt_bias_fwd, dt_bias_bwd, g_dn_head, w_dn_up,
           w_fourier, w_o, g_mlp, w_mlp_up, w_mlp_down, g_final):
    depth = g_mix.shape[0]
    assert depth == 1, "the final rmsnorm is fused into the last block's MLP kernel"
    b, t, d = x.shape
    out = _layer(x, g_mix[0], w_in[0], conv_w[0], a_log_fwd[0], a_log_bwd[0], dt_bias_fwd[0],
                 dt_bias_bwd[0], g_dn_head[0], w_dn_up[0], w_fourier[0], w_o[0], g_mlp[0],
                 w_mlp_up[0], w_mlp_down[0], g_final)
    return out.reshape(b, t, d)
```

```python
import functools

import jax
import jax.numpy as jnp
import numpy as np
from jax import lax
from jax.experimental import pallas as pl
from jax.experimental.pallas import tpu as pltpu

F32 = jnp.float32
BF16 = jnp.bfloat16

EPS = 1e-6
N_HEADS = 8
HEAD_DIM = 128
CHUNK = 64
CONV_WIDTH = 5
N_GROUPS = 8
GROUP_DIM = 128
LANES = 128
SUBLANES = 8
VMEM_LIMIT = 56 * 1024 * 1024

_NT = (((1,), (1,)), ((), ()))


def _dot(a, b):
    return jnp.dot(a, b, preferred_element_type=F32)


def _dot_nt(a, b):
    return lax.dot_general(a, b, _NT, preferred_element_type=F32)


def _split3(x):
    hi = x.astype(BF16)
    r = x - hi.astype(F32)
    mid = r.astype(BF16)
    r = r - mid.astype(F32)
    return hi, mid, r.astype(BF16)


def _sigmoid(x):
    return 1.0 / (1.0 + jnp.exp(-x))


def _silu(x):
    return x * _sigmoid(x)


def _softplus(x):
    return jnp.maximum(x, 0.0) + jnp.log1p(jnp.exp(-jnp.abs(x)))


def _cast_kernel(w_ref, o_ref):
    o_ref[...] = w_ref[...].astype(o_ref.dtype)


def _cast_rows(w, *, tr):
    rows, cols = w.shape
    return pl.pallas_call(
        _cast_kernel,
        grid=(rows // tr,),
        in_specs=[pl.BlockSpec((tr, cols), lambda i: (i, 0))],
        out_specs=pl.BlockSpec((tr, cols), lambda i: (i, 0)),
        out_shape=jax.ShapeDtypeStruct((rows, cols), BF16),
        compiler_params=pltpu.CompilerParams(dimension_semantics=("parallel",), vmem_limit_bytes=VMEM_LIMIT),
        name="cast_w_in",
    )(w)


GATE_ROWS = 256


def _in_proj_kernel(st_ref, x_ref, g_ref, w_ref, ws_ref, alog_ref, dt_ref, o_ref, gates_ref, gates_t_ref,
                    h_scr, *, tm, rb):
    j = pl.program_id(1)

    @pl.when(j == 0)
    def _():
        def body(r, c):
            rows = pl.ds(pl.multiple_of(r * rb, rb), rb)
            xf = x_ref[rows, :]
            ms = jnp.mean(xf * xf, axis=-1, keepdims=True)
            h_scr[rows, :] = (xf * lax.rsqrt(ms + EPS) * g_ref[...]).astype(BF16)
            return c
        lax.fori_loop(0, tm // rb, body, 0)

        gr = min(tm, GATE_ROWS)
        ri = lax.broadcasted_iota(jnp.int32, (gr, gr), 0)
        ci = lax.broadcasted_iota(jnp.int32, (gr, gr), 1)
        same = (ri // CHUNK) == (ci // CHUNK)
        ltri = jnp.where(same & (ri >= ci), 1.0, 0.0).astype(BF16)
        utri = jnp.where(same & (ri <= ci), 1.0, 0.0).astype(BF16)
        lane = lax.broadcasted_iota(jnp.int32, (gr, LANES), 1)
        neg_a = -jnp.exp(alog_ref[...])

        def gates(r, c):
            rows = pl.ds(pl.multiple_of(r * gr, gr), gr)
            s = _dot(h_scr[rows, :], ws_ref[...])
            g3 = _split3(neg_a * _softplus(s + dt_ref[...]))
            gc_f = _dot(ltri, g3[0]) + _dot(ltri, g3[1]) + _dot(ltri, g3[2])
            gc_b = _dot(utri, g3[0]) + _dot(utri, g3[1]) + _dot(utri, g3[2])
            tile = jnp.where(lane < 2 * N_HEADS, _sigmoid(s), jnp.where(lane < 3 * N_HEADS, gc_f, gc_b))
            gates_ref[rows, :] = tile
            gates_t_ref[:, rows] = tile.T
            return c
        lax.fori_loop(0, tm // gr, gates, 0)

    o_ref[...] = _dot_nt(h_scr[...], w_ref[...])


def _in_proj(x2d, g, w_t, w_scal, alog_row, dt_row, *, tm, tn, row_starts):
    m, d = x2d.shape
    starts = jnp.asarray(row_starts, jnp.int32)
    n_steps = len(row_starts)
    return pl.pallas_call(
        functools.partial(_in_proj_kernel, tm=tm, rb=min(tm, 128)),
        grid_spec=pltpu.PrefetchScalarGridSpec(
            num_scalar_prefetch=1,
            grid=(m // tm, n_steps),
            in_specs=[
                pl.BlockSpec((tm, d), lambda i, j, st: (i, 0)),
                pl.BlockSpec((1, d), lambda i, j, st: (0, 0)),
                pl.BlockSpec((pl.Element(tn), pl.Element(d)), lambda i, j, st: (pl.multiple_of(st[j], 32), 0)),
                pl.BlockSpec((d, LANES), lambda i, j, st: (0, 0)),
                pl.BlockSpec((1, LANES), lambda i, j, st: (0, 0)),
                pl.BlockSpec((1, LANES), lambda i, j, st: (0, 0)),
            ],
            out_specs=[
                pl.BlockSpec((tm, tn), lambda i, j, st: (i, j)),
                pl.BlockSpec((tm, LANES), lambda i, j, st: (i, 0)),
                pl.BlockSpec((LANES, tm), lambda i, j, st: (0, i)),
            ],
            scratch_shapes=[pltpu.VMEM((tm, d), BF16)],
        ),
        out_shape=[jax.ShapeDtypeStruct((m, n_steps * tn), F32), jax.ShapeDtypeStruct((m, LANES), F32),
                   jax.ShapeDtypeStruct((LANES, m), F32)],
        compiler_params=pltpu.CompilerParams(
            dimension_semantics=("parallel", "arbitrary"), vmem_limit_bytes=VMEM_LIMIT),
        name="in_proj",
    )(starts, x2d, g, w_t, w_scal, alog_row, dt_row)


def _delta_kernel(*refs, t, rb, gsz, n_cast):
    (q_ref, k_ref, v_ref, z_ref, gates_ref, gates_t_ref, cwq_ref, cwk_ref, cwv_ref, gh_ref), refs = (
        refs[:10], refs[10:])
    cast_in, (out_ref,), cast_out, refs = (
        refs[:n_cast], refs[n_cast:n_cast + 1], refs[n_cast + 1:2 * n_cast + 1], refs[2 * n_cast + 1:])
    pad_scr, qkv_scr, o_scr, kw0, r0, qp0, op0, gl0, kw1, r1, qp1, op1, gl1 = refs

    for src, dst in zip(cast_in, cast_out):
        dst[...] = src[...].astype(dst.dtype)

    nc = t // CHUNK
    ng = nc // gsz
    nrb = t // rb
    h = pl.program_id(1)

    zero8 = jnp.zeros((SUBLANES, LANES), F32)
    half = (CONV_WIDTH - 1) // 2
    streams = ((q_ref, cwq_ref), (k_ref, cwk_ref), (v_ref, cwv_ref))
    for idx in range(3):
        pad_scr[idx, 0:SUBLANES, :] = zero8
        pad_scr[idx, t + SUBLANES:t + 2 * SUBLANES, :] = zero8

    def copy_in(r, c):
        r0_ = pl.multiple_of(r * rb, rb)
        for idx, (src_ref, _) in enumerate(streams):
            pad_scr[idx, pl.ds(pl.multiple_of(r0_ + SUBLANES, SUBLANES), rb), :] = src_ref[pl.ds(r0_, rb), :]
        return c
    lax.fori_loop(0, nrb, copy_in, 0)

    for r in range(nrb):
        for idx, (_, cw_ref) in enumerate(streams):
            acc = None
            for i in range(CONV_WIDTH):
                off = r * rb + SUBLANES - half + i
                term = cw_ref[i:i + 1, :] * pad_scr[idx, off:off + rb, :]
                acc = term if acc is None else acc + term
            y = _silu(acc)
            if idx < 2:
                y = y * lax.rsqrt(jnp.sum(y * y, axis=-1, keepdims=True) + EPS)
            if idx == 0:
                y = y * (HEAD_DIM ** -0.5)
            qkv_scr[idx, r * rb:(r + 1) * rb, :] = y

    lane = lax.broadcasted_iota(jnp.int32, (CHUNK, LANES), 1)
    ri = lax.broadcasted_iota(jnp.int32, (CHUNK, LANES), 0)
    lo = lane < CHUNK
    ci = jnp.where(lo, lane, lane - CHUNK)
    ahead = jnp.where(lo, ri - ci, ci - ri)
    incl = ahead >= 0
    strict = ahead > 0
    eye = jnp.where(ri == ci, 1.0, 0.0)
    lane_row = lax.broadcasted_iota(jnp.int32, (1, LANES), 1)
    head_row = lax.broadcasted_iota(jnp.int32, (N_HEADS, LANES), 0)
    bufs = ((kw0, r0, qp0, op0, gl0), (kw1, r1, qp1, op1, gl1))

    def chunk_of(gi, j, d):
        return gi * gsz + j if d == 0 else nc - 1 - (gi * gsz + j)

    def column(tile, c):
        return jnp.sum(jnp.where(lane == c, tile, 0.0), axis=-1, keepdims=True)

    def pack(m):
        return jnp.where(lo, m[:CHUNK], m[CHUNK:])

    def blockdiag(m):
        zero = jnp.zeros_like(m)
        return jnp.concatenate([jnp.where(lo, m, zero), jnp.where(lo, zero, m)], axis=0)

    def prep(gi, buf):
        kw_s, r_s, qp_s, op_s, gl_s = buf
        ch = [(j, d) for j in range(gsz) for d in range(2)]
        rows = [pl.ds(chunk_of(gi, j, d) * CHUNK, CHUNK) for j, d in ch]
        q = [qkv_scr[0, r, :] for r in rows]
        k = [qkv_scr[1, r, :] for r in rows]
        v = [qkv_scr[2, r, :] for r in rows]
        gt = [gates_ref[r, :] for r in rows]
        beta = [column(a, d * N_HEADS + h) for a, (j, d) in zip(gt, ch)]
        g = [column(a, (2 + d) * N_HEADS + h) for a, (j, d) in zip(gt, ch)]
        gl = [a[CHUNK - 1:CHUNK, :] if d == 0 else a[0:1, :] for a, (j, d) in zip(g, ch)]
        kb = [a * bt for a, bt in zip(k, beta)]

        def g_row(j, d):
            c = chunk_of(gi, j, d)
            odd = c % 2
            win = gates_t_ref[(2 + d) * N_HEADS:(3 + d) * N_HEADS, pl.ds((c - odd) * CHUNK, 2 * CHUNK)]
            win = jnp.sum(jnp.where(head_row == h, win, 0.0), axis=0, keepdims=True)
            return pltpu.roll(win, CHUNK, axis=1) if odd != d else win

        pair = lambda xs, j: jnp.concatenate([xs[2 * j], xs[2 * j + 1]], axis=0)
        k2 = [pair(k, j).astype(BF16) for j in range(gsz)]
        kk = [pack(_dot_nt(pair(kb, j).astype(BF16), k2[j])) for j in range(gsz)]
        qk_raw = [pack(_dot_nt(pair(q, j).astype(BF16), k2[j])) for j in range(gsz)]
        diff = [jnp.where(lo, g[2 * j], g[2 * j + 1])
                - jnp.where(lane_row < CHUNK, g_row(j, 0), g_row(j, 1)) for j in range(gsz)]
        decay = [jnp.where(incl, jnp.exp(jnp.where(incl, a, 0.0)), 0.0) for a in diff]
        pw = [jnp.where(strict, -(a * dc), 0.0) for a, dc in zip(kk, decay)]
        x = [eye + a for a in pw]
        p16 = [a.astype(BF16) for a in pw]
        p16 = [_dot(a, blockdiag(a)).astype(BF16) for a in p16]
        for _ in range(4):
            res = [_dot(jnp.concatenate([a, b.astype(BF16)], axis=0), blockdiag(a)) for a, b in zip(p16, x)]
            p16 = [a[:CHUNK].astype(BF16) for a in res]
            x = [a + b[CHUNK:] for a, b in zip(x, res)]
        x = [a + _dot(a.astype(BF16), blockdiag(b)) for a, b in zip(x, p16)]
        qk2 = [a * dc for a, dc in zip(qk_raw, decay)]
        unpack = lambda xs: [(xs[j][:, :CHUNK] if d == 0 else xs[j][:, CHUNK:]).astype(BF16) for j, d in ch]
        xinv = unpack(x)
        qk = unpack(qk2)
        eg = [jnp.exp(a) for a in g]
        rhs = [jnp.concatenate([a * bt, b * e], axis=1).astype(BF16)
               for a, bt, b, e in zip(v, beta, kb, eg)]
        sol = [_dot(a, b).astype(BF16) for a, b in zip(xinv, rhs)]
        kd = [(a * jnp.exp(l - b)).T.astype(BF16) for a, b, l in zip(k, g, gl)]
        kdu_kdw = [_dot(a, b) for a, b in zip(kd, sol)]
        qku_qkw = [_dot(a, b) for a, b in zip(qk, sol)]
        for i, (j, d) in enumerate(ch):
            srows = slice(j * HEAD_DIM, (j + 1) * HEAD_DIM)
            crows = slice(j * CHUNK, (j + 1) * CHUNK)
            r_s[d, srows, :] = kdu_kdw[i][:, :HEAD_DIM]
            kw_s[d, srows, :] = kdu_kdw[i][:, HEAD_DIM:].astype(BF16)
            op_s[d, crows, :] = qku_qkw[i][:, :HEAD_DIM]
            qp_s[d, crows, :] = (q[i] * eg[i] - qku_qkw[i][:, HEAD_DIM:]).astype(BF16)
            gl_s[d, j * SUBLANES:(j + 1) * SUBLANES, :] = jnp.broadcast_to(jnp.exp(gl[i]), (SUBLANES, LANES))

    seen = (set(), set())

    def scan(gi, buf, carry):
        kw_s, r_s, qp_s, op_s, gl_s = buf
        st = list(carry)
        for j in range(gsz):
            srows = slice(j * HEAD_DIM, (j + 1) * HEAD_DIM)
            crows = slice(j * CHUNK, (j + 1) * CHUNK)
            for d in range(2):
                c = chunk_of(gi, j, d)
                rows = pl.ds(c * CHUNK, CHUNK)
                st16 = st[d].astype(BF16)
                o = _dot(qp_s[d, crows, :], st16) + op_s[d, crows, :]
                if c in seen[1 - d]:
                    o = o + o_scr[1 - d, rows, :]
                    y = o * lax.rsqrt(jnp.mean(o * o, axis=-1, keepdims=True) + EPS) * gh_ref[...]
                    out_ref[rows, :] = (y * _silu(z_ref[rows, :])).astype(BF16)
                else:
                    o_scr[d, rows, :] = o
                seen[d].add(c)
                gl = gl_s[d, j * SUBLANES:j * SUBLANES + 1, :]
                st[d] = gl * st[d] + (r_s[d, srows, :] - _dot(kw_s[d, srows, :], st16))
        return tuple(st)

    carry = (jnp.zeros((HEAD_DIM, HEAD_DIM), F32),) * 2
    prep(0, bufs[0])
    for gi in range(1, ng):
        prep(gi, bufs[gi % 2])
        carry = scan(gi - 1, bufs[(gi - 1) % 2], carry)
    scan(ng - 1, bufs[(ng - 1) % 2], carry)
    assert all(len(sn) == nc for sn in seen)


def _delta(proj3, gates3, gates_t, conv_w, g_head, cast_weights, *, q_blk, z_blk):
    b, t, _ = proj3.shape
    n_steps = b * N_HEADS
    slab_specs = [pl.BlockSpec((w.shape[0] // n_steps, w.shape[1]), lambda bi, hi: (bi * N_HEADS + hi, 0))
                  for w in cast_weights]
    rb = min(t, 256)
    gsz = min(16, t // (2 * CHUNK))
    assert (t // CHUNK) % (2 * gsz) == 0

    def col(off):
        return pl.BlockSpec((None, t, HEAD_DIM), lambda bi, hi: (bi, 0, off + hi))

    def cw(off):
        return pl.BlockSpec((CONV_WIDTH, HEAD_DIM), lambda bi, hi: (0, off + hi))

    operands = [
        pltpu.VMEM((2, gsz * HEAD_DIM, HEAD_DIM), BF16),
        pltpu.VMEM((2, gsz * HEAD_DIM, HEAD_DIM), F32),
        pltpu.VMEM((2, gsz * CHUNK, HEAD_DIM), BF16),
        pltpu.VMEM((2, gsz * CHUNK, HEAD_DIM), F32),
        pltpu.VMEM((2, gsz * SUBLANES, LANES), F32),
    ]
    return pl.pallas_call(
        functools.partial(_delta_kernel, t=t, rb=rb, gsz=gsz, n_cast=len(cast_weights)),
        grid=(b, N_HEADS),
        in_specs=[
            col(q_blk), col(q_blk + N_HEADS), col(q_blk + 2 * N_HEADS), col(z_blk),
            pl.BlockSpec((None, t, LANES), lambda bi, hi: (bi, 0, 0)),
            pl.BlockSpec((LANES, t), lambda bi, hi: (0, bi)),
            cw(0), cw(N_HEADS), cw(2 * N_HEADS),
            pl.BlockSpec((1, HEAD_DIM), lambda bi, hi: (0, 0)),
        ] + slab_specs,
        out_specs=[pl.BlockSpec((None, t, HEAD_DIM), lambda bi, hi: (bi, 0, hi))] + slab_specs,
        out_shape=[jax.ShapeDtypeStruct((b, t, N_HEADS * HEAD_DIM), BF16)]
        + [jax.ShapeDtypeStruct(w.shape, BF16) for w in cast_weights],
        scratch_shapes=[
            pltpu.VMEM((3, t + 2 * SUBLANES, LANES), F32),
            pltpu.VMEM((3, t, HEAD_DIM), F32),
            pltpu.VMEM((2, t, HEAD_DIM), F32),
        ] + operands + operands,
        compiler_params=pltpu.CompilerParams(
            dimension_semantics=("parallel", "arbitrary"), vmem_limit_bytes=VMEM_LIMIT),
        name="delta",
    )(proj3, proj3, proj3, proj3, gates3, gates_t, conv_w, conv_w, conv_w, g_head, *cast_weights)


def _dft_chan_kernel(f_ref, wd_ref, o_ref):
    for g in range(N_GROUPS):
        cols = slice(g * GROUP_DIM, (g + 1) * GROUP_DIM)
        y = _dot(f_ref[:, cols].astype(BF16), wd_ref[...])
        o_ref[0, :, cols] = y[:, :GROUP_DIM].astype(BF16)
        o_ref[1, :, cols] = y[:, GROUP_DIM:].astype(BF16)


def _dft_chan(proj3, wd, *, f_blk, tt):
    b, t, _ = proj3.shape
    fw = N_GROUPS * GROUP_DIM
    return pl.pallas_call(
        _dft_chan_kernel,
        grid=(b, t // tt),
        in_specs=[
            pl.BlockSpec((None, tt, fw), lambda bi, ti: (bi, ti, f_blk)),
            pl.BlockSpec((GROUP_DIM, 2 * GROUP_DIM), lambda bi, ti: (0, 0)),
        ],
        out_specs=pl.BlockSpec((2, tt, fw), lambda bi, ti: (0, ti, bi)),
        out_shape=jax.ShapeDtypeStruct((2, t, b * fw), BF16),
        compiler_params=pltpu.CompilerParams(
            dimension_semantics=("parallel", "parallel"), vmem_limit_bytes=VMEM_LIMIT),
        name="dft_chan",
    )(proj3, wd)


def _scaled_matmul_kernel(a_ref, b_ref, o_ref, *, scale):
    o_ref[...] = (_dot(a_ref[...], b_ref[...]) * scale).astype(o_ref.dtype)


def _scaled_matmul(a, b, *, scale, tm, tn, name):
    m, k = a.shape
    n = b.shape[1]
    return pl.pallas_call(
        functools.partial(_scaled_matmul_kernel, scale=scale),
        grid=(n // tn, m // tm),
        in_specs=[
            pl.BlockSpec((tm, k), lambda j, i: (i, 0)),
            pl.BlockSpec((k, tn), lambda j, i: (0, j)),
        ],
        out_specs=pl.BlockSpec((tm, tn), lambda j, i: (i, j)),
        out_shape=jax.ShapeDtypeStruct((m, n), BF16),
        compiler_params=pltpu.CompilerParams(
            dimension_semantics=("parallel", "parallel"), vmem_limit_bytes=VMEM_LIMIT),
        name=name,
    )(a, b)


@functools.lru_cache(maxsize=None)
def _dft_tables(t, d):
    n = np.arange(t, dtype=np.int64)
    ang = 2.0 * np.pi * ((n[:, None] * n[None, :]) % t) / t
    wt = np.concatenate([np.cos(ang), np.sin(ang)], axis=1).astype(np.float32)
    c = np.arange(d, dtype=np.int64)
    angd = 2.0 * np.pi * ((c[:, None] * c[None, :]) % d) / d
    wd = np.concatenate([np.cos(angd), -np.sin(angd)], axis=1).astype(np.float32)
    return wt, wd


def _merge_kernel(og_ref, fr_ref, ga_ref, gf_ref, x_ref, wdn_ref, wf_ref, wo_ref, g_ref,
                  x1_ref, h2_ref):
    ya = _dot(og_ref[...], wdn_ref[...])
    yf = _dot(fr_ref[...], wf_ref[...])
    merged = _sigmoid(ga_ref[...]) * ya + _sigmoid(gf_ref[...]) * yf
    x1 = x_ref[...] + _dot(merged.astype(BF16), wo_ref[...])
    x1_ref[...] = x1
    ms = jnp.mean(x1 * x1, axis=-1, keepdims=True)
    h2_ref[...] = (x1 * lax.rsqrt(ms + EPS) * g_ref[...]).astype(BF16)


def _merge(og2, fr, proj2, x2d, w_dn, w_f, w_o, g_mlp, *, t, tm):
    m, d = x2d.shape
    bw = og2.shape[1]
    tpb = t // tm

    def const(shape):
        return pl.BlockSpec(shape, lambda i: (0, 0), pipeline_mode=pl.Buffered(1))

    return pl.pallas_call(
        _merge_kernel,
        grid=(m // tm,),
        in_specs=[
            pl.BlockSpec((tm, bw), lambda i: (i, 0)),
            pl.BlockSpec((tm, bw), lambda i: (i % tpb, i // tpb)),
            pl.BlockSpec((tm, d), lambda i: (i, 0)),
            pl.BlockSpec((tm, d), lambda i: (i, 1)),
            pl.BlockSpec((tm, d), lambda i: (i, 0)),
            const((bw, d)), const((bw, d)), const((d, d)), const((1, d)),
        ],
        out_specs=[pl.BlockSpec((tm, d), lambda i: (i, 0)), pl.BlockSpec((tm, d), lambda i: (i, 0))],
        out_shape=[jax.ShapeDtypeStruct((m, d), F32), jax.ShapeDtypeStruct((m, d), BF16)],
        compiler_params=pltpu.CompilerParams(
            dimension_semantics=("parallel",), vmem_limit_bytes=VMEM_LIMIT),
        name="merge",
    )(og2, fr, proj2, proj2, x2d, w_dn, w_f, w_o, g_mlp)


def _mlp_kernel(h_ref, wu_ref, wd_ref, x1_ref, g_ref, o_ref):
    j = pl.program_id(1)

    @pl.when(j == 0)
    def _():
        o_ref[...] = x1_ref[...]

    a = jnp.maximum(_dot(h_ref[...], wu_ref[...]), 0.0)
    o_ref[...] += _dot((a * a).astype(BF16), wd_ref[...])

    @pl.when(j == pl.num_programs(1) - 1)
    def _():
        y = o_ref[...]
        ms = jnp.mean(y * y, axis=-1, keepdims=True)
        o_ref[...] = y * lax.rsqrt(ms + EPS) * g_ref[...]


def _mlp(h2, w_up, w_down, x1, g_final, *, tm, tf):
    m, d = x1.shape
    ff = w_up.shape[1]
    return pl.pallas_call(
        _mlp_kernel,
        grid=(m // tm, ff // tf),
        in_specs=[
            pl.BlockSpec((tm, d), lambda i, j: (i, 0)),
            pl.BlockSpec((d, tf), lambda i, j: (0, j)),
            pl.BlockSpec((tf, d), lambda i, j: (j, 0)),
            pl.BlockSpec((tm, d), lambda i, j: (i, 0)),
            pl.BlockSpec((1, d), lambda i, j: (0, 0)),
        ],
        out_specs=pl.BlockSpec((tm, d), lambda i, j: (i, 0)),
        out_shape=jax.ShapeDtypeStruct((m, d), F32),
        compiler_params=pltpu.CompilerParams(
            dimension_semantics=("parallel", "arbitrary"), vmem_limit_bytes=VMEM_LIMIT),
        name="mlp",
    )(h2, w_up, w_down, x1, g_final)


def _largest_divisor(n, cap, multiple):
    return max(c for c in range(multiple, cap + 1, multiple) if n % c == 0)


def _layer(x, g_mix, w_in, conv_w, a_log_f, a_log_b, dt_f, dt_b, g_head, w_dn_up, w_fourier, w_o,
           g_mlp, w_mlp_up, w_mlp_down, g_out):
    b, t, d = x.shape
    m = b * t
    dn = N_HEADS * HEAD_DIM
    fw = N_GROUPS * GROUP_DIM
    o_q, o_k, o_v, o_z, o_f, o_s, o_g = np.cumsum([0, dn, dn, dn, dn, fw, 4 * N_HEADS])
    w_t = _cast_rows(jnp.swapaxes(w_in, 0, 1), tr=_largest_divisor(w_in.shape[1], 1024, 32))
    w_scal = jnp.pad(w_t[o_s:o_g].T, ((0, 0), (0, LANES - 4 * N_HEADS)))
    tn = 1024
    row_starts = tuple(range(int(o_g), w_t.shape[0], tn)) + tuple(range(int(o_q), int(o_s), tn))
    q_blk = (2 * d) // HEAD_DIM
    z_blk = q_blk + 3 * N_HEADS
    f_blk = (2 * d + 4 * dn) // fw

    def gate_row(fwd, bwd):
        return jnp.pad(jnp.concatenate([fwd, bwd]).astype(F32),
                       (2 * N_HEADS, LANES - 4 * N_HEADS)).reshape(1, LANES)

    x2d = x.reshape(m, d)
    proj, gates, gates_t = _in_proj(x2d, g_mix.reshape(1, d), w_t, w_scal, gate_row(a_log_f, a_log_b),
                                    gate_row(dt_f, dt_b), tm=min(m, 1024), tn=tn, row_starts=row_starts)
    proj3 = proj.reshape(b, t, proj.shape[1])

    og, w_dn16, w_f16, w_o16, w_up16, w_down16 = _delta(
        proj3, gates.reshape(b, t, LANES), gates_t, conv_w, g_head.reshape(1, HEAD_DIM),
        [w_dn_up, w_fourier, w_o, w_mlp_up, w_mlp_down], q_blk=q_blk, z_blk=z_blk)

    wt, wd = _dft_tables(t, GROUP_DIM)
    g2 = _dft_chan(proj3, jnp.asarray(wd).astype(BF16), f_blk=f_blk, tt=min(t, 512))
    fr = _scaled_matmul(jnp.asarray(wt).astype(BF16), g2.reshape(2 * t, b * fw),
                        scale=float((t * GROUP_DIM) ** -0.5), tm=min(t, 512), tn=fw, name="dft_seq")

    x1, h2 = _merge(og.reshape(m, dn), fr, proj, x2d, w_dn16, w_f16, w_o16, g_mlp.reshape(1, d),
                    t=t, tm=min(t, 256))
    return _mlp(h2, w_up16, w_down16, x1, g_out.reshape(1, d), tm=min(m, 512), tf=1024)


def kernel(x, g_mix, w_in, conv_w, a_log_fwd, a_log_bwd, dt_bias_fwd, dt_bias_bwd, g_dn_head, w_dn_up,
           w_fourier, w_o, g_mlp, w_mlp_up, w_mlp_down, g_final):
    depth = g_mix.shape[0]
    assert depth == 1, "the final rmsnorm is fused into the last block's MLP kernel"
    b, t, d = x.shape
    out = _layer(x, g_mix[0], w_in[0], conv_w[0], a_log_fwd[0], a_log_bwd[0], dt_bias_fwd[0],
                 dt_bias_bwd[0], g_dn_head[0], w_dn_up[0], w_fourier[0], w_o[0], g_mlp[0],
                 w_mlp_up[0], w_mlp_down[0], g_final)
    return out.reshape(b, t, d)
```

```python
import functools

import jax
import jax.numpy as jnp
import numpy as np
from jax import lax
from jax.experimental import pallas as pl
from jax.experimental.pallas import tpu as pltpu

F32 = jnp.float32
BF16 = jnp.bfloat16

EPS = 1e-6
N_HEADS = 8
HEAD_DIM = 128
CHUNK = 64
CONV_WIDTH = 5
N_GROUPS = 8
GROUP_DIM = 128
LANES = 128
SUBLANES = 8
VMEM_LIMIT = 56 * 1024 * 1024

_NT = (((1,), (1,)), ((), ()))


def _dot(a, b):
    return jnp.dot(a, b, preferred_element_type=F32)


def _dot_nt(a, b):
    return lax.dot_general(a, b, _NT, preferred_element_type=F32)


def _split3(x):
    hi = x.astype(BF16)
    r = x - hi.astype(F32)
    mid = r.astype(BF16)
    r = r - mid.astype(F32)
    return hi, mid, r.astype(BF16)


def _sigmoid(x):
    return 1.0 / (1.0 + jnp.exp(-x))


def _silu(x):
    return x * _sigmoid(x)


def _softplus(x):
    return jnp.maximum(x, 0.0) + jnp.log1p(jnp.exp(-jnp.abs(x)))


def _cast_kernel(w_ref, o_ref):
    o_ref[...] = w_ref[...].astype(o_ref.dtype)


def _cast_rows(w, *, tr):
    rows, cols = w.shape
    return pl.pallas_call(
        _cast_kernel,
        grid=(rows // tr,),
        in_specs=[pl.BlockSpec((tr, cols), lambda i: (i, 0))],
        out_specs=pl.BlockSpec((tr, cols), lambda i: (i, 0)),
        out_shape=jax.ShapeDtypeStruct((rows, cols), BF16),
        compiler_params=pltpu.CompilerParams(dimension_semantics=("parallel",), vmem_limit_bytes=VMEM_LIMIT),
        name="cast_w_in",
    )(w)


GATE_ROWS = 256


def _in_proj_kernel(st_ref, x_ref, g_ref, w_ref, ws_ref, alog_ref, dt_ref, wd_ref, o_ref, gates_ref, gates_t_ref,
                    g2_ref, h_scr, *, tm, rb):
    j = pl.program_id(1)

    @pl.when(j == 0)
    def _():
        def body(r, c):
            rows = pl.ds(pl.multiple_of(r * rb, rb), rb)
            xf = x_ref[rows, :]
            ms = jnp.mean(xf * xf, axis=-1, keepdims=True)
            h_scr[rows, :] = (xf * lax.rsqrt(ms + EPS) * g_ref[...]).astype(BF16)
            return c
        lax.fori_loop(0, tm // rb, body, 0)

        gr = min(tm, GATE_ROWS)
        ri = lax.broadcasted_iota(jnp.int32, (gr, gr), 0)
        ci = lax.broadcasted_iota(jnp.int32, (gr, gr), 1)
        same = (ri // CHUNK) == (ci // CHUNK)
        ltri = jnp.where(same & (ri >= ci), 1.0, 0.0).astype(BF16)
        utri = jnp.where(same & (ri <= ci), 1.0, 0.0).astype(BF16)
        lane = lax.broadcasted_iota(jnp.int32, (gr, LANES), 1)
        neg_a = -jnp.exp(alog_ref[...])

        def gates(r, c):
            for sub in range(gates_unroll):
                rows = pl.ds(pl.multiple_of((r * gates_unroll + sub) * gr, gr), gr)
                s = _dot(h_scr[rows, :], ws_ref[...])
                g3 = _split3(neg_a * _softplus(s + dt_ref[...]))
                gc_f = _dot(ltri, g3[0]) + _dot(ltri, g3[1]) + _dot(ltri, g3[2])
                gc_b = _dot(utri, g3[0]) + _dot(utri, g3[1]) + _dot(utri, g3[2])
                tile = jnp.where(lane < 2 * N_HEADS, _sigmoid(s), jnp.where(lane < 3 * N_HEADS, gc_f, gc_b))
                gates_ref[rows, :] = tile
                gates_t_ref[:, rows] = tile.T
            return c
        gates_unroll = 2 if tm % (2 * gr) == 0 else 1
        lax.fori_loop(0, tm // (gr * gates_unroll), gates, 0)

    o_ref[...] = _dot_nt(h_scr[...], w_ref[...])

    @pl.when(j == pl.num_programs(1) - 1)
    def _():
        for g in range(N_GROUPS):
            cols = slice(g * GROUP_DIM, (g + 1) * GROUP_DIM)
            y = _dot(o_ref[:, cols].astype(BF16), wd_ref[...])
            g2_ref[0, :, cols] = y[:, :GROUP_DIM].astype(BF16)
            g2_ref[1, :, cols] = y[:, GROUP_DIM:].astype(BF16)


def _in_proj(x2d, g, w_t, w_scal, alog_row, dt_row, wd, *, tm, tn, row_starts, t):
    m, d = x2d.shape
    starts = jnp.asarray(row_starts, jnp.int32)
    n_steps = len(row_starts)
    return pl.pallas_call(
        functools.partial(_in_proj_kernel, tm=tm, rb=min(tm, 128)),
        grid_spec=pltpu.PrefetchScalarGridSpec(
            num_scalar_prefetch=1,
            grid=(m // tm, n_steps),
            in_specs=[
                pl.BlockSpec((tm, d), lambda i, j, st: (i, 0)),
                pl.BlockSpec((1, d), lambda i, j, st: (0, 0)),
                pl.BlockSpec((pl.Element(tn), pl.Element(d)), lambda i, j, st: (pl.multiple_of(st[j], 32), 0)),
                pl.BlockSpec((d, LANES), lambda i, j, st: (0, 0)),
                pl.BlockSpec((1, LANES), lambda i, j, st: (0, 0)),
                pl.BlockSpec((1, LANES), lambda i, j, st: (0, 0)),
                pl.BlockSpec((GROUP_DIM, 2 * GROUP_DIM), lambda i, j, st: (0, 0)),
            ],
            out_specs=[
                pl.BlockSpec((tm, tn), lambda i, j, st: (i, j)),
                pl.BlockSpec((tm, LANES), lambda i, j, st: (i, 0)),
                pl.BlockSpec((LANES, tm), lambda i, j, st: (0, i)),
                pl.BlockSpec((2, tm, tn), lambda i, j, st: (0, i % (t // tm), i // (t // tm))),
            ],
            scratch_shapes=[pltpu.VMEM((tm, d), BF16)],
        ),
        out_shape=[jax.ShapeDtypeStruct((m, n_steps * tn), F32), jax.ShapeDtypeStruct((m, LANES), F32),
                   jax.ShapeDtypeStruct((LANES, m), F32),
                   jax.ShapeDtypeStruct((2, t, (m // t) * tn), BF16)],
        compiler_params=pltpu.CompilerParams(
            dimension_semantics=("parallel", "arbitrary"), vmem_limit_bytes=VMEM_LIMIT),
        name="in_proj",
    )(starts, x2d, g, w_t, w_scal, alog_row, dt_row, wd)


def _delta_kernel(*refs, t, rb, gsz, n_cast):
    (q_ref, k_ref, v_ref, z_ref, gates_ref, gates_t_ref, cwq_ref, cwk_ref, cwv_ref, gh_ref), refs = (
        refs[:10], refs[10:])
    cast_in, (out_ref,), cast_out, refs = (
        refs[:n_cast], refs[n_cast:n_cast + 1], refs[n_cast + 1:2 * n_cast + 1], refs[2 * n_cast + 1:])
    pad_scr, qkv_scr, o_scr, kw0, r0, qp0, op0, gl0, kw1, r1, qp1, op1, gl1 = refs

    for src, dst in zip(cast_in, cast_out):
        dst[...] = src[...].astype(dst.dtype)

    nc = t // CHUNK
    ng = nc // gsz
    nrb = t // rb
    h = pl.program_id(1)

    half = (CONV_WIDTH - 1) // 2
    streams = ((q_ref, cwq_ref), (k_ref, cwk_ref), (v_ref, cwv_ref))
    zero8 = jnp.zeros((SUBLANES, LANES), F32)
    for idx, (src_ref, _) in enumerate(streams):
        pad_scr[0, idx, 0:SUBLANES, :] = zero8
        pad_scr[0, idx, SUBLANES:rb + 2 * SUBLANES, :] = src_ref[0:rb + SUBLANES, :]
        pad_scr[1, idx, 0:rb + SUBLANES, :] = src_ref[t - rb - SUBLANES:t, :]
        pad_scr[1, idx, rb + SUBLANES:rb + 2 * SUBLANES, :] = zero8

    for r in range(nrb):
        for idx, (src_ref, cw_ref) in enumerate(streams):
            acc = None
            for i in range(CONV_WIDTH):
                if r == 0 or r == nrb - 1:
                    off = SUBLANES - half + i
                    tap = pad_scr[0 if r == 0 else 1, idx, off:off + rb, :]
                else:
                    off = r * rb - half + i
                    tap = src_ref[off:off + rb, :]
                term = cw_ref[i:i + 1, :] * tap
                acc = term if acc is None else acc + term
            y = _silu(acc)
            if idx < 2:
                y = y * lax.rsqrt(jnp.sum(y * y, axis=-1, keepdims=True) + EPS)
            if idx == 0:
                y = y * (HEAD_DIM ** -0.5)
            qkv_scr[idx, r * rb:(r + 1) * rb, :] = y

    lane = lax.broadcasted_iota(jnp.int32, (CHUNK, LANES), 1)
    ri = lax.broadcasted_iota(jnp.int32, (CHUNK, LANES), 0)
    lo = lane < CHUNK
    ci = jnp.where(lo, lane, lane - CHUNK)
    ahead = jnp.where(lo, ri - ci, ci - ri)
    incl = ahead >= 0
    strict = ahead > 0
    eye = jnp.where(ri == ci, 1.0, 0.0)
    lane_row = lax.broadcasted_iota(jnp.int32, (1, LANES), 1)
    head_row = lax.broadcasted_iota(jnp.int32, (N_HEADS, LANES), 0)
    bufs = ((kw0, r0, qp0, op0, gl0), (kw1, r1, qp1, op1, gl1))

    def chunk_of(gi, j, d):
        return gi * gsz + j if d == 0 else nc - 1 - (gi * gsz + j)

    def column(tile, c):
        return jnp.sum(jnp.where(lane == c, tile, 0.0), axis=-1, keepdims=True)

    def pack(m):
        return jnp.where(lo, m[:CHUNK], m[CHUNK:])

    def blockdiag(m):
        zero = jnp.zeros_like(m)
        return jnp.concatenate([jnp.where(lo, m, zero), jnp.where(lo, zero, m)], axis=0)

    def prep(gi, buf):
        kw_s, r_s, qp_s, op_s, gl_s = buf
        ch = [(j, d) for j in range(gsz) for d in range(2)]
        rows = [pl.ds(chunk_of(gi, j, d) * CHUNK, CHUNK) for j, d in ch]
        q = [qkv_scr[0, r, :] for r in rows]
        k = [qkv_scr[1, r, :] for r in rows]
        v = [qkv_scr[2, r, :] for r in rows]
        gt = [gates_ref[r, :] for r in rows]
        beta = [column(a, d * N_HEADS + h) for a, (j, d) in zip(gt, ch)]
        g = [column(a, (2 + d) * N_HEADS + h) for a, (j, d) in zip(gt, ch)]
        gl = [a[CHUNK - 1:CHUNK, :] if d == 0 else a[0:1, :] for a, (j, d) in zip(g, ch)]
        kb = [a * bt for a, bt in zip(k, beta)]

        def g_row(j, d):
            c = chunk_of(gi, j, d)
            odd = c % 2
            win = gates_t_ref[(2 + d) * N_HEADS:(3 + d) * N_HEADS, pl.ds((c - odd) * CHUNK, 2 * CHUNK)]
            win = jnp.sum(jnp.where(head_row == h, win, 0.0), axis=0, keepdims=True)
            return pltpu.roll(win, CHUNK, axis=1) if odd != d else win

        pair = lambda xs, j: jnp.concatenate([xs[2 * j], xs[2 * j + 1]], axis=0)
        k2 = [pair(k, j).astype(BF16) for j in range(gsz)]
        kk = [pack(_dot_nt(pair(kb, j).astype(BF16), k2[j])) for j in range(gsz)]
        qk_raw = [pack(_dot_nt(pair(q, j).astype(BF16), k2[j])) for j in range(gsz)]
        diff = [jnp.where(lo, g[2 * j], g[2 * j + 1])
                - jnp.where(lane_row < CHUNK, g_row(j, 0), g_row(j, 1)) for j in range(gsz)]
        decay = [jnp.where(incl, jnp.exp(jnp.where(incl, a, 0.0)), 0.0) for a in diff]
        pw = [jnp.where(strict, -(a * dc), 0.0) for a, dc in zip(kk, decay)]
        x = [eye + a for a in pw]
        p16 = [a.astype(BF16) for a in pw]
        p16 = [_dot(a, blockdiag(a)).astype(BF16) for a in p16]
        for _ in range(4):
            res = [_dot(jnp.concatenate([a, b.astype(BF16)], axis=0), blockdiag(a)) for a, b in zip(p16, x)]
            p16 = [a[:CHUNK].astype(BF16) for a in res]
            x = [a + b[CHUNK:] for a, b in zip(x, res)]
        x = [a + _dot(a.astype(BF16), blockdiag(b)) for a, b in zip(x, p16)]
        qk2 = [a * dc for a, dc in zip(qk_raw, decay)]
        unpack = lambda xs: [(xs[j][:, :CHUNK] if d == 0 else xs[j][:, CHUNK:]).astype(BF16) for j, d in ch]
        xinv = unpack(x)
        qk = unpack(qk2)
        eg = [jnp.exp(a) for a in g]
        rhs = [jnp.concatenate([a * bt, b * e], axis=1).astype(BF16)
               for a, bt, b, e in zip(v, beta, kb, eg)]
        sol = [_dot(a, b).astype(BF16) for a, b in zip(xinv, rhs)]
        kd = [(a * jnp.exp(l - b)).T.astype(BF16) for a, b, l in zip(k, g, gl)]
        kdu_kdw = [_dot(a, b) for a, b in zip(kd, sol)]
        qku_qkw = [_dot(a, b) for a, b in zip(qk, sol)]
        for i, (j, d) in enumerate(ch):
            srows = slice(j * HEAD_DIM, (j + 1) * HEAD_DIM)
            crows = slice(j * CHUNK, (j + 1) * CHUNK)
            r_s[d, srows, :] = kdu_kdw[i][:, :HEAD_DIM]
            kw_s[d, srows, :] = kdu_kdw[i][:, HEAD_DIM:].astype(BF16)
            op_s[d, crows, :] = qku_qkw[i][:, :HEAD_DIM]
            qp_s[d, crows, :] = (q[i] * eg[i] - qku_qkw[i][:, HEAD_DIM:]).astype(BF16)
            gl_s[d, j * SUBLANES:(j + 1) * SUBLANES, :] = jnp.broadcast_to(jnp.exp(gl[i]), (SUBLANES, LANES))

    seen = (set(), set())

    def scan(gi, buf, carry):
        kw_s, r_s, qp_s, op_s, gl_s = buf
        st = list(carry)
        for j in range(gsz):
            srows = slice(j * HEAD_DIM, (j + 1) * HEAD_DIM)
            crows = slice(j * CHUNK, (j + 1) * CHUNK)
            for d in range(2):
                c = chunk_of(gi, j, d)
                rows = pl.ds(c * CHUNK, CHUNK)
                st16 = st[d].astype(BF16)
                o = _dot(qp_s[d, crows, :], st16) + op_s[d, crows, :]
                if c in seen[1 - d]:
                    o = o + o_scr[1 - d, rows, :]
                    y = o * lax.rsqrt(jnp.mean(o * o, axis=-1, keepdims=True) + EPS) * gh_ref[...]
                    out_ref[rows, :] = (y * _silu(z_ref[rows, :])).astype(BF16)
                else:
                    o_scr[d, rows, :] = o
                seen[d].add(c)
                gl = gl_s[d, j * SUBLANES:j * SUBLANES + 1, :]
                st[d] = gl * st[d] + (r_s[d, srows, :] - _dot(kw_s[d, srows, :], st16))
        return tuple(st)

    carry = (jnp.zeros((HEAD_DIM, HEAD_DIM), F32),) * 2
    prep(0, bufs[0])
    for gi in range(1, ng):
        prep(gi, bufs[gi % 2])
        carry = scan(gi - 1, bufs[(gi - 1) % 2], carry)
    scan(ng - 1, bufs[(ng - 1) % 2], carry)
    assert all(len(sn) == nc for sn in seen)


def _delta(proj3, gates3, gates_t, conv_w, g_head, cast_weights, *, q_blk, z_blk):
    b, t, _ = proj3.shape
    n_steps = b * N_HEADS
    slab_specs = [pl.BlockSpec((w.shape[0] // n_steps, w.shape[1]), lambda bi, hi: (bi * N_HEADS + hi, 0))
                  for w in cast_weights]
    rb = min(t // 2, 256)
    gsz = min(16, t // (2 * CHUNK))
    assert (t // CHUNK) % (2 * gsz) == 0

    def col(off):
        return pl.BlockSpec((None, t, HEAD_DIM), lambda bi, hi: (bi, 0, off + hi))

    def cw(off):
        return pl.BlockSpec((CONV_WIDTH, HEAD_DIM), lambda bi, hi: (0, off + hi))

    operands = [
        pltpu.VMEM((2, gsz * HEAD_DIM, HEAD_DIM), BF16),
        pltpu.VMEM((2, gsz * HEAD_DIM, HEAD_DIM), F32),
        pltpu.VMEM((2, gsz * CHUNK, HEAD_DIM), BF16),
        pltpu.VMEM((2, gsz * CHUNK, HEAD_DIM), F32),
        pltpu.VMEM((2, gsz * SUBLANES, LANES), F32),
    ]
    return pl.pallas_call(
        functools.partial(_delta_kernel, t=t, rb=rb, gsz=gsz, n_cast=len(cast_weights)),
        grid=(b, N_HEADS),
        in_specs=[
            col(q_blk), col(q_blk + N_HEADS), col(q_blk + 2 * N_HEADS), col(z_blk),
            pl.BlockSpec((None, t, LANES), lambda bi, hi: (bi, 0, 0)),
            pl.BlockSpec((LANES, t), lambda bi, hi: (0, bi)),
            cw(0), cw(N_HEADS), cw(2 * N_HEADS),
            pl.BlockSpec((1, HEAD_DIM), lambda bi, hi: (0, 0)),
        ] + slab_specs,
        out_specs=[pl.BlockSpec((None, t, HEAD_DIM), lambda bi, hi: (bi, 0, hi))] + slab_specs,
        out_shape=[jax.ShapeDtypeStruct((b, t, N_HEADS * HEAD_DIM), BF16)]
        + [jax.ShapeDtypeStruct(w.shape, BF16) for w in cast_weights],
        scratch_shapes=[
            pltpu.VMEM((2, 3, rb + 2 * SUBLANES, LANES), F32),
            pltpu.VMEM((3, t, HEAD_DIM), F32),
            pltpu.VMEM((2, t, HEAD_DIM), F32),
        ] + operands + operands,
        compiler_params=pltpu.CompilerParams(
            dimension_semantics=("parallel", "arbitrary"), vmem_limit_bytes=VMEM_LIMIT),
        name="delta",
    )(proj3, proj3, proj3, proj3, gates3, gates_t, conv_w, conv_w, conv_w, g_head, *cast_weights)


def _scaled_matmul_kernel(a_ref, b_ref, o_ref, *, scale):
    o_ref[...] = (_dot(a_ref[...], b_ref[...]) * scale).astype(o_ref.dtype)


def _scaled_matmul(a, b, *, scale, tm, tn, name):
    m, k = a.shape
    n = b.shape[1]
    return pl.pallas_call(
        functools.partial(_scaled_matmul_kernel, scale=scale),
        grid=(n // tn, m // tm),
        in_specs=[
            pl.BlockSpec((tm, k), lambda j, i: (i, 0)),
            pl.BlockSpec((k, tn), lambda j, i: (0, j)),
        ],
        out_specs=pl.BlockSpec((tm, tn), lambda j, i: (i, j)),
        out_shape=jax.ShapeDtypeStruct((m, n), BF16),
        compiler_params=pltpu.CompilerParams(
            dimension_semantics=("parallel", "parallel"), vmem_limit_bytes=VMEM_LIMIT),
        name=name,
    )(a, b)


@functools.lru_cache(maxsize=None)
def _dft_tables(t, d):
    n = np.arange(t, dtype=np.int64)
    ang = 2.0 * np.pi * ((n[:, None] * n[None, :]) % t) / t
    wt = np.concatenate([np.cos(ang), np.sin(ang)], axis=1).astype(np.float32)
    c = np.arange(d, dtype=np.int64)
    angd = 2.0 * np.pi * ((c[:, None] * c[None, :]) % d) / d
    wd = np.concatenate([np.cos(angd), -np.sin(angd)], axis=1).astype(np.float32)
    return wt, wd


def _merge_kernel(og_ref, fr_ref, ga_ref, gf_ref, x_ref, wdn_ref, wf_ref, wo_ref, g_ref,
                  x1_ref, h2_ref):
    ya = _dot(og_ref[...], wdn_ref[...])
    yf = _dot(fr_ref[...], wf_ref[...])
    merged = _sigmoid(ga_ref[...]) * ya + _sigmoid(gf_ref[...]) * yf
    x1 = x_ref[...] + _dot(merged.astype(BF16), wo_ref[...])
    x1_ref[...] = x1
    ms = jnp.mean(x1 * x1, axis=-1, keepdims=True)
    h2_ref[...] = (x1 * lax.rsqrt(ms + EPS) * g_ref[...]).astype(BF16)


def _merge(og2, fr, proj2, x2d, w_dn, w_f, w_o, g_mlp, *, t, tm):
    m, d = x2d.shape
    bw = og2.shape[1]
    tpb = t // tm

    def const(shape):
        return pl.BlockSpec(shape, lambda i: (0, 0), pipeline_mode=pl.Buffered(1))

    return pl.pallas_call(
        _merge_kernel,
        grid=(m // tm,),
        in_specs=[
            pl.BlockSpec((tm, bw), lambda i: (i, 0)),
            pl.BlockSpec((tm, bw), lambda i: (i % tpb, i // tpb)),
            pl.BlockSpec((tm, d), lambda i: (i, 0)),
            pl.BlockSpec((tm, d), lambda i: (i, 1)),
            pl.BlockSpec((tm, d), lambda i: (i, 0)),
            const((bw, d)), const((bw, d)), const((d, d)), const((1, d)),
        ],
        out_specs=[pl.BlockSpec((tm, d), lambda i: (i, 0)), pl.BlockSpec((tm, d), lambda i: (i, 0))],
        out_shape=[jax.ShapeDtypeStruct((m, d), F32), jax.ShapeDtypeStruct((m, d), BF16)],
        compiler_params=pltpu.CompilerParams(
            dimension_semantics=("parallel",), vmem_limit_bytes=VMEM_LIMIT),
        name="merge",
    )(og2, fr, proj2, proj2, x2d, w_dn, w_f, w_o, g_mlp)


def _mlp_kernel(h_ref, wu_ref, wd_ref, x1_ref, g_ref, o_ref):
    j = pl.program_id(1)

    @pl.when(j == 0)
    def _():
        o_ref[...] = x1_ref[...]

    a = jnp.maximum(_dot(h_ref[...], wu_ref[...]), 0.0)
    o_ref[...] += _dot((a * a).astype(BF16), wd_ref[...])

    @pl.when(j == pl.num_programs(1) - 1)
    def _():
        y = o_ref[...]
        ms = jnp.mean(y * y, axis=-1, keepdims=True)
        o_ref[...] = y * lax.rsqrt(ms + EPS) * g_ref[...]


def _mlp(h2, w_up, w_down, x1, g_final, *, tm, tf):
    m, d = x1.shape
    ff = w_up.shape[1]
    return pl.pallas_call(
        _mlp_kernel,
        grid=(m // tm, ff // tf),
        in_specs=[
            pl.BlockSpec((tm, d), lambda i, j: (i, 0)),
            pl.BlockSpec((d, tf), lambda i, j: (0, j)),
            pl.BlockSpec((tf, d), lambda i, j: (j, 0)),
            pl.BlockSpec((tm, d), lambda i, j: (i, 0)),
            pl.BlockSpec((1, d), lambda i, j: (0, 0)),
        ],
        out_specs=pl.BlockSpec((tm, d), lambda i, j: (i, 0)),
        out_shape=jax.ShapeDtypeStruct((m, d), F32),
        compiler_params=pltpu.CompilerParams(
            dimension_semantics=("parallel", "arbitrary"), vmem_limit_bytes=VMEM_LIMIT),
        name="mlp",
    )(h2, w_up, w_down, x1, g_final)


def _largest_divisor(n, cap, multiple):
    return max(c for c in range(multiple, cap + 1, multiple) if n % c == 0)


def _layer(x, g_mix, w_in, conv_w, a_log_f, a_log_b, dt_f, dt_b, g_head, w_dn_up, w_fourier, w_o,
           g_mlp, w_mlp_up, w_mlp_down, g_out):
    b, t, d = x.shape
    m = b * t
    dn = N_HEADS * HEAD_DIM
    fw = N_GROUPS * GROUP_DIM
    o_q, o_k, o_v, o_z, o_f, o_s, o_g = np.cumsum([0, dn, dn, dn, dn, fw, 4 * N_HEADS])
    w_t = _cast_rows(jnp.swapaxes(w_in, 0, 1), tr=_largest_divisor(w_in.shape[1], 1024, 32))
    w_scal = jnp.pad(w_t[o_s:o_g].T, ((0, 0), (0, LANES - 4 * N_HEADS)))
    tn = 1024
    row_starts = tuple(range(int(o_g), w_t.shape[0], tn)) + tuple(range(int(o_q), int(o_s), tn))
    q_blk = (2 * d) // HEAD_DIM
    z_blk = q_blk + 3 * N_HEADS

    def gate_row(fwd, bwd):
        return jnp.pad(jnp.concatenate([fwd, bwd]).astype(F32),
                       (2 * N_HEADS, LANES - 4 * N_HEADS)).reshape(1, LANES)

    x2d = x.reshape(m, d)
    wt, wd = _dft_tables(t, GROUP_DIM)
    assert tn == fw and row_starts[-1] == o_f, "the channel DFT rides on the last (Fourier) column step"
    proj, gates, gates_t, g2 = _in_proj(x2d, g_mix.reshape(1, d), w_t, w_scal, gate_row(a_log_f, a_log_b),
                                        gate_row(dt_f, dt_b), jnp.asarray(wd).astype(BF16),
                                        tm=min(m, 1024), tn=tn, row_starts=row_starts, t=t)
    proj3 = proj.reshape(b, t, proj.shape[1])

    og, w_dn16, w_f16, w_o16, w_up16, w_down16 = _delta(
        proj3, gates.reshape(b, t, LANES), gates_t, conv_w, g_head.reshape(1, HEAD_DIM),
        [w_dn_up, w_fourier, w_o, w_mlp_up, w_mlp_down], q_blk=q_blk, z_blk=z_blk)

    fr = _scaled_matmul(jnp.asarray(wt).astype(BF16), g2.reshape(2 * t, b * fw),
                        scale=float((t * GROUP_DIM) ** -0.5), tm=min(t, 512), tn=fw, name="dft_seq")

    x1, h2 = _merge(og.reshape(m, dn), fr, proj, x2d, w_dn16, w_f16, w_o16, g_mlp.reshape(1, d),
                    t=t, tm=min(t, 256))
    return _mlp(h2, w_up16, w_down16, x1, g_out.reshape(1, d), tm=min(m, 512), tf=1024)


def kernel(x, g_mix, w_in, conv_w, a_log_fwd, a_log_bwd, dt_bias_fwd, dt_bias_bwd, g_dn_head, w_dn_up,
           w_fourier, w_o, g_mlp, w_mlp_up, w_mlp_down, g_final):
    depth = g_mix.shape[0]
    assert depth == 1, "the final rmsnorm is fused into the last block's MLP kernel"
    b, t, d = x.shape
    out = _layer(x, g_mix[0], w_in[0], conv_w[0], a_log_fwd[0], a_log_bwd[0], dt_bias_fwd[0],
                 dt_bias_bwd[0], g_dn_head[0], w_dn_up[0], w_fourier[0], w_o[0], g_mlp[0],
                 w_mlp_up[0], w_mlp_down[0], g_final)
    return out.reshape(b, t, d)
```

```python
import functools

import jax
import jax.numpy as jnp
import numpy as np
from jax import lax
from jax.experimental import pallas as pl
from jax.experimental.pallas import tpu as pltpu

F32 = jnp.float32
BF16 = jnp.bfloat16

EPS = 1e-6
N_HEADS = 8
HEAD_DIM = 128
CHUNK = 64
CONV_WIDTH = 5
N_GROUPS = 8
GROUP_DIM = 128
LANES = 128
SUBLANES = 8
VMEM_LIMIT = 56 * 1024 * 1024

_NT = (((1,), (1,)), ((), ()))


def _dot(a, b):
    return jnp.dot(a, b, preferred_element_type=F32)


def _dot_nt(a, b):
    return lax.dot_general(a, b, _NT, preferred_element_type=F32)


def _split3(x):
    hi = x.astype(BF16)
    r = x - hi.astype(F32)
    mid = r.astype(BF16)
    r = r - mid.astype(F32)
    return hi, mid, r.astype(BF16)


def _sigmoid(x):
    return 1.0 / (1.0 + jnp.exp(-x))


def _silu(x):
    return x * _sigmoid(x)


def _softplus(x):
    return jnp.maximum(x, 0.0) + jnp.log1p(jnp.exp(-jnp.abs(x)))


GATE_ROWS = 256


def _prologue_kernel(w_ref, x_ref, g_ref, ws_ref, alog_ref, dt_ref, w16_ref, h_ref, gates_ref, gates_t_ref,
                     ws_scr, *, xr, rb, n_x_steps):
    w16_ref[...] = w_ref[...].astype(BF16)

    @pl.when(pl.program_id(0) == 0)
    def _():
        ws_scr[...] = jnp.zeros(ws_scr.shape, BF16)
        ws_scr[0:ws_ref.shape[0], :] = ws_ref[...].astype(BF16)

    @pl.when(pl.program_id(0) < n_x_steps)
    def _():
        def body(r, c):
            rows = pl.ds(pl.multiple_of(r * rb, rb), rb)
            xf = x_ref[rows, :]
            ms = jnp.mean(xf * xf, axis=-1, keepdims=True)
            h_ref[rows, :] = (xf * lax.rsqrt(ms + EPS) * g_ref[...]).astype(BF16)
            return c
        lax.fori_loop(0, xr // rb, body, 0)

        gr = GATE_ROWS
        ri = lax.broadcasted_iota(jnp.int32, (gr, gr), 0)
        ci = lax.broadcasted_iota(jnp.int32, (gr, gr), 1)
        same = (ri // CHUNK) == (ci // CHUNK)
        ltri = jnp.where(same & (ri >= ci), 1.0, 0.0).astype(BF16)
        utri = jnp.where(same & (ri <= ci), 1.0, 0.0).astype(BF16)
        lane = lax.broadcasted_iota(jnp.int32, (gr, LANES), 1)
        neg_a = -jnp.exp(alog_ref[...])
        for sub in range(xr // gr):
            rows = slice(sub * gr, (sub + 1) * gr)
            s = _dot_nt(h_ref[rows, :], ws_scr[...])
            g3 = _split3(neg_a * _softplus(s + dt_ref[...]))
            gc_f = _dot(ltri, g3[0]) + _dot(ltri, g3[1]) + _dot(ltri, g3[2])
            gc_b = _dot(utri, g3[0]) + _dot(utri, g3[1]) + _dot(utri, g3[2])
            tile = jnp.where(lane < 2 * N_HEADS, _sigmoid(s), jnp.where(lane < 3 * N_HEADS, gc_f, gc_b))
            gates_ref[rows, :] = tile
            gates_t_ref[:, rows] = tile.T


def _prologue(w_t, x2d, g, alog_row, dt_row, *, tr, xr, scal_start, n_scal):
    rows, d = w_t.shape
    assert scal_start % n_scal == 0 and n_scal % SUBLANES == 0 and n_scal <= LANES
    m = x2d.shape[0]
    n_steps, n_x = rows // tr, m // xr
    assert n_x <= n_steps and xr % GATE_ROWS == 0
    xi = lambda i: jnp.minimum(i, n_x - 1)
    return pl.pallas_call(
        functools.partial(_prologue_kernel, xr=xr, rb=128, n_x_steps=n_x),
        grid=(n_steps,),
        in_specs=[
            pl.BlockSpec((tr, d), lambda i: (i, 0)),
            pl.BlockSpec((xr, d), lambda i: (xi(i), 0)),
            pl.BlockSpec((1, d), lambda i: (0, 0)),
            pl.BlockSpec((n_scal, d), lambda i: (scal_start // n_scal, 0)),
            pl.BlockSpec((1, LANES), lambda i: (0, 0)),
            pl.BlockSpec((1, LANES), lambda i: (0, 0)),
        ],
        out_specs=[
            pl.BlockSpec((tr, d), lambda i: (i, 0)),
            pl.BlockSpec((xr, d), lambda i: (xi(i), 0)),
            pl.BlockSpec((xr, LANES), lambda i: (xi(i), 0)),
            pl.BlockSpec((LANES, xr), lambda i: (0, xi(i))),
        ],
        out_shape=[jax.ShapeDtypeStruct((rows, d), BF16), jax.ShapeDtypeStruct((m, d), BF16),
                   jax.ShapeDtypeStruct((m, LANES), F32), jax.ShapeDtypeStruct((LANES, m), F32)],
        scratch_shapes=[pltpu.VMEM((LANES, d), BF16)],
        compiler_params=pltpu.CompilerParams(dimension_semantics=("arbitrary",), vmem_limit_bytes=VMEM_LIMIT),
        name="prologue",
    )(w_t, x2d, g, w_t, alog_row, dt_row)


def _in_proj_kernel(st_ref, h_ref, w_ref, wd_ref, o_ref, g2_ref):
    o_ref[...] = _dot_nt(h_ref[...], w_ref[...])

    @pl.when(pl.program_id(1) == pl.num_programs(1) - 1)
    def _():
        for g in range(N_GROUPS):
            cols = slice(g * GROUP_DIM, (g + 1) * GROUP_DIM)
            y = _dot(o_ref[:, cols].astype(BF16), wd_ref[...])
            g2_ref[0, :, cols] = y[:, :GROUP_DIM].astype(BF16)
            g2_ref[1, :, cols] = y[:, GROUP_DIM:].astype(BF16)


def _in_proj(h2d, w_t, wd, *, tm, tn, row_starts, t):
    m, d = h2d.shape
    starts = jnp.asarray(row_starts, jnp.int32)
    n_steps = len(row_starts)
    return pl.pallas_call(
        _in_proj_kernel,
        grid_spec=pltpu.PrefetchScalarGridSpec(
            num_scalar_prefetch=1,
            grid=(m // tm, n_steps),
            in_specs=[
                pl.BlockSpec((tm, d), lambda i, j, st: (i, 0)),
                pl.BlockSpec((pl.Element(tn), pl.Element(d)), lambda i, j, st: (pl.multiple_of(st[j], 32), 0)),
                pl.BlockSpec((GROUP_DIM, 2 * GROUP_DIM), lambda i, j, st: (0, 0)),
            ],
            out_specs=[
                pl.BlockSpec((tm, tn), lambda i, j, st: (i, j)),
                pl.BlockSpec((2, tm, tn), lambda i, j, st: (0, i % (t // tm), i // (t // tm))),
            ],
        ),
        out_shape=[jax.ShapeDtypeStruct((m, n_steps * tn), F32),
                   jax.ShapeDtypeStruct((2, t, (m // t) * tn), BF16)],
        compiler_params=pltpu.CompilerParams(
            dimension_semantics=("parallel", "arbitrary"), vmem_limit_bytes=VMEM_LIMIT),
        name="in_proj",
    )(starts, h2d, w_t, wd)


def _delta_kernel(*refs, t, rb, gsz, n_cast):
    (q_ref, k_ref, v_ref, z_ref, gates_ref, gates_t_ref, cwq_ref, cwk_ref, cwv_ref, gh_ref), refs = (
        refs[:10], refs[10:])
    cast_in, (out_ref,), cast_out, refs = (
        refs[:n_cast], refs[n_cast:n_cast + 1], refs[n_cast + 1:2 * n_cast + 1], refs[2 * n_cast + 1:])
    pad_scr, qkv_scr, o_scr, kw0, r0, qp0, op0, gl0, kw1, r1, qp1, op1, gl1 = refs

    for src, dst in zip(cast_in, cast_out):
        dst[...] = src[...].astype(dst.dtype)

    nc = t // CHUNK
    ng = nc // gsz
    nrb = t // rb
    h = pl.program_id(1)

    half = (CONV_WIDTH - 1) // 2
    streams = ((q_ref, cwq_ref), (k_ref, cwk_ref), (v_ref, cwv_ref))
    zero8 = jnp.zeros((SUBLANES, LANES), F32)
    for idx, (src_ref, _) in enumerate(streams):
        pad_scr[0, idx, 0:SUBLANES, :] = zero8
        pad_scr[0, idx, SUBLANES:rb + 2 * SUBLANES, :] = src_ref[0:rb + SUBLANES, :]
        pad_scr[1, idx, 0:rb + SUBLANES, :] = src_ref[t - rb - SUBLANES:t, :]
        pad_scr[1, idx, rb + SUBLANES:rb + 2 * SUBLANES, :] = zero8

    for r in range(nrb):
        for idx, (src_ref, cw_ref) in enumerate(streams):
            acc = None
            for i in range(CONV_WIDTH):
                if r == 0 or r == nrb - 1:
                    off = SUBLANES - half + i
                    tap = pad_scr[0 if r == 0 else 1, idx, off:off + rb, :]
                else:
                    off = r * rb - half + i
                    tap = src_ref[off:off + rb, :]
                term = cw_ref[i:i + 1, :] * tap
                acc = term if acc is None else acc + term
            y = _silu(acc)
            if idx < 2:
                y = y * lax.rsqrt(jnp.sum(y * y, axis=-1, keepdims=True) + EPS)
            if idx == 0:
                y = y * (HEAD_DIM ** -0.5)
            qkv_scr[idx, r * rb:(r + 1) * rb, :] = y

    lane = lax.broadcasted_iota(jnp.int32, (CHUNK, LANES), 1)
    ri = lax.broadcasted_iota(jnp.int32, (CHUNK, LANES), 0)
    lo = lane < CHUNK
    ci = jnp.where(lo, lane, lane - CHUNK)
    ahead = jnp.where(lo, ri - ci, ci - ri)
    incl = ahead >= 0
    strict = ahead > 0
    eye = jnp.where(ri == ci, 1.0, 0.0)
    lane_row = lax.broadcasted_iota(jnp.int32, (1, LANES), 1)
    head_row = lax.broadcasted_iota(jnp.int32, (N_HEADS, LANES), 0)
    bufs = ((kw0, r0, qp0, op0, gl0), (kw1, r1, qp1, op1, gl1))

    def chunk_of(gi, j, d):
        return gi * gsz + j if d == 0 else nc - 1 - (gi * gsz + j)

    def column(tile, c):
        return jnp.sum(jnp.where(lane == c, tile, 0.0), axis=-1, keepdims=True)

    def pack(m):
        return jnp.where(lo, m[:CHUNK], m[CHUNK:])

    def blockdiag(m):
        zero = jnp.zeros_like(m)
        return jnp.concatenate([jnp.where(lo, m, zero), jnp.where(lo, zero, m)], axis=0)

    def prep(gi, buf):
        kw_s, r_s, qp_s, op_s, gl_s = buf
        ch = [(j, d) for j in range(gsz) for d in range(2)]
        rows = [pl.ds(chunk_of(gi, j, d) * CHUNK, CHUNK) for j, d in ch]
        q = [qkv_scr[0, r, :] for r in rows]
        k = [qkv_scr[1, r, :] for r in rows]
        v = [qkv_scr[2, r, :] for r in rows]
        gt = [gates_ref[r, :] for r in rows]
        beta = [column(a, d * N_HEADS + h) for a, (j, d) in zip(gt, ch)]
        g = [column(a, (2 + d) * N_HEADS + h) for a, (j, d) in zip(gt, ch)]
        gl = [a[CHUNK - 1:CHUNK, :] if d == 0 else a[0:1, :] for a, (j, d) in zip(g, ch)]
        kb = [a * bt for a, bt in zip(k, beta)]

        def g_row(j, d):
            c = chunk_of(gi, j, d)
            odd = c % 2
            win = gates_t_ref[(2 + d) * N_HEADS:(3 + d) * N_HEADS, pl.ds((c - odd) * CHUNK, 2 * CHUNK)]
            win = jnp.sum(jnp.where(head_row == h, win, 0.0), axis=0, keepdims=True)
            return pltpu.roll(win, CHUNK, axis=1) if odd != d else win

        pair = lambda xs, j: jnp.concatenate([xs[2 * j], xs[2 * j + 1]], axis=0)
        k2 = [pair(k, j).astype(BF16) for j in range(gsz)]
        kk = [pack(_dot_nt(pair(kb, j).astype(BF16), k2[j])) for j in range(gsz)]
        qk_raw = [pack(_dot_nt(pair(q, j).astype(BF16), k2[j])) for j in range(gsz)]
        diff = [jnp.where(lo, g[2 * j], g[2 * j + 1])
                - jnp.where(lane_row < CHUNK, g_row(j, 0), g_row(j, 1)) for j in range(gsz)]
        decay = [jnp.where(incl, jnp.exp(jnp.where(incl, a, 0.0)), 0.0) for a in diff]
        pw = [jnp.where(strict, -(a * dc), 0.0) for a, dc in zip(kk, decay)]
        x = [eye + a for a in pw]
        p16 = [a.astype(BF16) for a in pw]
        p16 = [_dot(a, blockdiag(a)).astype(BF16) for a in p16]
        for _ in range(4):
            res = [_dot(jnp.concatenate([a, b.astype(BF16)], axis=0), blockdiag(a)) for a, b in zip(p16, x)]
            p16 = [a[:CHUNK].astype(BF16) for a in res]
            x = [a + b[CHUNK:] for a, b in zip(x, res)]
        x = [a + _dot(a.astype(BF16), blockdiag(b)) for a, b in zip(x, p16)]
        qk2 = [a * dc for a, dc in zip(qk_raw, decay)]
        unpack = lambda xs: [(xs[j][:, :CHUNK] if d == 0 else xs[j][:, CHUNK:]).astype(BF16) for j, d in ch]
        xinv = unpack(x)
        qk = unpack(qk2)
        eg = [jnp.exp(a) for a in g]
        rhs = [jnp.concatenate([a * bt, b * e], axis=1).astype(BF16)
               for a, bt, b, e in zip(v, beta, kb, eg)]
        sol = [_dot(a, b).astype(BF16) for a, b in zip(xinv, rhs)]
        kd = [(a * jnp.exp(l - b)).T.astype(BF16) for a, b, l in zip(k, g, gl)]
        kdu_kdw = [_dot(a, b) for a, b in zip(kd, sol)]
        qku_qkw = [_dot(a, b) for a, b in zip(qk, sol)]
        for i, (j, d) in enumerate(ch):
            srows = slice(j * HEAD_DIM, (j + 1) * HEAD_DIM)
            crows = slice(j * CHUNK, (j + 1) * CHUNK)
            r_s[d, srows, :] = kdu_kdw[i][:, :HEAD_DIM]
            kw_s[d, srows, :] = kdu_kdw[i][:, HEAD_DIM:].astype(BF16)
            op_s[d, crows, :] = qku_qkw[i][:, :HEAD_DIM]
            qp_s[d, crows, :] = (q[i] * eg[i] - qku_qkw[i][:, HEAD_DIM:]).astype(BF16)
            gl_s[d, j * SUBLANES:(j + 1) * SUBLANES, :] = jnp.broadcast_to(jnp.exp(gl[i]), (SUBLANES, LANES))

    seen = (set(), set())

    def scan(gi, buf, carry):
        kw_s, r_s, qp_s, op_s, gl_s = buf
        st = list(carry)
        for j in range(gsz):
            srows = slice(j * HEAD_DIM, (j + 1) * HEAD_DIM)
            crows = slice(j * CHUNK, (j + 1) * CHUNK)
            for d in range(2):
                c = chunk_of(gi, j, d)
                rows = pl.ds(c * CHUNK, CHUNK)
                st16 = st[d].astype(BF16)
                o = _dot(qp_s[d, crows, :], st16) + op_s[d, crows, :]
                if c in seen[1 - d]:
                    o = o + o_scr[1 - d, rows, :]
                    y = o * lax.rsqrt(jnp.mean(o * o, axis=-1, keepdims=True) + EPS) * gh_ref[...]
                    out_ref[rows, :] = (y * _silu(z_ref[rows, :])).astype(BF16)
                else:
                    o_scr[d, rows, :] = o
                seen[d].add(c)
                gl = gl_s[d, j * SUBLANES:j * SUBLANES + 1, :]
                st[d] = gl * st[d] + (r_s[d, srows, :] - _dot(kw_s[d, srows, :], st16))
        return tuple(st)

    carry = (jnp.zeros((HEAD_DIM, HEAD_DIM), F32),) * 2
    prep(0, bufs[0])
    for gi in range(1, ng):
        prep(gi, bufs[gi % 2])
        carry = scan(gi - 1, bufs[(gi - 1) % 2], carry)
    scan(ng - 1, bufs[(ng - 1) % 2], carry)
    assert all(len(sn) == nc for sn in seen)


def _delta(proj3, gates3, gates_t, conv_w, g_head, cast_weights, *, q_blk, z_blk):
    b, t, _ = proj3.shape
    n_steps = b * N_HEADS
    slab_specs = [pl.BlockSpec((w.shape[0] // n_steps, w.shape[1]), lambda bi, hi: (bi * N_HEADS + hi, 0))
                  for w in cast_weights]
    rb = min(t // 2, 256)
    gsz = min(16, t // (2 * CHUNK))
    assert (t // CHUNK) % (2 * gsz) == 0

    def col(off):
        return pl.BlockSpec((None, t, HEAD_DIM), lambda bi, hi: (bi, 0, off + hi))

    def cw(off):
        return pl.BlockSpec((CONV_WIDTH, HEAD_DIM), lambda bi, hi: (0, off + hi))

    operands = [
        pltpu.VMEM((2, gsz * HEAD_DIM, HEAD_DIM), BF16),
        pltpu.VMEM((2, gsz * HEAD_DIM, HEAD_DIM), F32),
        pltpu.VMEM((2, gsz * CHUNK, HEAD_DIM), BF16),
        pltpu.VMEM((2, gsz * CHUNK, HEAD_DIM), F32),
        pltpu.VMEM((2, gsz * SUBLANES, LANES), F32),
    ]
    return pl.pallas_call(
        functools.partial(_delta_kernel, t=t, rb=rb, gsz=gsz, n_cast=len(cast_weights)),
        grid=(b, N_HEADS),
        in_specs=[
            col(q_blk), col(q_blk + N_HEADS), col(q_blk + 2 * N_HEADS), col(z_blk),
            pl.BlockSpec((None, t, LANES), lambda bi, hi: (bi, 0, 0)),
            pl.BlockSpec((LANES, t), lambda bi, hi: (0, bi)),
            cw(0), cw(N_HEADS), cw(2 * N_HEADS),
            pl.BlockSpec((1, HEAD_DIM), lambda bi, hi: (0, 0)),
        ] + slab_specs,
        out_specs=[pl.BlockSpec((None, t, HEAD_DIM), lambda bi, hi: (bi, 0, hi))] + slab_specs,
        out_shape=[jax.ShapeDtypeStruct((b, t, N_HEADS * HEAD_DIM), BF16)]
        + [jax.ShapeDtypeStruct(w.shape, BF16) for w in cast_weights],
        scratch_shapes=[
            pltpu.VMEM((2, 3, rb + 2 * SUBLANES, LANES), F32),
            pltpu.VMEM((3, t, HEAD_DIM), F32),
            pltpu.VMEM((2, t, HEAD_DIM), F32),
        ] + operands + operands,
        compiler_params=pltpu.CompilerParams(
            dimension_semantics=("parallel", "arbitrary"), vmem_limit_bytes=VMEM_LIMIT),
        name="delta",
    )(proj3, proj3, proj3, proj3, gates3, gates_t, conv_w, conv_w, conv_w, g_head, *cast_weights)


def _scaled_matmul_kernel(a_ref, b_ref, o_ref, *, scale):
    o_ref[...] = (_dot(a_ref[...], b_ref[...]) * scale).astype(o_ref.dtype)


def _scaled_matmul(a, b, *, scale, tm, tn, name):
    m, k = a.shape
    n = b.shape[1]
    return pl.pallas_call(
        functools.partial(_scaled_matmul_kernel, scale=scale),
        grid=(n // tn, m // tm),
        in_specs=[
            pl.BlockSpec((tm, k), lambda j, i: (i, 0)),
            pl.BlockSpec((k, tn), lambda j, i: (0, j)),
        ],
        out_specs=pl.BlockSpec((tm, tn), lambda j, i: (i, j)),
        out_shape=jax.ShapeDtypeStruct((m, n), BF16),
        compiler_params=pltpu.CompilerParams(
            dimension_semantics=("parallel", "parallel"), vmem_limit_bytes=VMEM_LIMIT),
        name=name,
    )(a, b)


@functools.lru_cache(maxsize=None)
def _dft_tables(t, d):
    n = np.arange(t, dtype=np.int64)
    ang = 2.0 * np.pi * ((n[:, None] * n[None, :]) % t) / t
    wt = np.concatenate([np.cos(ang), np.sin(ang)], axis=1).astype(np.float32)
    c = np.arange(d, dtype=np.int64)
    angd = 2.0 * np.pi * ((c[:, None] * c[None, :]) % d) / d
    wd = np.concatenate([np.cos(angd), -np.sin(angd)], axis=1).astype(np.float32)
    return wt, wd


def _merge_kernel(og_ref, fr_ref, ga_ref, gf_ref, x_ref, wdn_ref, wf_ref, wo_ref, g_ref,
                  x1_ref, h2_ref):
    ya = _dot(og_ref[...], wdn_ref[...])
    yf = _dot(fr_ref[...], wf_ref[...])
    merged = _sigmoid(ga_ref[...]) * ya + _sigmoid(gf_ref[...]) * yf
    x1 = x_ref[...] + _dot(merged.astype(BF16), wo_ref[...])
    x1_ref[...] = x1
    ms = jnp.mean(x1 * x1, axis=-1, keepdims=True)
    h2_ref[...] = (x1 * lax.rsqrt(ms + EPS) * g_ref[...]).astype(BF16)


def _merge(og2, fr, proj2, x2d, w_dn, w_f, w_o, g_mlp, *, t, tm):
    m, d = x2d.shape
    bw = og2.shape[1]
    tpb = t // tm

    def const(shape):
        return pl.BlockSpec(shape, lambda i: (0, 0), pipeline_mode=pl.Buffered(1))

    return pl.pallas_call(
        _merge_kernel,
        grid=(m // tm,),
        in_specs=[
            pl.BlockSpec((tm, bw), lambda i: (i, 0)),
            pl.BlockSpec((tm, bw), lambda i: (i % tpb, i // tpb)),
            pl.BlockSpec((tm, d), lambda i: (i, 0)),
            pl.BlockSpec((tm, d), lambda i: (i, 1)),
            pl.BlockSpec((tm, d), lambda i: (i, 0)),
            const((bw, d)), const((bw, d)), const((d, d)), const((1, d)),
        ],
        out_specs=[pl.BlockSpec((tm, d), lambda i: (i, 0)), pl.BlockSpec((tm, d), lambda i: (i, 0))],
        out_shape=[jax.ShapeDtypeStruct((m, d), F32), jax.ShapeDtypeStruct((m, d), BF16)],
        compiler_params=pltpu.CompilerParams(
            dimension_semantics=("parallel",), vmem_limit_bytes=VMEM_LIMIT),
        name="merge",
    )(og2, fr, proj2, proj2, x2d, w_dn, w_f, w_o, g_mlp)


def _mlp_kernel(h_ref, wu_ref, wd_ref, x1_ref, g_ref, o_ref):
    j = pl.program_id(1)

    @pl.when(j == 0)
    def _():
        o_ref[...] = x1_ref[...]

    a = jnp.maximum(_dot(h_ref[...], wu_ref[...]), 0.0)
    o_ref[...] += _dot((a * a).astype(BF16), wd_ref[...])

    @pl.when(j == pl.num_programs(1) - 1)
    def _():
        y = o_ref[...]
        ms = jnp.mean(y * y, axis=-1, keepdims=True)
        o_ref[...] = y * lax.rsqrt(ms + EPS) * g_ref[...]


def _mlp(h2, w_up, w_down, x1, g_final, *, tm, tf):
    m, d = x1.shape
    ff = w_up.shape[1]
    return pl.pallas_call(
        _mlp_kernel,
        grid=(m // tm, ff // tf),
        in_specs=[
            pl.BlockSpec((tm, d), lambda i, j: (i, 0)),
            pl.BlockSpec((d, tf), lambda i, j: (0, j)),
            pl.BlockSpec((tf, d), lambda i, j: (j, 0)),
            pl.BlockSpec((tm, d), lambda i, j: (i, 0)),
            pl.BlockSpec((1, d), lambda i, j: (0, 0)),
        ],
        out_specs=pl.BlockSpec((tm, d), lambda i, j: (i, 0)),
        out_shape=jax.ShapeDtypeStruct((m, d), F32),
        compiler_params=pltpu.CompilerParams(
            dimension_semantics=("parallel", "arbitrary"), vmem_limit_bytes=VMEM_LIMIT),
        name="mlp",
    )(h2, w_up, w_down, x1, g_final)


def _largest_divisor(n, cap, multiple):
    return max(c for c in range(multiple, cap + 1, multiple) if n % c == 0)


def _layer(x, g_mix, w_in, conv_w, a_log_f, a_log_b, dt_f, dt_b, g_head, w_dn_up, w_fourier, w_o,
           g_mlp, w_mlp_up, w_mlp_down, g_out):
    b, t, d = x.shape
    m = b * t
    dn = N_HEADS * HEAD_DIM
    fw = N_GROUPS * GROUP_DIM
    o_q, o_k, o_v, o_z, o_f, o_s, o_g = np.cumsum([0, dn, dn, dn, dn, fw, 4 * N_HEADS])
    w_t32 = jnp.swapaxes(w_in, 0, 1)
    tn = 1024
    row_starts = tuple(range(int(o_g), w_t32.shape[0], tn)) + tuple(range(int(o_q), int(o_s), tn))
    q_blk = (2 * d) // HEAD_DIM
    z_blk = q_blk + 3 * N_HEADS

    def gate_row(fwd, bwd):
        return jnp.pad(jnp.concatenate([fwd, bwd]).astype(F32),
                       (2 * N_HEADS, LANES - 4 * N_HEADS)).reshape(1, LANES)

    x2d = x.reshape(m, d)
    w_t, h2d, gates, gates_t = _prologue(w_t32, x2d, g_mix.reshape(1, d), gate_row(a_log_f, a_log_b),
                                         gate_row(dt_f, dt_b), tr=_largest_divisor(w_t32.shape[0], 1024, 32),
                                         xr=min(m, 512), scal_start=int(o_s), n_scal=4 * N_HEADS)
    wt, wd = _dft_tables(t, GROUP_DIM)
    assert tn == fw and row_starts[-1] == o_f, "the channel DFT rides on the last (Fourier) column step"
    proj, g2 = _in_proj(h2d, w_t, jnp.asarray(wd).astype(BF16), tm=min(m, 1024), tn=tn,
                        row_starts=row_starts, t=t)
    proj3 = proj.reshape(b, t, proj.shape[1])

    og, w_dn16, w_f16, w_o16, w_up16, w_down16 = _delta(
        proj3, gates.reshape(b, t, LANES), gates_t, conv_w, g_head.reshape(1, HEAD_DIM),
        [w_dn_up, w_fourier, w_o, w_mlp_up, w_mlp_down], q_blk=q_blk, z_blk=z_blk)

    fr = _scaled_matmul(jnp.asarray(wt).astype(BF16), g2.reshape(2 * t, b * fw),
                        scale=float((t * GROUP_DIM) ** -0.5), tm=min(t, 512), tn=fw, name="dft_seq")

    x1, h2 = _merge(og.reshape(m, dn), fr, proj, x2d, w_dn16, w_f16, w_o16, g_mlp.reshape(1, d),
                    t=t, tm=min(t, 256))
    return _mlp(h2, w_up16, w_down16, x1, g_out.reshape(1, d), tm=min(m, 512), tf=1024)


def kernel(x, g_mix, w_in, conv_w, a_log_fwd, a_log_bwd, dt_bias_fwd, dt_bias_bwd, g_dn_head, w_dn_up,
           w_fourier, w_o, g_mlp, w_mlp_up, w_mlp_down, g_final):
    depth = g_mix.shape[0]
    assert depth == 1, "the final rmsnorm is fused into the last block's MLP kernel"
    b, t, d = x.shape
    out = _layer(x, g_mix[0], w_in[0], conv_w[0], a_log_fwd[0], a_log_bwd[0], dt_bias_fwd[0],
                 dt_bias_bwd[0], g_dn_head[0], w_dn_up[0], w_fourier[0], w_o[0], g_mlp[0],
                 w_mlp_up[0], w_mlp_down[0], g_final)
    return out.reshape(b, t, d)
```

```python
import functools

import jax
import jax.numpy as jnp
import numpy as np
from jax import lax
from jax.experimental import pallas as pl
from jax.experimental.pallas import tpu as pltpu

F32 = jnp.float32
BF16 = jnp.bfloat16

EPS = 1e-6
N_HEADS = 8
HEAD_DIM = 128
CHUNK = 64
CONV_WIDTH = 5
N_GROUPS = 8
GROUP_DIM = 128
LANES = 128
SUBLANES = 8
VMEM_LIMIT = 56 * 1024 * 1024

_NT = (((1,), (1,)), ((), ()))


def _dot(a, b):
    return jnp.dot(a, b, preferred_element_type=F32)


def _dot_nt(a, b):
    return lax.dot_general(a, b, _NT, preferred_element_type=F32)


def _split3(x):
    hi = x.astype(BF16)
    r = x - hi.astype(F32)
    mid = r.astype(BF16)
    r = r - mid.astype(F32)
    return hi, mid, r.astype(BF16)


def _sigmoid(x):
    return 1.0 / (1.0 + jnp.exp(-x))


def _silu(x):
    return x * _sigmoid(x)


def _softplus(x):
    return jnp.maximum(x, 0.0) + jnp.log1p(jnp.exp(-jnp.abs(x)))


GATE_ROWS = 256


def _prologue_kernel(w_ref, x_ref, g_ref, ws_ref, alog_ref, dt_ref, w16_ref, h_ref, gates_ref, gates_t_ref,
                     ws_scr, *, xr, rb, n_x_steps):
    w16_ref[...] = w_ref[...].astype(BF16)

    @pl.when(pl.program_id(0) == 0)
    def _():
        ws_scr[...] = jnp.zeros(ws_scr.shape, BF16)
        ws_scr[0:ws_ref.shape[0], :] = ws_ref[...].astype(BF16)

    @pl.when(pl.program_id(0) < n_x_steps)
    def _():
        def body(r, c):
            rows = pl.ds(pl.multiple_of(r * rb, rb), rb)
            xf = x_ref[rows, :]
            ms = jnp.mean(xf * xf, axis=-1, keepdims=True)
            h_ref[rows, :] = (xf * lax.rsqrt(ms + EPS) * g_ref[...]).astype(BF16)
            return c
        lax.fori_loop(0, xr // rb, body, 0)

        gr = GATE_ROWS
        ri = lax.broadcasted_iota(jnp.int32, (gr, gr), 0)
        ci = lax.broadcasted_iota(jnp.int32, (gr, gr), 1)
        same = (ri // CHUNK) == (ci // CHUNK)
        ltri = jnp.where(same & (ri >= ci), 1.0, 0.0).astype(BF16)
        utri = jnp.where(same & (ri <= ci), 1.0, 0.0).astype(BF16)
        lane = lax.broadcasted_iota(jnp.int32, (gr, LANES), 1)
        neg_a = -jnp.exp(alog_ref[...])
        for sub in range(xr // gr):
            rows = slice(sub * gr, (sub + 1) * gr)
            s = _dot_nt(h_ref[rows, :], ws_scr[...])
            g3 = _split3(neg_a * _softplus(s + dt_ref[...]))
            gc_f = _dot(ltri, g3[0]) + _dot(ltri, g3[1]) + _dot(ltri, g3[2])
            gc_b = _dot(utri, g3[0]) + _dot(utri, g3[1]) + _dot(utri, g3[2])
            tile = jnp.where(lane < 2 * N_HEADS, _sigmoid(s), jnp.where(lane < 3 * N_HEADS, gc_f, gc_b))
            gates_ref[rows, :] = tile
            gates_t_ref[:, rows] = tile.T


def _prologue(w_t, x2d, g, alog_row, dt_row, *, tr, xr, scal_start, n_scal):
    rows, d = w_t.shape
    assert scal_start % n_scal == 0 and n_scal % SUBLANES == 0 and n_scal <= LANES
    m = x2d.shape[0]
    n_steps, n_x = rows // tr, m // xr
    assert n_x <= n_steps and xr % GATE_ROWS == 0
    xi = lambda i: jnp.minimum(i, n_x - 1)
    return pl.pallas_call(
        functools.partial(_prologue_kernel, xr=xr, rb=128, n_x_steps=n_x),
        grid=(n_steps,),
        in_specs=[
            pl.BlockSpec((tr, d), lambda i: (i, 0)),
            pl.BlockSpec((xr, d), lambda i: (xi(i), 0)),
            pl.BlockSpec((1, d), lambda i: (0, 0)),
            pl.BlockSpec((n_scal, d), lambda i: (scal_start // n_scal, 0)),
            pl.BlockSpec((1, LANES), lambda i: (0, 0)),
            pl.BlockSpec((1, LANES), lambda i: (0, 0)),
        ],
        out_specs=[
            pl.BlockSpec((tr, d), lambda i: (i, 0)),
            pl.BlockSpec((xr, d), lambda i: (xi(i), 0)),
            pl.BlockSpec((xr, LANES), lambda i: (xi(i), 0)),
            pl.BlockSpec((LANES, xr), lambda i: (0, xi(i))),
        ],
        out_shape=[jax.ShapeDtypeStruct((rows, d), BF16), jax.ShapeDtypeStruct((m, d), BF16),
                   jax.ShapeDtypeStruct((m, LANES), F32), jax.ShapeDtypeStruct((LANES, m), F32)],
        scratch_shapes=[pltpu.VMEM((LANES, d), BF16)],
        compiler_params=pltpu.CompilerParams(dimension_semantics=("arbitrary",), vmem_limit_bytes=VMEM_LIMIT),
        name="prologue",
    )(w_t, x2d, g, w_t, alog_row, dt_row)


def _in_proj_kernel(st_ref, h_ref, w_ref, wd_ref, o_ref, g2_ref):
    o_ref[...] = _dot_nt(h_ref[...], w_ref[...])

    @pl.when(pl.program_id(1) == pl.num_programs(1) - 1)
    def _():
        for g in range(N_GROUPS):
            cols = slice(g * GROUP_DIM, (g + 1) * GROUP_DIM)
            y = _dot(o_ref[:, cols].astype(BF16), wd_ref[...])
            g2_ref[0, :, cols] = y[:, :GROUP_DIM].astype(BF16)
            g2_ref[1, :, cols] = y[:, GROUP_DIM:].astype(BF16)


def _in_proj(h2d, w_t, wd, *, tm, tn, row_starts, t):
    m, d = h2d.shape
    starts = jnp.asarray(row_starts, jnp.int32)
    n_steps = len(row_starts)
    return pl.pallas_call(
        _in_proj_kernel,
        grid_spec=pltpu.PrefetchScalarGridSpec(
            num_scalar_prefetch=1,
            grid=(m // tm, n_steps),
            in_specs=[
                pl.BlockSpec((tm, d), lambda i, j, st: (i, 0)),
                pl.BlockSpec((pl.Element(tn), pl.Element(d)), lambda i, j, st: (pl.multiple_of(st[j], 32), 0)),
                pl.BlockSpec((GROUP_DIM, 2 * GROUP_DIM), lambda i, j, st: (0, 0)),
            ],
            out_specs=[
                pl.BlockSpec((tm, tn), lambda i, j, st: (i, j)),
                pl.BlockSpec((2, tm, tn), lambda i, j, st: (0, i % (t // tm), i // (t // tm))),
            ],
        ),
        out_shape=[jax.ShapeDtypeStruct((m, n_steps * tn), F32),
                   jax.ShapeDtypeStruct((2, t, (m // t) * tn), BF16)],
        compiler_params=pltpu.CompilerParams(
            dimension_semantics=("parallel", "arbitrary"), vmem_limit_bytes=VMEM_LIMIT),
        name="in_proj",
    )(starts, h2d, w_t, wd)


def _delta_kernel(*refs, t, rb, gsz, n_cast):
    (q_ref, k_ref, v_ref, z_ref, gates_ref, gates_t_ref, cwq_ref, cwk_ref, cwv_ref, gh_ref), refs = (
        refs[:10], refs[10:])
    cast_in, (out_ref,), cast_out, refs = (
        refs[:n_cast], refs[n_cast:n_cast + 1], refs[n_cast + 1:2 * n_cast + 1], refs[2 * n_cast + 1:])
    pad_scr, qkv_scr, o_scr, kw0, r0, qp0, op0, gl0, kw1, r1, qp1, op1, gl1 = refs

    for src, dst in zip(cast_in, cast_out):
        dst[...] = src[...].astype(dst.dtype)

    nc = t // CHUNK
    ng = nc // gsz
    nrb = t // rb
    h = pl.program_id(1)

    half = (CONV_WIDTH - 1) // 2
    streams = ((q_ref, cwq_ref), (k_ref, cwk_ref), (v_ref, cwv_ref))
    zero8 = jnp.zeros((SUBLANES, LANES), F32)
    for idx, (src_ref, _) in enumerate(streams):
        pad_scr[0, idx, 0:SUBLANES, :] = zero8
        pad_scr[0, idx, SUBLANES:rb + 2 * SUBLANES, :] = src_ref[0:rb + SUBLANES, :]
        pad_scr[1, idx, 0:rb + SUBLANES, :] = src_ref[t - rb - SUBLANES:t, :]
        pad_scr[1, idx, rb + SUBLANES:rb + 2 * SUBLANES, :] = zero8

    for r in range(nrb):
        for idx, (src_ref, cw_ref) in enumerate(streams):
            acc = None
            for i in range(CONV_WIDTH):
                if r == 0 or r == nrb - 1:
                    off = SUBLANES - half + i
                    tap = pad_scr[0 if r == 0 else 1, idx, off:off + rb, :]
                else:
                    off = r * rb - half + i
                    tap = src_ref[off:off + rb, :]
                term = cw_ref[i:i + 1, :] * tap
                acc = term if acc is None else acc + term
            y = _silu(acc)
            if idx < 2:
                y = y * lax.rsqrt(jnp.sum(y * y, axis=-1, keepdims=True) + EPS)
            if idx == 0:
                y = y * (HEAD_DIM ** -0.5)
            qkv_scr[idx, r * rb:(r + 1) * rb, :] = y

    lane = lax.broadcasted_iota(jnp.int32, (CHUNK, LANES), 1)
    ri = lax.broadcasted_iota(jnp.int32, (CHUNK, LANES), 0)
    lo = lane < CHUNK
    ci = jnp.where(lo, lane, lane - CHUNK)
    ahead = jnp.where(lo, ri - ci, ci - ri)
    incl = ahead >= 0
    strict = ahead > 0
    eye = jnp.where(ri == ci, 1.0, 0.0)
    lane_row = lax.broadcasted_iota(jnp.int32, (1, LANES), 1)
    head_row = lax.broadcasted_iota(jnp.int32, (N_HEADS, LANES), 0)
    bufs = ((kw0, r0, qp0, op0, gl0), (kw1, r1, qp1, op1, gl1))

    def chunk_of(gi, j, d):
        return gi * gsz + j if d == 0 else nc - 1 - (gi * gsz + j)

    def column(tile, c):
        return jnp.sum(jnp.where(lane == c, tile, 0.0), axis=-1, keepdims=True)

    def pack(m):
        return jnp.where(lo, m[:CHUNK], m[CHUNK:])

    def blockdiag(m):
        zero = jnp.zeros_like(m)
        return jnp.concatenate([jnp.where(lo, m, zero), jnp.where(lo, zero, m)], axis=0)

    def prep(gi, buf):
        kw_s, r_s, qp_s, op_s, gl_s = buf
        ch = [(j, d) for j in range(gsz) for d in range(2)]
        rows = [pl.ds(chunk_of(gi, j, d) * CHUNK, CHUNK) for j, d in ch]
        q = [qkv_scr[0, r, :] for r in rows]
        k = [qkv_scr[1, r, :] for r in rows]
        v = [qkv_scr[2, r, :] for r in rows]
        gt = [gates_ref[r, :] for r in rows]
        beta = [column(a, d * N_HEADS + h) for a, (j, d) in zip(gt, ch)]
        g = [column(a, (2 + d) * N_HEADS + h) for a, (j, d) in zip(gt, ch)]
        gl = [a[CHUNK - 1:CHUNK, :] if d == 0 else a[0:1, :] for a, (j, d) in zip(g, ch)]
        kb = [a * bt for a, bt in zip(k, beta)]

        def g_row(j, d):
            c = chunk_of(gi, j, d)
            odd = c % 2
            win = gates_t_ref[(2 + d) * N_HEADS:(3 + d) * N_HEADS, pl.ds((c - odd) * CHUNK, 2 * CHUNK)]
            win = jnp.sum(jnp.where(head_row == h, win, 0.0), axis=0, keepdims=True)
            return pltpu.roll(win, CHUNK, axis=1) if odd != d else win

        pair = lambda xs, j: jnp.concatenate([xs[2 * j], xs[2 * j + 1]], axis=0)
        k2 = [pair(k, j).astype(BF16) for j in range(gsz)]
        kk = [pack(_dot_nt(pair(kb, j).astype(BF16), k2[j])) for j in range(gsz)]
        qk_raw = [pack(_dot_nt(pair(q, j).astype(BF16), k2[j])) for j in range(gsz)]
        diff = [jnp.where(lo, g[2 * j], g[2 * j + 1])
                - jnp.where(lane_row < CHUNK, g_row(j, 0), g_row(j, 1)) for j in range(gsz)]
        decay = [jnp.where(incl, jnp.exp(jnp.where(incl, a, 0.0)), 0.0) for a in diff]
        pw = [jnp.where(strict, -(a * dc), 0.0) for a, dc in zip(kk, decay)]
        x = [eye + a for a in pw]
        p16 = [a.astype(BF16) for a in pw]
        p16 = [_dot(a, blockdiag(a)).astype(BF16) for a in p16]
        for _ in range(4):
            res = [_dot(jnp.concatenate([a, b.astype(BF16)], axis=0), blockdiag(a)) for a, b in zip(p16, x)]
            p16 = [a[:CHUNK].astype(BF16) for a in res]
            x = [a + b[CHUNK:] for a, b in zip(x, res)]
        x = [a + _dot(a.astype(BF16), blockdiag(b)) for a, b in zip(x, p16)]
        qk2 = [a * dc for a, dc in zip(qk_raw, decay)]
        unpack = lambda xs: [(xs[j][:, :CHUNK] if d == 0 else xs[j][:, CHUNK:]).astype(BF16) for j, d in ch]
        xinv = unpack(x)
        qk = unpack(qk2)
        eg = [jnp.exp(a) for a in g]
        rhs = [jnp.concatenate([a * bt, b * e], axis=1).astype(BF16)
               for a, bt, b, e in zip(v, beta, kb, eg)]
        sol = [_dot(a, b).astype(BF16) for a, b in zip(xinv, rhs)]
        kd = [(a * jnp.exp(l - b)).T.astype(BF16) for a, b, l in zip(k, g, gl)]
        kdu_kdw = [_dot(a, b) for a, b in zip(kd, sol)]
        qku_qkw = [_dot(a, b) for a, b in zip(qk, sol)]
        for i, (j, d) in enumerate(ch):
            srows = slice(j * HEAD_DIM, (j + 1) * HEAD_DIM)
            crows = slice(j * CHUNK, (j + 1) * CHUNK)
            r_s[d, srows, :] = kdu_kdw[i][:, :HEAD_DIM]
            kw_s[d, srows, :] = kdu_kdw[i][:, HEAD_DIM:].astype(BF16)
            op_s[d, crows, :] = qku_qkw[i][:, :HEAD_DIM]
            qp_s[d, crows, :] = (q[i] * eg[i] - qku_qkw[i][:, HEAD_DIM:]).astype(BF16)
            gl_s[d, j * SUBLANES:(j + 1) * SUBLANES, :] = jnp.broadcast_to(jnp.exp(gl[i]), (SUBLANES, LANES))

    seen = (set(), set())

    def scan(gi, buf, carry):
        kw_s, r_s, qp_s, op_s, gl_s = buf
        st = list(carry)
        for j in range(gsz):
            srows = slice(j * HEAD_DIM, (j + 1) * HEAD_DIM)
            crows = slice(j * CHUNK, (j + 1) * CHUNK)
            for d in range(2):
                c = chunk_of(gi, j, d)
                rows = pl.ds(c * CHUNK, CHUNK)
                st16 = st[d].astype(BF16)
                o = _dot(qp_s[d, crows, :], st16) + op_s[d, crows, :]
                if c in seen[1 - d]:
                    o = o + o_scr[1 - d, rows, :]
                    y = o * lax.rsqrt(jnp.mean(o * o, axis=-1, keepdims=True) + EPS) * gh_ref[...]
                    out_ref[rows, :] = (y * _silu(z_ref[rows, :])).astype(BF16)
                else:
                    o_scr[d, rows, :] = o
                seen[d].add(c)
                gl = gl_s[d, j * SUBLANES:j * SUBLANES + 1, :]
                st[d] = gl * st[d] + (r_s[d, srows, :] - _dot(kw_s[d, srows, :], st16))
        return tuple(st)

    carry = (jnp.zeros((HEAD_DIM, HEAD_DIM), F32),) * 2
    prep(0, bufs[0])
    for gi in range(1, ng):
        prep(gi, bufs[gi % 2])
        carry = scan(gi - 1, bufs[(gi - 1) % 2], carry)
    scan(ng - 1, bufs[(ng - 1) % 2], carry)
    assert all(len(sn) == nc for sn in seen)


def _delta(proj3, gates3, gates_t, conv_w, g_head, cast_weights, *, q_blk, z_blk):
    b, t, _ = proj3.shape
    n_steps = b * N_HEADS
    slab_specs = [pl.BlockSpec((w.shape[0] // n_steps, w.shape[1]), lambda bi, hi: (bi * N_HEADS + hi, 0))
                  for w in cast_weights]
    rb = min(t // 2, 256)
    gsz = min(16, t // (2 * CHUNK))
    assert (t // CHUNK) % (2 * gsz) == 0

    def col(off):
        return pl.BlockSpec((None, t, HEAD_DIM), lambda bi, hi: (bi, 0, off + hi))

    def cw(off):
        return pl.BlockSpec((CONV_WIDTH, HEAD_DIM), lambda bi, hi: (0, off + hi))

    operands = [
        pltpu.VMEM((2, gsz * HEAD_DIM, HEAD_DIM), BF16),
        pltpu.VMEM((2, gsz * HEAD_DIM, HEAD_DIM), F32),
        pltpu.VMEM((2, gsz * CHUNK, HEAD_DIM), BF16),
        pltpu.VMEM((2, gsz * CHUNK, HEAD_DIM), F32),
        pltpu.VMEM((2, gsz * SUBLANES, LANES), F32),
    ]
    return pl.pallas_call(
        functools.partial(_delta_kernel, t=t, rb=rb, gsz=gsz, n_cast=len(cast_weights)),
        grid=(b, N_HEADS),
        in_specs=[
            col(q_blk), col(q_blk + N_HEADS), col(q_blk + 2 * N_HEADS), col(z_blk),
            pl.BlockSpec((None, t, LANES), lambda bi, hi: (bi, 0, 0)),
            pl.BlockSpec((LANES, t), lambda bi, hi: (0, bi)),
            cw(0), cw(N_HEADS), cw(2 * N_HEADS),
            pl.BlockSpec((1, HEAD_DIM), lambda bi, hi: (0, 0)),
        ] + slab_specs,
        out_specs=[pl.BlockSpec((None, t, HEAD_DIM), lambda bi, hi: (bi, 0, hi))] + slab_specs,
        out_shape=[jax.ShapeDtypeStruct((b, t, N_HEADS * HEAD_DIM), BF16)]
        + [jax.ShapeDtypeStruct(w.shape, BF16) for w in cast_weights],
        scratch_shapes=[
            pltpu.VMEM((2, 3, rb + 2 * SUBLANES, LANES), F32),
            pltpu.VMEM((3, t, HEAD_DIM), F32),
            pltpu.VMEM((2, t, HEAD_DIM), F32),
        ] + operands + operands,
        compiler_params=pltpu.CompilerParams(
            dimension_semantics=("parallel", "arbitrary"), vmem_limit_bytes=VMEM_LIMIT),
        name="delta",
    )(proj3, proj3, proj3, proj3, gates3, gates_t, conv_w, conv_w, conv_w, g_head, *cast_weights)


DFT_COLS = 256


def _dft_seq_kernel(g_ref, c_ref, s_ref, p_ref, o_ref, *, scale):
    half = s_ref.shape[0]
    row = lax.broadcasted_iota(jnp.int32, (half, DFT_COLS), 0)
    for c0 in range(0, o_ref.shape[1], DFT_COLS):
        cols = slice(c0, c0 + DFT_COLS)
        a = _dot(c_ref[...], g_ref[0, :, cols]) * scale
        b = _dot(s_ref[...], g_ref[1, :, cols]) * scale
        o_ref[0:half, cols] = (a[:half] + b).astype(o_ref.dtype)
        mirrored = _dot(p_ref[...], (a[:half] - b).astype(BF16))
        o_ref[half:2 * half, cols] = jnp.where(row == 0, a[half:half + 1], mirrored).astype(o_ref.dtype)


def _dft_seq(g2, c_half, s_half, perm, *, scale, tn):
    _, t, n = g2.shape

    def const(shape):
        return pl.BlockSpec(shape, lambda j: (0, 0), pipeline_mode=pl.Buffered(1))

    return pl.pallas_call(
        functools.partial(_dft_seq_kernel, scale=scale),
        grid=(n // tn,),
        in_specs=[
            pl.BlockSpec((2, t, tn), lambda j: (0, 0, j)),
            const(c_half.shape), const(s_half.shape), const(perm.shape),
        ],
        out_specs=pl.BlockSpec((t, tn), lambda j: (0, j)),
        out_shape=jax.ShapeDtypeStruct((t, n), BF16),
        compiler_params=pltpu.CompilerParams(dimension_semantics=("parallel",), vmem_limit_bytes=VMEM_LIMIT),
        name="dft_seq",
    )(g2, c_half, s_half, perm)


@functools.lru_cache(maxsize=None)
def _dft_tables(t, d):
    half = t // 2
    k = np.arange(half + SUBLANES, dtype=np.int64)
    n = np.arange(t, dtype=np.int64)
    ang = 2.0 * np.pi * ((k[:, None] * n[None, :]) % t) / t
    c_half = np.cos(ang).astype(np.float32)
    s_half = np.sin(ang[:half]).astype(np.float32)
    perm = np.zeros((half, half), np.float32)
    perm[np.arange(1, half), half - np.arange(1, half)] = 1.0
    c = np.arange(d, dtype=np.int64)
    angd = 2.0 * np.pi * ((c[:, None] * c[None, :]) % d) / d
    wd = np.concatenate([np.cos(angd), -np.sin(angd)], axis=1).astype(np.float32)
    return c_half, s_half, perm, wd


def _merge_kernel(og_ref, fr_ref, ga_ref, gf_ref, x_ref, wdn_ref, wf_ref, wo_ref, g_ref,
                  x1_ref, h2_ref):
    ya = _dot(og_ref[...], wdn_ref[...])
    yf = _dot(fr_ref[...], wf_ref[...])
    merged = _sigmoid(ga_ref[...]) * ya + _sigmoid(gf_ref[...]) * yf
    x1 = x_ref[...] + _dot(merged.astype(BF16), wo_ref[...])
    x1_ref[...] = x1
    ms = jnp.mean(x1 * x1, axis=-1, keepdims=True)
    h2_ref[...] = (x1 * lax.rsqrt(ms + EPS) * g_ref[...]).astype(BF16)


def _merge(og2, fr, proj2, x2d, w_dn, w_f, w_o, g_mlp, *, t, tm):
    m, d = x2d.shape
    bw = og2.shape[1]
    tpb = t // tm

    def const(shape):
        return pl.BlockSpec(shape, lambda i: (0, 0), pipeline_mode=pl.Buffered(1))

    return pl.pallas_call(
        _merge_kernel,
        grid=(m // tm,),
        in_specs=[
            pl.BlockSpec((tm, bw), lambda i: (i, 0)),
            pl.BlockSpec((tm, bw), lambda i: (i % tpb, i // tpb)),
            pl.BlockSpec((tm, d), lambda i: (i, 0)),
            pl.BlockSpec((tm, d), lambda i: (i, 1)),
            pl.BlockSpec((tm, d), lambda i: (i, 0)),
            const((bw, d)), const((bw, d)), const((d, d)), const((1, d)),
        ],
        out_specs=[pl.BlockSpec((tm, d), lambda i: (i, 0)), pl.BlockSpec((tm, d), lambda i: (i, 0))],
        out_shape=[jax.ShapeDtypeStruct((m, d), F32), jax.ShapeDtypeStruct((m, d), BF16)],
        compiler_params=pltpu.CompilerParams(
            dimension_semantics=("parallel",), vmem_limit_bytes=VMEM_LIMIT),
        name="merge",
    )(og2, fr, proj2, proj2, x2d, w_dn, w_f, w_o, g_mlp)


def _mlp_kernel(h_ref, wu_ref, wd_ref, x1_ref, g_ref, o_ref):
    j = pl.program_id(1)

    @pl.when(j == 0)
    def _():
        o_ref[...] = x1_ref[...]

    a = jnp.maximum(_dot(h_ref[...], wu_ref[...]), 0.0)
    o_ref[...] += _dot((a * a).astype(BF16), wd_ref[...])

    @pl.when(j == pl.num_programs(1) - 1)
    def _():
        y = o_ref[...]
        ms = jnp.mean(y * y, axis=-1, keepdims=True)
        o_ref[...] = y * lax.rsqrt(ms + EPS) * g_ref[...]


def _mlp(h2, w_up, w_down, x1, g_final, *, tm, tf):
    m, d = x1.shape
    ff = w_up.shape[1]
    return pl.pallas_call(
        _mlp_kernel,
        grid=(m // tm, ff // tf),
        in_specs=[
            pl.BlockSpec((tm, d), lambda i, j: (i, 0)),
            pl.BlockSpec((d, tf), lambda i, j: (0, j)),
            pl.BlockSpec((tf, d), lambda i, j: (j, 0)),
            pl.BlockSpec((tm, d), lambda i, j: (i, 0)),
            pl.BlockSpec((1, d), lambda i, j: (0, 0)),
        ],
        out_specs=pl.BlockSpec((tm, d), lambda i, j: (i, 0)),
        out_shape=jax.ShapeDtypeStruct((m, d), F32),
        compiler_params=pltpu.CompilerParams(
            dimension_semantics=("parallel", "arbitrary"), vmem_limit_bytes=VMEM_LIMIT),
        name="mlp",
    )(h2, w_up, w_down, x1, g_final)


def _largest_divisor(n, cap, multiple):
    return max(c for c in range(multiple, cap + 1, multiple) if n % c == 0)


def _layer(x, g_mix, w_in, conv_w, a_log_f, a_log_b, dt_f, dt_b, g_head, w_dn_up, w_fourier, w_o,
           g_mlp, w_mlp_up, w_mlp_down, g_out):
    b, t, d = x.shape
    m = b * t
    dn = N_HEADS * HEAD_DIM
    fw = N_GROUPS * GROUP_DIM
    o_q, o_k, o_v, o_z, o_f, o_s, o_g = np.cumsum([0, dn, dn, dn, dn, fw, 4 * N_HEADS])
    w_t32 = jnp.swapaxes(w_in, 0, 1)
    tn = 1024
    row_starts = tuple(range(int(o_g), w_t32.shape[0], tn)) + tuple(range(int(o_q), int(o_s), tn))
    q_blk = (2 * d) // HEAD_DIM
    z_blk = q_blk + 3 * N_HEADS

    def gate_row(fwd, bwd):
        return jnp.pad(jnp.concatenate([fwd, bwd]).astype(F32),
                       (2 * N_HEADS, LANES - 4 * N_HEADS)).reshape(1, LANES)

    x2d = x.reshape(m, d)
    w_t, h2d, gates, gates_t = _prologue(w_t32, x2d, g_mix.reshape(1, d), gate_row(a_log_f, a_log_b),
                                         gate_row(dt_f, dt_b), tr=_largest_divisor(w_t32.shape[0], 1024, 32),
                                         xr=min(m, 512), scal_start=int(o_s), n_scal=4 * N_HEADS)
    c_half, s_half, perm, wd = _dft_tables(t, GROUP_DIM)
    assert tn == fw and row_starts[-1] == o_f, "the channel DFT rides on the last (Fourier) column step"
    proj, g2 = _in_proj(h2d, w_t, jnp.asarray(wd).astype(BF16), tm=min(m, 1024), tn=tn,
                        row_starts=row_starts, t=t)
    proj3 = proj.reshape(b, t, proj.shape[1])

    og, w_dn16, w_f16, w_o16, w_up16, w_down16 = _delta(
        proj3, gates.reshape(b, t, LANES), gates_t, conv_w, g_head.reshape(1, HEAD_DIM),
        [w_dn_up, w_fourier, w_o, w_mlp_up, w_mlp_down], q_blk=q_blk, z_blk=z_blk)

    fr = _dft_seq(g2, jnp.asarray(c_half).astype(BF16), jnp.asarray(s_half).astype(BF16),
                  jnp.asarray(perm).astype(BF16), scale=float((t * GROUP_DIM) ** -0.5), tn=fw)

    x1, h2 = _merge(og.reshape(m, dn), fr, proj, x2d, w_dn16, w_f16, w_o16, g_mlp.reshape(1, d),
                    t=t, tm=min(t, 256))
    return _mlp(h2, w_up16, w_down16, x1, g_out.reshape(1, d), tm=min(m, 512), tf=1024)


def kernel(x, g_mix, w_in, conv_w, a_log_fwd, a_log_bwd, dt_bias_fwd, dt_bias_bwd, g_dn_head, w_dn_up,
           w_fourier, w_o, g_mlp, w_mlp_up, w_mlp_down, g_final):
    depth = g_mix.shape[0]
    assert depth == 1, "the final rmsnorm is fused into the last block's MLP kernel"
    b, t, d = x.shape
    out = _layer(x, g_mix[0], w_in[0], conv_w[0], a_log_fwd[0], a_log_bwd[0], dt_bias_fwd[0],
                 dt_bias_bwd[0], g_dn_head[0], w_dn_up[0], w_fourier[0], w_o[0], g_mlp[0],
                 w_mlp_up[0], w_mlp_down[0], g_final)
    return out.reshape(b, t, d)
```

```python
import functools

import jax
import jax.numpy as jnp
import numpy as np
from jax import lax
from jax.experimental import pallas as pl
from jax.experimental.pallas import tpu as pltpu

F32 = jnp.float32
BF16 = jnp.bfloat16

EPS = 1e-6
N_HEADS = 8
HEAD_DIM = 128
CHUNK = 64
CONV_WIDTH = 5
N_GROUPS = 8
GROUP_DIM = 128
LANES = 128
SUBLANES = 8
VMEM_LIMIT = 56 * 1024 * 1024

_NT = (((1,), (1,)), ((), ()))


def _dot(a, b):
    return jnp.dot(a, b, preferred_element_type=F32)


def _dot_nt(a, b):
    return lax.dot_general(a, b, _NT, preferred_element_type=F32)


def _split3(x):
    hi = x.astype(BF16)
    r = x - hi.astype(F32)
    mid = r.astype(BF16)
    r = r - mid.astype(F32)
    return hi, mid, r.astype(BF16)


def _sigmoid(x):
    return 1.0 / (1.0 + jnp.exp(-x))


def _silu(x):
    return x * _sigmoid(x)


def _softplus(x):
    return jnp.maximum(x, 0.0) + jnp.log1p(jnp.exp(-jnp.abs(x)))


GATE_ROWS = 256


def _prologue_kernel(w_ref, x_ref, g_ref, ws_ref, alog_ref, dt_ref, w16_ref, h_ref, gates_ref, gates_t_ref,
                     ws_scr, *, xr, rb, n_x_steps):
    w16_ref[...] = w_ref[...].astype(BF16)

    @pl.when(pl.program_id(0) == 0)
    def _():
        ws_scr[...] = jnp.zeros(ws_scr.shape, BF16)
        ws_scr[0:ws_ref.shape[0], :] = ws_ref[...].astype(BF16)

    @pl.when(pl.program_id(0) < n_x_steps)
    def _():
        def body(r, c):
            rows = pl.ds(pl.multiple_of(r * rb, rb), rb)
            xf = x_ref[rows, :]
            ms = jnp.mean(xf * xf, axis=-1, keepdims=True)
            h_ref[rows, :] = (xf * lax.rsqrt(ms + EPS) * g_ref[...]).astype(BF16)
            return c
        lax.fori_loop(0, xr // rb, body, 0)

        gr = GATE_ROWS
        ri = lax.broadcasted_iota(jnp.int32, (gr, gr), 0)
        ci = lax.broadcasted_iota(jnp.int32, (gr, gr), 1)
        same = (ri // CHUNK) == (ci // CHUNK)
        ltri = jnp.where(same & (ri >= ci), 1.0, 0.0).astype(BF16)
        utri = jnp.where(same & (ri <= ci), 1.0, 0.0).astype(BF16)
        lane = lax.broadcasted_iota(jnp.int32, (gr, LANES), 1)
        neg_a = -jnp.exp(alog_ref[...])
        for sub in range(xr // gr):
            rows = slice(sub * gr, (sub + 1) * gr)
            s = _dot_nt(h_ref[rows, :], ws_scr[...])
            g3 = _split3(neg_a * _softplus(s + dt_ref[...]))
            gc_f = _dot(ltri, g3[0]) + _dot(ltri, g3[1]) + _dot(ltri, g3[2])
            gc_b = _dot(utri, g3[0]) + _dot(utri, g3[1]) + _dot(utri, g3[2])
            tile = jnp.where(lane < 2 * N_HEADS, _sigmoid(s), jnp.where(lane < 3 * N_HEADS, gc_f, gc_b))
            gates_ref[rows, :] = tile
            gates_t_ref[:, rows] = tile.T


def _prologue(w_t, x2d, g, alog_row, dt_row, *, tr, xr, scal_start, n_scal):
    rows, d = w_t.shape
    assert scal_start % n_scal == 0 and n_scal % SUBLANES == 0 and n_scal <= LANES
    m = x2d.shape[0]
    n_steps, n_x = rows // tr, m // xr
    assert n_x <= n_steps and xr % GATE_ROWS == 0
    xi = lambda i: jnp.minimum(i, n_x - 1)
    return pl.pallas_call(
        functools.partial(_prologue_kernel, xr=xr, rb=128, n_x_steps=n_x),
        grid=(n_steps,),
        in_specs=[
            pl.BlockSpec((tr, d), lambda i: (i, 0)),
            pl.BlockSpec((xr, d), lambda i: (xi(i), 0)),
            pl.BlockSpec((1, d), lambda i: (0, 0)),
            pl.BlockSpec((n_scal, d), lambda i: (scal_start // n_scal, 0)),
            pl.BlockSpec((1, LANES), lambda i: (0, 0)),
            pl.BlockSpec((1, LANES), lambda i: (0, 0)),
        ],
        out_specs=[
            pl.BlockSpec((tr, d), lambda i: (i, 0)),
            pl.BlockSpec((xr, d), lambda i: (xi(i), 0)),
            pl.BlockSpec((xr, LANES), lambda i: (xi(i), 0)),
            pl.BlockSpec((LANES, xr), lambda i: (0, xi(i))),
        ],
        out_shape=[jax.ShapeDtypeStruct((rows, d), BF16), jax.ShapeDtypeStruct((m, d), BF16),
                   jax.ShapeDtypeStruct((m, LANES), F32), jax.ShapeDtypeStruct((LANES, m), F32)],
        scratch_shapes=[pltpu.VMEM((LANES, d), BF16)],
        compiler_params=pltpu.CompilerParams(dimension_semantics=("arbitrary",), vmem_limit_bytes=VMEM_LIMIT),
        name="prologue",
    )(w_t, x2d, g, w_t, alog_row, dt_row)


def _in_proj_kernel(st_ref, h_ref, w_ref, wd_ref, o_ref, g2_ref):
    o_ref[...] = _dot_nt(h_ref[...], w_ref[...])

    @pl.when(pl.program_id(1) == pl.num_programs(1) - 1)
    def _():
        for g in range(N_GROUPS):
            cols = slice(g * GROUP_DIM, (g + 1) * GROUP_DIM)
            y = _dot(o_ref[:, cols].astype(BF16), wd_ref[...])
            g2_ref[0, :, cols] = y[:, :GROUP_DIM].astype(BF16)
            g2_ref[1, :, cols] = y[:, GROUP_DIM:].astype(BF16)


def _in_proj(h2d, w_t, wd, *, tm, tn, row_starts, t):
    m, d = h2d.shape
    starts = jnp.asarray(row_starts, jnp.int32)
    n_steps = len(row_starts)
    return pl.pallas_call(
        _in_proj_kernel,
        grid_spec=pltpu.PrefetchScalarGridSpec(
            num_scalar_prefetch=1,
            grid=(m // tm, n_steps),
            in_specs=[
                pl.BlockSpec((tm, d), lambda i, j, st: (i, 0)),
                pl.BlockSpec((pl.Element(tn), pl.Element(d)), lambda i, j, st: (pl.multiple_of(st[j], 32), 0)),
                pl.BlockSpec((GROUP_DIM, 2 * GROUP_DIM), lambda i, j, st: (0, 0)),
            ],
            out_specs=[
                pl.BlockSpec((tm, tn), lambda i, j, st: (i, j)),
                pl.BlockSpec((2, tm, tn), lambda i, j, st: (0, i % (t // tm), i // (t // tm))),
            ],
        ),
        out_shape=[jax.ShapeDtypeStruct((m, n_steps * tn), F32),
                   jax.ShapeDtypeStruct((2, t, (m // t) * tn), BF16)],
        compiler_params=pltpu.CompilerParams(
            dimension_semantics=("parallel", "arbitrary"), vmem_limit_bytes=VMEM_LIMIT),
        name="in_proj",
    )(starts, h2d, w_t, wd)


def _delta_kernel(*refs, t, rb, gsz, n_cast):
    (q_ref, k_ref, v_ref, z_ref, gates_ref, gates_t_ref, cwq_ref, cwk_ref, cwv_ref, gh_ref), refs = (
        refs[:10], refs[10:])
    cast_in, (out_ref,), cast_out, refs = (
        refs[:n_cast], refs[n_cast:n_cast + 1], refs[n_cast + 1:2 * n_cast + 1], refs[2 * n_cast + 1:])
    pad_scr, qkv_all, o_scr, kw0, r0, qp0, op0, gl0, kw1, r1, qp1, op1, gl1 = refs

    step = pl.program_id(0)
    h = lax.rem(jnp.maximum(step - 1, 0), N_HEADS)
    qkv_scr = qkv_all.at[lax.rem(step + 1, 2)]
    qkv_next = qkv_all.at[lax.rem(step, 2)]
    nc = t // CHUNK
    ng = nc // gsz
    nrb = t // rb

    half = (CONV_WIDTH - 1) // 2
    streams = ((q_ref, cwq_ref), (k_ref, cwk_ref), (v_ref, cwv_ref))

    def conv_edges():
        zero8 = jnp.zeros((SUBLANES, LANES), F32)
        for idx, (src_ref, _) in enumerate(streams):
            pad_scr[0, idx, 0:SUBLANES, :] = zero8
            pad_scr[0, idx, SUBLANES:rb + 2 * SUBLANES, :] = src_ref[0:rb + SUBLANES, :]
            pad_scr[1, idx, 0:rb + SUBLANES, :] = src_ref[t - rb - SUBLANES:t, :]
            pad_scr[1, idx, rb + SUBLANES:rb + 2 * SUBLANES, :] = zero8

    def conv_block(r):
        for idx, (src_ref, cw_ref) in enumerate(streams):
            acc = None
            for i in range(CONV_WIDTH):
                if r == 0 or r == nrb - 1:
                    off = SUBLANES - half + i
                    tap = pad_scr[0 if r == 0 else 1, idx, off:off + rb, :]
                else:
                    off = r * rb - half + i
                    tap = src_ref[off:off + rb, :]
                term = cw_ref[i:i + 1, :] * tap
                acc = term if acc is None else acc + term
            y = _silu(acc)
            if idx < 2:
                y = y * lax.rsqrt(jnp.sum(y * y, axis=-1, keepdims=True) + EPS)
            if idx == 0:
                y = y * (HEAD_DIM ** -0.5)
            qkv_next[idx, r * rb:(r + 1) * rb, :] = y

    @pl.when(step == 0)
    def _():
        conv_edges()
        for r in range(nrb):
            conv_block(r)

    @pl.when(step > 0)
    def _():
        _delta_main(h, qkv_scr, z_ref, gates_ref, gates_t_ref, gh_ref, cast_in, out_ref, cast_out, o_scr,
                    ((kw0, r0, qp0, op0, gl0), (kw1, r1, qp1, op1, gl1)), conv_edges, conv_block,
                    nc=nc, ng=ng, nrb=nrb, gsz=gsz)


def _delta_main(h, qkv_scr, z_ref, gates_ref, gates_t_ref, gh_ref, cast_in, out_ref, cast_out, o_scr, bufs,
                conv_edges, conv_block, *, nc, ng, nrb, gsz):
    for src, dst in zip(cast_in, cast_out):
        dst[...] = src[...].astype(dst.dtype)
    conv_edges()

    lane = lax.broadcasted_iota(jnp.int32, (CHUNK, LANES), 1)
    ri = lax.broadcasted_iota(jnp.int32, (CHUNK, LANES), 0)
    lo = lane < CHUNK
    ci = jnp.where(lo, lane, lane - CHUNK)
    ahead = jnp.where(lo, ri - ci, ci - ri)
    incl = ahead >= 0
    strict = ahead > 0
    eye = jnp.where(ri == ci, 1.0, 0.0)
    lane_row = lax.broadcasted_iota(jnp.int32, (1, LANES), 1)
    head_row = lax.broadcasted_iota(jnp.int32, (N_HEADS, LANES), 0)

    def chunk_of(gi, j, d):
        return gi * gsz + j if d == 0 else nc - 1 - (gi * gsz + j)

    def column(tile, c):
        return jnp.sum(jnp.where(lane == c, tile, 0.0), axis=-1, keepdims=True)

    def pack(m):
        return jnp.where(lo, m[:CHUNK], m[CHUNK:])

    def blockdiag(m):
        zero = jnp.zeros_like(m)
        return jnp.concatenate([jnp.where(lo, m, zero), jnp.where(lo, zero, m)], axis=0)

    def prep(gi, buf):
        kw_s, r_s, qp_s, op_s, gl_s = buf
        ch = [(j, d) for j in range(gsz) for d in range(2)]
        rows = [pl.ds(chunk_of(gi, j, d) * CHUNK, CHUNK) for j, d in ch]
        q = [qkv_scr[0, r, :] for r in rows]
        k = [qkv_scr[1, r, :] for r in rows]
        v = [qkv_scr[2, r, :] for r in rows]
        gt = [gates_ref[r, :] for r in rows]
        beta = [column(a, d * N_HEADS + h) for a, (j, d) in zip(gt, ch)]
        g = [column(a, (2 + d) * N_HEADS + h) for a, (j, d) in zip(gt, ch)]
        gl = [a[CHUNK - 1:CHUNK, :] if d == 0 else a[0:1, :] for a, (j, d) in zip(g, ch)]
        kb = [a * bt for a, bt in zip(k, beta)]

        def g_row(j, d):
            c = chunk_of(gi, j, d)
            odd = c % 2
            win = gates_t_ref[(2 + d) * N_HEADS:(3 + d) * N_HEADS, pl.ds((c - odd) * CHUNK, 2 * CHUNK)]
            win = jnp.sum(jnp.where(head_row == h, win, 0.0), axis=0, keepdims=True)
            return pltpu.roll(win, CHUNK, axis=1) if odd != d else win

        pair = lambda xs, j: jnp.concatenate([xs[2 * j], xs[2 * j + 1]], axis=0)
        k2 = [pair(k, j).astype(BF16) for j in range(gsz)]
        kk = [pack(_dot_nt(pair(kb, j).astype(BF16), k2[j])) for j in range(gsz)]
        qk_raw = [pack(_dot_nt(pair(q, j).astype(BF16), k2[j])) for j in range(gsz)]
        diff = [jnp.where(lo, g[2 * j], g[2 * j + 1])
                - jnp.where(lane_row < CHUNK, g_row(j, 0), g_row(j, 1)) for j in range(gsz)]
        decay = [jnp.where(incl, jnp.exp(jnp.where(incl, a, 0.0)), 0.0) for a in diff]
        pw = [jnp.where(strict, -(a * dc), 0.0) for a, dc in zip(kk, decay)]
        x = [eye + a for a in pw]
        p16 = [a.astype(BF16) for a in pw]
        p16 = [_dot(a, blockdiag(a)).astype(BF16) for a in p16]
        for _ in range(4):
            res = [_dot(jnp.concatenate([a, b.astype(BF16)], axis=0), blockdiag(a)) for a, b in zip(p16, x)]
            p16 = [a[:CHUNK].astype(BF16) for a in res]
            x = [a + b[CHUNK:] for a, b in zip(x, res)]
        x = [a + _dot(a.astype(BF16), blockdiag(b)) for a, b in zip(x, p16)]
        qk2 = [a * dc for a, dc in zip(qk_raw, decay)]
        unpack = lambda xs: [(xs[j][:, :CHUNK] if d == 0 else xs[j][:, CHUNK:]).astype(BF16) for j, d in ch]
        xinv = unpack(x)
        qk = unpack(qk2)
        eg = [jnp.exp(a) for a in g]
        rhs = [jnp.concatenate([a * bt, b * e], axis=1).astype(BF16)
               for a, bt, b, e in zip(v, beta, kb, eg)]
        sol = [_dot(a, b).astype(BF16) for a, b in zip(xinv, rhs)]
        kd = [(a * jnp.exp(l - b)).T.astype(BF16) for a, b, l in zip(k, g, gl)]
        kdu_kdw = [_dot(a, b) for a, b in zip(kd, sol)]
        qku_qkw = [_dot(a, b) for a, b in zip(qk, sol)]
        for i, (j, d) in enumerate(ch):
            srows = slice(j * HEAD_DIM, (j + 1) * HEAD_DIM)
            crows = slice(j * CHUNK, (j + 1) * CHUNK)
            r_s[d, srows, :] = kdu_kdw[i][:, :HEAD_DIM]
            kw_s[d, srows, :] = kdu_kdw[i][:, HEAD_DIM:].astype(BF16)
            op_s[d, crows, :] = qku_qkw[i][:, :HEAD_DIM]
            qp_s[d, crows, :] = (q[i] * eg[i] - qku_qkw[i][:, HEAD_DIM:]).astype(BF16)
            gl_s[d, j * SUBLANES:(j + 1) * SUBLANES, :] = jnp.broadcast_to(jnp.exp(gl[i]), (SUBLANES, LANES))

    seen = (set(), set())

    def scan(gi, buf, carry, fill=()):
        kw_s, r_s, qp_s, op_s, gl_s = buf
        st = list(carry)
        fill = list(fill)
        stride = max(1, gsz // max(1, len(fill)))
        assert len(fill) <= gsz
        for j in range(gsz):
            if fill and j % stride == 0:
                fill.pop(0)()
            srows = slice(j * HEAD_DIM, (j + 1) * HEAD_DIM)
            crows = slice(j * CHUNK, (j + 1) * CHUNK)
            for d in range(2):
                c = chunk_of(gi, j, d)
                rows = pl.ds(c * CHUNK, CHUNK)
                st16 = st[d].astype(BF16)
                o = _dot(qp_s[d, crows, :], st16) + op_s[d, crows, :]
                if c in seen[1 - d]:
                    o = o + o_scr[1 - d, rows, :]
                    y = o * lax.rsqrt(jnp.mean(o * o, axis=-1, keepdims=True) + EPS) * gh_ref[...]
                    out_ref[rows, :] = (y * _silu(z_ref[rows, :])).astype(BF16)
                else:
                    o_scr[d, rows, :] = o
                seen[d].add(c)
                gl = gl_s[d, j * SUBLANES:j * SUBLANES + 1, :]
                st[d] = gl * st[d] + (r_s[d, srows, :] - _dot(kw_s[d, srows, :], st16))
        assert not fill
        return tuple(st)

    carry = (jnp.zeros((HEAD_DIM, HEAD_DIM), F32),) * 2
    prep(0, bufs[0])
    for gi in range(1, ng):
        prep(gi, bufs[gi % 2])
        carry = scan(gi - 1, bufs[(gi - 1) % 2], carry)
    scan(ng - 1, bufs[(ng - 1) % 2], carry, fill=[functools.partial(conv_block, r) for r in range(nrb)])
    assert all(len(sn) == nc for sn in seen)


def _delta(proj3, gates3, gates_t, conv_w, g_head, cast_weights, *, q_blk, z_blk):
    b, t, _ = proj3.shape
    n_pairs = b * N_HEADS
    rb = min(t // 2, 256)
    gsz = min(16, t // (2 * CHUNK))
    assert (t // CHUNK) % (2 * gsz) == 0

    def conv_pair(s):
        p = jnp.minimum(s, n_pairs - 1)
        return p // N_HEADS, lax.rem(p, N_HEADS)

    def main_pair(s):
        p = jnp.maximum(s - 1, 0)
        return p // N_HEADS, lax.rem(p, N_HEADS)

    def col(off):
        return pl.BlockSpec((None, t, HEAD_DIM), lambda s: (conv_pair(s)[0], 0, off + conv_pair(s)[1]))

    def cw(off):
        return pl.BlockSpec((CONV_WIDTH, HEAD_DIM), lambda s: (0, off + conv_pair(s)[1]))

    slab_specs = [pl.BlockSpec((w.shape[0] // n_pairs, w.shape[1]), lambda s: (jnp.maximum(s - 1, 0), 0))
                  for w in cast_weights]
    operands = [
        pltpu.VMEM((2, gsz * HEAD_DIM, HEAD_DIM), BF16),
        pltpu.VMEM((2, gsz * HEAD_DIM, HEAD_DIM), F32),
        pltpu.VMEM((2, gsz * CHUNK, HEAD_DIM), BF16),
        pltpu.VMEM((2, gsz * CHUNK, HEAD_DIM), F32),
        pltpu.VMEM((2, gsz * SUBLANES, LANES), F32),
    ]
    return pl.pallas_call(
        functools.partial(_delta_kernel, t=t, rb=rb, gsz=gsz, n_cast=len(cast_weights)),
        grid=(n_pairs + 1,),
        in_specs=[
            col(q_blk), col(q_blk + N_HEADS), col(q_blk + 2 * N_HEADS),
            pl.BlockSpec((None, t, HEAD_DIM), lambda s: (main_pair(s)[0], 0, z_blk + main_pair(s)[1])),
            pl.BlockSpec((None, t, LANES), lambda s: (main_pair(s)[0], 0, 0)),
            pl.BlockSpec((LANES, t), lambda s: (0, main_pair(s)[0])),
            cw(0), cw(N_HEADS), cw(2 * N_HEADS),
            pl.BlockSpec((1, HEAD_DIM), lambda s: (0, 0)),
        ] + slab_specs,
        out_specs=[pl.BlockSpec((None, t, HEAD_DIM), lambda s: (main_pair(s)[0], 0, main_pair(s)[1]))] + slab_specs,
        out_shape=[jax.ShapeDtypeStruct((b, t, N_HEADS * HEAD_DIM), BF16)]
        + [jax.ShapeDtypeStruct(w.shape, BF16) for w in cast_weights],
        scratch_shapes=[
            pltpu.VMEM((2, 3, rb + 2 * SUBLANES, LANES), F32),
            pltpu.VMEM((2, 3, t, HEAD_DIM), F32),
            pltpu.VMEM((2, t, HEAD_DIM), F32),
        ] + operands + operands,
        compiler_params=pltpu.CompilerParams(
            dimension_semantics=("arbitrary",), vmem_limit_bytes=VMEM_LIMIT),
        name="delta",
    )(proj3, proj3, proj3, proj3, gates3, gates_t, conv_w, conv_w, conv_w, g_head, *cast_weights)


DFT_COLS = 256


def _dft_seq_kernel(g_ref, c_ref, s_ref, p_ref, o_ref, *, scale):
    half = s_ref.shape[0]
    row = lax.broadcasted_iota(jnp.int32, (half, DFT_COLS), 0)
    for c0 in range(0, o_ref.shape[1], DFT_COLS):
        cols = slice(c0, c0 + DFT_COLS)
        a = _dot(c_ref[...], g_ref[0, :, cols]) * scale
        b = _dot(s_ref[...], g_ref[1, :, cols]) * scale
        o_ref[0:half, cols] = (a[:half] + b).astype(o_ref.dtype)
        mirrored = _dot(p_ref[...], (a[:half] - b).astype(BF16))
        o_ref[half:2 * half, cols] = jnp.where(row == 0, a[half:half + 1], mirrored).astype(o_ref.dtype)


def _dft_seq(g2, c_half, s_half, perm, *, scale, tn):
    _, t, n = g2.shape

    def const(shape):
        return pl.BlockSpec(shape, lambda j: (0, 0), pipeline_mode=pl.Buffered(1))

    return pl.pallas_call(
        functools.partial(_dft_seq_kernel, scale=scale),
        grid=(n // tn,),
        in_specs=[
            pl.BlockSpec((2, t, tn), lambda j: (0, 0, j)),
            const(c_half.shape), const(s_half.shape), const(perm.shape),
        ],
        out_specs=pl.BlockSpec((t, tn), lambda j: (0, j)),
        out_shape=jax.ShapeDtypeStruct((t, n), BF16),
        compiler_params=pltpu.CompilerParams(dimension_semantics=("parallel",), vmem_limit_bytes=VMEM_LIMIT),
        name="dft_seq",
    )(g2, c_half, s_half, perm)


@functools.lru_cache(maxsize=None)
def _dft_tables(t, d):
    half = t // 2
    k = np.arange(half + SUBLANES, dtype=np.int64)
    n = np.arange(t, dtype=np.int64)
    ang = 2.0 * np.pi * ((k[:, None] * n[None, :]) % t) / t
    c_half = np.cos(ang).astype(np.float32)
    s_half = np.sin(ang[:half]).astype(np.float32)
    perm = np.zeros((half, half), np.float32)
    perm[np.arange(1, half), half - np.arange(1, half)] = 1.0
    c = np.arange(d, dtype=np.int64)
    angd = 2.0 * np.pi * ((c[:, None] * c[None, :]) % d) / d
    wd = np.concatenate([np.cos(angd), -np.sin(angd)], axis=1).astype(np.float32)
    return c_half, s_half, perm, wd


def _merge_kernel(og_ref, fr_ref, ga_ref, gf_ref, x_ref, wdn_ref, wf_ref, wo_ref, g_ref,
                  x1_ref, h2_ref):
    ya = _dot(og_ref[...], wdn_ref[...])
    yf = _dot(fr_ref[...], wf_ref[...])
    merged = _sigmoid(ga_ref[...]) * ya + _sigmoid(gf_ref[...]) * yf
    x1 = x_ref[...] + _dot(merged.astype(BF16), wo_ref[...])
    x1_ref[...] = x1
    ms = jnp.mean(x1 * x1, axis=-1, keepdims=True)
    h2_ref[...] = (x1 * lax.rsqrt(ms + EPS) * g_ref[...]).astype(BF16)


def _merge(og2, fr, proj2, x2d, w_dn, w_f, w_o, g_mlp, *, t, tm):
    m, d = x2d.shape
    bw = og2.shape[1]
    tpb = t // tm

    def const(shape):
        return pl.BlockSpec(shape, lambda i: (0, 0), pipeline_mode=pl.Buffered(1))

    return pl.pallas_call(
        _merge_kernel,
        grid=(m // tm,),
        in_specs=[
            pl.BlockSpec((tm, bw), lambda i: (i, 0)),
            pl.BlockSpec((tm, bw), lambda i: (i % tpb, i // tpb)),
            pl.BlockSpec((tm, d), lambda i: (i, 0)),
            pl.BlockSpec((tm, d), lambda i: (i, 1)),
            pl.BlockSpec((tm, d), lambda i: (i, 0)),
            const((bw, d)), const((bw, d)), const((d, d)), const((1, d)),
        ],
        out_specs=[pl.BlockSpec((tm, d), lambda i: (i, 0)), pl.BlockSpec((tm, d), lambda i: (i, 0))],
        out_shape=[jax.ShapeDtypeStruct((m, d), F32), jax.ShapeDtypeStruct((m, d), BF16)],
        compiler_params=pltpu.CompilerParams(
            dimension_semantics=("parallel",), vmem_limit_bytes=VMEM_LIMIT),
        name="merge",
    )(og2, fr, proj2, proj2, x2d, w_dn, w_f, w_o, g_mlp)


def _mlp_kernel(h_ref, wu_ref, wd_ref, x1_ref, g_ref, o_ref):
    j = pl.program_id(1)

    @pl.when(j == 0)
    def _():
        o_ref[...] = x1_ref[...]

    a = jnp.maximum(_dot(h_ref[...], wu_ref[...]), 0.0)
    o_ref[...] += _dot((a * a).astype(BF16), wd_ref[...])

    @pl.when(j == pl.num_programs(1) - 1)
    def _():
        y = o_ref[...]
        ms = jnp.mean(y * y, axis=-1, keepdims=True)
        o_ref[...] = y * lax.rsqrt(ms + EPS) * g_ref[...]


def _mlp(h2, w_up, w_down, x1, g_final, *, tm, tf):
    m, d = x1.shape
    ff = w_up.shape[1]
    return pl.pallas_call(
        _mlp_kernel,
        grid=(m // tm, ff // tf),
        in_specs=[
            pl.BlockSpec((tm, d), lambda i, j: (i, 0)),
            pl.BlockSpec((d, tf), lambda i, j: (0, j)),
            pl.BlockSpec((tf, d), lambda i, j: (j, 0)),
            pl.BlockSpec((tm, d), lambda i, j: (i, 0)),
            pl.BlockSpec((1, d), lambda i, j: (0, 0)),
        ],
        out_specs=pl.BlockSpec((tm, d), lambda i, j: (i, 0)),
        out_shape=jax.ShapeDtypeStruct((m, d), F32),
        compiler_params=pltpu.CompilerParams(
            dimension_semantics=("parallel", "arbitrary"), vmem_limit_bytes=VMEM_LIMIT),
        name="mlp",
    )(h2, w_up, w_down, x1, g_final)


def _largest_divisor(n, cap, multiple):
    return max(c for c in range(multiple, cap + 1, multiple) if n % c == 0)


def _layer(x, g_mix, w_in, conv_w, a_log_f, a_log_b, dt_f, dt_b, g_head, w_dn_up, w_fourier, w_o,
           g_mlp, w_mlp_up, w_mlp_down, g_out):
    b, t, d = x.shape
    m = b * t
    dn = N_HEADS * HEAD_DIM
    fw = N_GROUPS * GROUP_DIM
    o_q, o_k, o_v, o_z, o_f, o_s, o_g = np.cumsum([0, dn, dn, dn, dn, fw, 4 * N_HEADS])
    w_t32 = jnp.swapaxes(w_in, 0, 1)
    tn = 1024
    row_starts = tuple(range(int(o_g), w_t32.shape[0], tn)) + tuple(range(int(o_q), int(o_s), tn))
    q_blk = (2 * d) // HEAD_DIM
    z_blk = q_blk + 3 * N_HEADS

    def gate_row(fwd, bwd):
        return jnp.pad(jnp.concatenate([fwd, bwd]).astype(F32),
                       (2 * N_HEADS, LANES - 4 * N_HEADS)).reshape(1, LANES)

    x2d = x.reshape(m, d)
    w_t, h2d, gates, gates_t = _prologue(w_t32, x2d, g_mix.reshape(1, d), gate_row(a_log_f, a_log_b),
                                         gate_row(dt_f, dt_b), tr=_largest_divisor(w_t32.shape[0], 1024, 32),
                                         xr=min(m, 512), scal_start=int(o_s), n_scal=4 * N_HEADS)
    c_half, s_half, perm, wd = _dft_tables(t, GROUP_DIM)
    assert tn == fw and row_starts[-1] == o_f, "the channel DFT rides on the last (Fourier) column step"
    proj, g2 = _in_proj(h2d, w_t, jnp.asarray(wd).astype(BF16), tm=min(m, 1024), tn=tn,
                        row_starts=row_starts, t=t)
    proj3 = proj.reshape(b, t, proj.shape[1])

    og, w_dn16, w_f16, w_o16, w_up16, w_down16 = _delta(
        proj3, gates.reshape(b, t, LANES), gates_t, conv_w, g_head.reshape(1, HEAD_DIM),
        [w_dn_up, w_fourier, w_o, w_mlp_up, w_mlp_down], q_blk=q_blk, z_blk=z_blk)

    fr = _dft_seq(g2, jnp.asarray(c_half).astype(BF16), jnp.asarray(s_half).astype(BF16),
                  jnp.asarray(perm).astype(BF16), scale=float((t * GROUP_DIM) ** -0.5), tn=fw)

    x1, h2 = _merge(og.reshape(m, dn), fr, proj, x2d, w_dn16, w_f16, w_o16, g_mlp.reshape(1, d),
                    t=t, tm=min(t, 256))
    return _mlp(h2, w_up16, w_down16, x1, g_out.reshape(1, d), tm=min(m, 512), tf=1024)


def kernel(x, g_mix, w_in, conv_w, a_log_fwd, a_log_bwd, dt_bias_fwd, dt_bias_bwd, g_dn_head, w_dn_up,
           w_fourier, w_o, g_mlp, w_mlp_up, w_mlp_down, g_final):
    depth = g_mix.shape[0]
    assert depth == 1, "the final rmsnorm is fused into the last block's MLP kernel"
    b, t, d = x.shape
    out = _layer(x, g_mix[0], w_in[0], conv_w[0], a_log_fwd[0], a_log_bwd[0], dt_bias_fwd[0],
                 dt_bias_bwd[0], g_dn_head[0], w_dn_up[0], w_fourier[0], w_o[0], g_mlp[0],
                 w_mlp_up[0], w_mlp_down[0], g_final)
    return out.reshape(b, t, d)
```

```python
import functools

import jax
import jax.numpy as jnp
import numpy as np
from jax import lax
from jax.experimental import pallas as pl
from jax.experimental.pallas import tpu as pltpu

F32 = jnp.float32
BF16 = jnp.bfloat16

EPS = 1e-6
N_HEADS = 8
HEAD_DIM = 128
CHUNK = 64
CONV_WIDTH = 5
N_GROUPS = 8
GROUP_DIM = 128
LANES = 128
SUBLANES = 8
VMEM_LIMIT = 56 * 1024 * 1024

_NT = (((1,), (1,)), ((), ()))


def _dot(a, b):
    return jnp.dot(a, b, preferred_element_type=F32)


def _dot_nt(a, b):
    return lax.dot_general(a, b, _NT, preferred_element_type=F32)


def _split3(x):
    hi = x.astype(BF16)
    r = x - hi.astype(F32)
    mid = r.astype(BF16)
    r = r - mid.astype(F32)
    return hi, mid, r.astype(BF16)


def _sigmoid(x):
    return 1.0 / (1.0 + jnp.exp(-x))


def _silu(x):
    return x * _sigmoid(x)


def _softplus(x):
    return jnp.maximum(x, 0.0) + jnp.log1p(jnp.exp(-jnp.abs(x)))


GATE_ROWS = 256


def _prologue_kernel(w_ref, x_ref, g_ref, ws_ref, alog_ref, dt_ref, w16_ref, h_ref, gates_ref, gates_t_ref,
                     ws_scr, *, xr, rb, n_x_steps):
    w16_ref[...] = w_ref[...].astype(BF16)

    @pl.when(pl.program_id(0) == 0)
    def _():
        ws_scr[...] = jnp.zeros(ws_scr.shape, BF16)
        ws_scr[0:ws_ref.shape[0], :] = ws_ref[...].astype(BF16)

    @pl.when(pl.program_id(0) < n_x_steps)
    def _():
        def body(r, c):
            rows = pl.ds(pl.multiple_of(r * rb, rb), rb)
            xf = x_ref[rows, :]
            ms = jnp.mean(xf * xf, axis=-1, keepdims=True)
            h_ref[rows, :] = (xf * lax.rsqrt(ms + EPS) * g_ref[...]).astype(BF16)
            return c
        lax.fori_loop(0, xr // rb, body, 0)

        gr = GATE_ROWS
        ri = lax.broadcasted_iota(jnp.int32, (gr, gr), 0)
        ci = lax.broadcasted_iota(jnp.int32, (gr, gr), 1)
        same = (ri // CHUNK) == (ci // CHUNK)
        ltri = jnp.where(same & (ri >= ci), 1.0, 0.0).astype(BF16)
        utri = jnp.where(same & (ri <= ci), 1.0, 0.0).astype(BF16)
        lane = lax.broadcasted_iota(jnp.int32, (gr, LANES), 1)
        neg_a = -jnp.exp(alog_ref[...])
        for sub in range(xr // gr):
            rows = slice(sub * gr, (sub + 1) * gr)
            s = _dot_nt(h_ref[rows, :], ws_scr[...])
            g3 = _split3(neg_a * _softplus(s + dt_ref[...]))
            gc_f = _dot(ltri, g3[0]) + _dot(ltri, g3[1]) + _dot(ltri, g3[2])
            gc_b = _dot(utri, g3[0]) + _dot(utri, g3[1]) + _dot(utri, g3[2])
            tile = jnp.where(lane < 2 * N_HEADS, _sigmoid(s), jnp.where(lane < 3 * N_HEADS, gc_f, gc_b))
            gates_ref[rows, :] = tile
            gates_t_ref[:, rows] = tile.T


def _prologue(w_t, x2d, g, alog_row, dt_row, *, tr, xr, scal_start, n_scal):
    rows, d = w_t.shape
    assert scal_start % n_scal == 0 and n_scal % SUBLANES == 0 and n_scal <= LANES
    m = x2d.shape[0]
    n_steps, n_x = rows // tr, m // xr
    assert n_x <= n_steps and xr % GATE_ROWS == 0
    xi = lambda i: jnp.minimum(i, n_x - 1)
    return pl.pallas_call(
        functools.partial(_prologue_kernel, xr=xr, rb=128, n_x_steps=n_x),
        grid=(n_steps,),
        in_specs=[
            pl.BlockSpec((tr, d), lambda i: (i, 0)),
            pl.BlockSpec((xr, d), lambda i: (xi(i), 0)),
            pl.BlockSpec((1, d), lambda i: (0, 0)),
            pl.BlockSpec((n_scal, d), lambda i: (scal_start // n_scal, 0)),
            pl.BlockSpec((1, LANES), lambda i: (0, 0)),
            pl.BlockSpec((1, LANES), lambda i: (0, 0)),
        ],
        out_specs=[
            pl.BlockSpec((tr, d), lambda i: (i, 0)),
            pl.BlockSpec((xr, d), lambda i: (xi(i), 0)),
            pl.BlockSpec((xr, LANES), lambda i: (xi(i), 0)),
            pl.BlockSpec((LANES, xr), lambda i: (0, xi(i))),
        ],
        out_shape=[jax.ShapeDtypeStruct((rows, d), BF16), jax.ShapeDtypeStruct((m, d), BF16),
                   jax.ShapeDtypeStruct((m, LANES), F32), jax.ShapeDtypeStruct((LANES, m), F32)],
        scratch_shapes=[pltpu.VMEM((LANES, d), BF16)],
        compiler_params=pltpu.CompilerParams(dimension_semantics=("arbitrary",), vmem_limit_bytes=VMEM_LIMIT),
        name="prologue",
    )(w_t, x2d, g, w_t, alog_row, dt_row)


def _in_proj_kernel(st_ref, h_ref, w_ref, wd_ref, og_ref, o_ref, g2_ref, *, n_gate_steps):
    j = pl.program_id(1)

    @pl.when(j < n_gate_steps)
    def _():
        og_ref[...] = _dot_nt(h_ref[...], w_ref[...]).astype(BF16)

    @pl.when(j >= n_gate_steps)
    def _():
        o_ref[...] = _dot_nt(h_ref[...], w_ref[...])

    @pl.when(j == pl.num_programs(1) - 1)
    def _():
        for g in range(N_GROUPS):
            cols = slice(g * GROUP_DIM, (g + 1) * GROUP_DIM)
            y = _dot(o_ref[:, cols].astype(BF16), wd_ref[...])
            g2_ref[0, :, cols] = y[:, :GROUP_DIM].astype(BF16)
            g2_ref[1, :, cols] = y[:, GROUP_DIM:].astype(BF16)


def _in_proj(h2d, w_t, wd, *, tm, tn, row_starts, n_gate_steps, t):
    m, d = h2d.shape
    starts = jnp.asarray(row_starts, jnp.int32)
    n_steps, ng = len(row_starts), n_gate_steps
    return pl.pallas_call(
        functools.partial(_in_proj_kernel, n_gate_steps=ng),
        grid_spec=pltpu.PrefetchScalarGridSpec(
            num_scalar_prefetch=1,
            grid=(m // tm, n_steps),
            in_specs=[
                pl.BlockSpec((tm, d), lambda i, j, st: (i, 0)),
                pl.BlockSpec((pl.Element(tn), pl.Element(d)), lambda i, j, st: (pl.multiple_of(st[j], 32), 0)),
                pl.BlockSpec((GROUP_DIM, 2 * GROUP_DIM), lambda i, j, st: (0, 0)),
            ],
            out_specs=[
                pl.BlockSpec((tm, tn), lambda i, j, st: (i, jnp.minimum(j, ng - 1))),
                pl.BlockSpec((tm, tn), lambda i, j, st: (i, jnp.maximum(j - ng, 0))),
                pl.BlockSpec((2, tm, tn), lambda i, j, st: (0, i % (t // tm), i // (t // tm))),
            ],
        ),
        out_shape=[jax.ShapeDtypeStruct((m, ng * tn), BF16),
                   jax.ShapeDtypeStruct((m, (n_steps - ng) * tn), F32),
                   jax.ShapeDtypeStruct((2, t, (m // t) * tn), BF16)],
        compiler_params=pltpu.CompilerParams(
            dimension_semantics=("parallel", "arbitrary"), vmem_limit_bytes=VMEM_LIMIT),
        name="in_proj",
    )(starts, h2d, w_t, wd)


def _delta_kernel(*refs, t, rb, gsz, n_cast):
    (q_ref, k_ref, v_ref, z_ref, gates_ref, gates_t_ref, cwq_ref, cwk_ref, cwv_ref, gh_ref), refs = (
        refs[:10], refs[10:])
    cast_in, (out_ref,), cast_out, refs = (
        refs[:n_cast], refs[n_cast:n_cast + 1], refs[n_cast + 1:2 * n_cast + 1], refs[2 * n_cast + 1:])
    pad_scr, qkv_scr, o_scr, kw0, r0, qp0, op0, gl0, kw1, r1, qp1, op1, gl1 = refs

    for src, dst in zip(cast_in, cast_out):
        dst[...] = src[...].astype(dst.dtype)

    nc = t // CHUNK
    ng = nc // gsz
    nrb = t // rb
    h = pl.program_id(1)

    half = (CONV_WIDTH - 1) // 2
    streams = ((q_ref, cwq_ref), (k_ref, cwk_ref), (v_ref, cwv_ref))
    zero8 = jnp.zeros((SUBLANES, LANES), F32)
    for idx, (src_ref, _) in enumerate(streams):
        pad_scr[0, idx, 0:SUBLANES, :] = zero8
        pad_scr[0, idx, SUBLANES:rb + 2 * SUBLANES, :] = src_ref[0:rb + SUBLANES, :]
        pad_scr[1, idx, 0:rb + SUBLANES, :] = src_ref[t - rb - SUBLANES:t, :]
        pad_scr[1, idx, rb + SUBLANES:rb + 2 * SUBLANES, :] = zero8

    for r in range(nrb):
        for idx, (src_ref, cw_ref) in enumerate(streams):
            acc = None
            for i in range(CONV_WIDTH):
                if r == 0 or r == nrb - 1:
                    off = SUBLANES - half + i
                    tap = pad_scr[0 if r == 0 else 1, idx, off:off + rb, :]
                else:
                    off = r * rb - half + i
                    tap = src_ref[off:off + rb, :]
                term = cw_ref[i:i + 1, :] * tap
                acc = term if acc is None else acc + term
            y = _silu(acc)
            if idx < 2:
                y = y * lax.rsqrt(jnp.sum(y * y, axis=-1, keepdims=True) + EPS)
            if idx == 0:
                y = y * (HEAD_DIM ** -0.5)
            qkv_scr[idx, r * rb:(r + 1) * rb, :] = y

    lane = lax.broadcasted_iota(jnp.int32, (CHUNK, LANES), 1)
    ri = lax.broadcasted_iota(jnp.int32, (CHUNK, LANES), 0)
    lo = lane < CHUNK
    ci = jnp.where(lo, lane, lane - CHUNK)
    ahead = jnp.where(lo, ri - ci, ci - ri)
    incl = ahead >= 0
    strict = ahead > 0
    eye = jnp.where(ri == ci, 1.0, 0.0)
    lane_row = lax.broadcasted_iota(jnp.int32, (1, LANES), 1)
    head_row = lax.broadcasted_iota(jnp.int32, (N_HEADS, LANES), 0)
    bufs = ((kw0, r0, qp0, op0, gl0), (kw1, r1, qp1, op1, gl1))

    def chunk_of(gi, j, d):
        return gi * gsz + j if d == 0 else nc - 1 - (gi * gsz + j)

    def column(tile, c):
        return jnp.sum(jnp.where(lane == c, tile, 0.0), axis=-1, keepdims=True)

    def pack(m):
        return jnp.where(lo, m[:CHUNK], m[CHUNK:])

    def blockdiag(m):
        zero = jnp.zeros_like(m)
        return jnp.concatenate([jnp.where(lo, m, zero), jnp.where(lo, zero, m)], axis=0)

    def prep(gi, buf):
        kw_s, r_s, qp_s, op_s, gl_s = buf
        ch = [(j, d) for j in range(gsz) for d in range(2)]
        rows = [pl.ds(chunk_of(gi, j, d) * CHUNK, CHUNK) for j, d in ch]
        q = [qkv_scr[0, r, :] for r in rows]
        k = [qkv_scr[1, r, :] for r in rows]
        v = [qkv_scr[2, r, :] for r in rows]
        gt = [gates_ref[r, :] for r in rows]
        beta = [column(a, d * N_HEADS + h) for a, (j, d) in zip(gt, ch)]
        g = [column(a, (2 + d) * N_HEADS + h) for a, (j, d) in zip(gt, ch)]
        gl = [a[CHUNK - 1:CHUNK, :] if d == 0 else a[0:1, :] for a, (j, d) in zip(g, ch)]
        kb = [a * bt for a, bt in zip(k, beta)]

        def g_row(j, d):
            c = chunk_of(gi, j, d)
            odd = c % 2
            win = gates_t_ref[(2 + d) * N_HEADS:(3 + d) * N_HEADS, pl.ds((c - odd) * CHUNK, 2 * CHUNK)]
            win = jnp.sum(jnp.where(head_row == h, win, 0.0), axis=0, keepdims=True)
            return pltpu.roll(win, CHUNK, axis=1) if odd != d else win

        pair = lambda xs, j: jnp.concatenate([xs[2 * j], xs[2 * j + 1]], axis=0)
        k2 = [pair(k, j).astype(BF16) for j in range(gsz)]
        kk = [pack(_dot_nt(pair(kb, j).astype(BF16), k2[j])) for j in range(gsz)]
        qk_raw = [pack(_dot_nt(pair(q, j).astype(BF16), k2[j])) for j in range(gsz)]
        diff = [jnp.where(lo, g[2 * j], g[2 * j + 1])
                - jnp.where(lane_row < CHUNK, g_row(j, 0), g_row(j, 1)) for j in range(gsz)]
        decay = [jnp.where(incl, jnp.exp(jnp.where(incl, a, 0.0)), 0.0) for a in diff]
        pw = [jnp.where(strict, -(a * dc), 0.0) for a, dc in zip(kk, decay)]
        x = [eye + a for a in pw]
        p16 = [a.astype(BF16) for a in pw]
        p16 = [_dot(a, blockdiag(a)).astype(BF16) for a in p16]
        for _ in range(4):
            res = [_dot(jnp.concatenate([a, b.astype(BF16)], axis=0), blockdiag(a)) for a, b in zip(p16, x)]
            p16 = [a[:CHUNK].astype(BF16) for a in res]
            x = [a + b[CHUNK:] for a, b in zip(x, res)]
        x = [a + _dot(a.astype(BF16), blockdiag(b)) for a, b in zip(x, p16)]
        qk2 = [a * dc for a, dc in zip(qk_raw, decay)]
        unpack = lambda xs: [(xs[j][:, :CHUNK] if d == 0 else xs[j][:, CHUNK:]).astype(BF16) for j, d in ch]
        xinv = unpack(x)
        qk = unpack(qk2)
        eg = [jnp.exp(a) for a in g]
        rhs = [jnp.concatenate([a * bt, b * e], axis=1).astype(BF16)
               for a, bt, b, e in zip(v, beta, kb, eg)]
        sol = [_dot(a, b).astype(BF16) for a, b in zip(xinv, rhs)]
        kd = [(a * jnp.exp(l - b)).T.astype(BF16) for a, b, l in zip(k, g, gl)]
        kdu_kdw = [_dot(a, b) for a, b in zip(kd, sol)]
        qku_qkw = [_dot(a, b) for a, b in zip(qk, sol)]
        for i, (j, d) in enumerate(ch):
            srows = slice(j * HEAD_DIM, (j + 1) * HEAD_DIM)
            crows = slice(j * CHUNK, (j + 1) * CHUNK)
            r_s[d, srows, :] = kdu_kdw[i][:, :HEAD_DIM]
            kw_s[d, srows, :] = kdu_kdw[i][:, HEAD_DIM:].astype(BF16)
            op_s[d, crows, :] = qku_qkw[i][:, :HEAD_DIM]
            qp_s[d, crows, :] = (q[i] * eg[i] - qku_qkw[i][:, HEAD_DIM:]).astype(BF16)
            gl_s[d, j * SUBLANES:(j + 1) * SUBLANES, :] = jnp.broadcast_to(jnp.exp(gl[i]), (SUBLANES, LANES))

    seen = (set(), set())

    def scan(gi, buf, carry):
        kw_s, r_s, qp_s, op_s, gl_s = buf
        st = list(carry)
        for j in range(gsz):
            srows = slice(j * HEAD_DIM, (j + 1) * HEAD_DIM)
            crows = slice(j * CHUNK, (j + 1) * CHUNK)
            for d in range(2):
                c = chunk_of(gi, j, d)
                rows = pl.ds(c * CHUNK, CHUNK)
                st16 = st[d].astype(BF16)
                o = _dot(qp_s[d, crows, :], st16) + op_s[d, crows, :]
                if c in seen[1 - d]:
                    o = o + o_scr[1 - d, rows, :]
                    y = o * lax.rsqrt(jnp.mean(o * o, axis=-1, keepdims=True) + EPS) * gh_ref[...]
                    out_ref[rows, :] = (y * _silu(z_ref[rows, :])).astype(BF16)
                else:
                    o_scr[d, rows, :] = o
                seen[d].add(c)
                gl = gl_s[d, j * SUBLANES:j * SUBLANES + 1, :]
                st[d] = gl * st[d] + (r_s[d, srows, :] - _dot(kw_s[d, srows, :], st16))
        return tuple(st)

    carry = (jnp.zeros((HEAD_DIM, HEAD_DIM), F32),) * 2
    prep(0, bufs[0])
    for gi in range(1, ng):
        prep(gi, bufs[gi % 2])
        carry = scan(gi - 1, bufs[(gi - 1) % 2], carry)
    scan(ng - 1, bufs[(ng - 1) % 2], carry)
    assert all(len(sn) == nc for sn in seen)


def _delta(proj3, gates3, gates_t, conv_w, g_head, cast_weights, *, q_blk, z_blk):
    b, t, _ = proj3.shape
    n_steps = b * N_HEADS
    slab_specs = [pl.BlockSpec((w.shape[0] // n_steps, w.shape[1]), lambda bi, hi: (bi * N_HEADS + hi, 0))
                  for w in cast_weights]
    rb = min(t // 2, 256)
    gsz = min(16, t // (2 * CHUNK))
    assert (t // CHUNK) % (2 * gsz) == 0

    def col(off):
        return pl.BlockSpec((None, t, HEAD_DIM), lambda bi, hi: (bi, 0, off + hi))

    def cw(off):
        return pl.BlockSpec((CONV_WIDTH, HEAD_DIM), lambda bi, hi: (0, off + hi))

    operands = [
        pltpu.VMEM((2, gsz * HEAD_DIM, HEAD_DIM), BF16),
        pltpu.VMEM((2, gsz * HEAD_DIM, HEAD_DIM), F32),
        pltpu.VMEM((2, gsz * CHUNK, HEAD_DIM), BF16),
        pltpu.VMEM((2, gsz * CHUNK, HEAD_DIM), F32),
        pltpu.VMEM((2, gsz * SUBLANES, LANES), F32),
    ]
    return pl.pallas_call(
        functools.partial(_delta_kernel, t=t, rb=rb, gsz=gsz, n_cast=len(cast_weights)),
        grid=(b, N_HEADS),
        in_specs=[
            col(q_blk), col(q_blk + N_HEADS), col(q_blk + 2 * N_HEADS), col(z_blk),
            pl.BlockSpec((None, t, LANES), lambda bi, hi: (bi, 0, 0)),
            pl.BlockSpec((LANES, t), lambda bi, hi: (0, bi)),
            cw(0), cw(N_HEADS), cw(2 * N_HEADS),
            pl.BlockSpec((1, HEAD_DIM), lambda bi, hi: (0, 0)),
        ] + slab_specs,
        out_specs=[pl.BlockSpec((None, t, HEAD_DIM), lambda bi, hi: (bi, 0, hi))] + slab_specs,
        out_shape=[jax.ShapeDtypeStruct((b, t, N_HEADS * HEAD_DIM), BF16)]
        + [jax.ShapeDtypeStruct(w.shape, BF16) for w in cast_weights],
        scratch_shapes=[
            pltpu.VMEM((2, 3, rb + 2 * SUBLANES, LANES), F32),
            pltpu.VMEM((3, t, HEAD_DIM), F32),
            pltpu.VMEM((2, t, HEAD_DIM), F32),
        ] + operands + operands,
        compiler_params=pltpu.CompilerParams(
            dimension_semantics=("parallel", "arbitrary"), vmem_limit_bytes=VMEM_LIMIT),
        name="delta",
    )(proj3, proj3, proj3, proj3, gates3, gates_t, conv_w, conv_w, conv_w, g_head, *cast_weights)


DFT_COLS = 256


def _dft_seq_kernel(g_ref, c_ref, s_ref, p_ref, o_ref, *, scale):
    half = s_ref.shape[0]
    row = lax.broadcasted_iota(jnp.int32, (half, DFT_COLS), 0)
    for c0 in range(0, o_ref.shape[1], DFT_COLS):
        cols = slice(c0, c0 + DFT_COLS)
        a = _dot(c_ref[...], g_ref[0, :, cols]) * scale
        b = _dot(s_ref[...], g_ref[1, :, cols]) * scale
        o_ref[0:half, cols] = (a[:half] + b).astype(o_ref.dtype)
        mirrored = _dot(p_ref[...], (a[:half] - b).astype(BF16))
        o_ref[half:2 * half, cols] = jnp.where(row == 0, a[half:half + 1], mirrored).astype(o_ref.dtype)


def _dft_seq(g2, c_half, s_half, perm, *, scale, tn):
    _, t, n = g2.shape

    def const(shape):
        return pl.BlockSpec(shape, lambda j: (0, 0), pipeline_mode=pl.Buffered(1))

    return pl.pallas_call(
        functools.partial(_dft_seq_kernel, scale=scale),
        grid=(n // tn,),
        in_specs=[
            pl.BlockSpec((2, t, tn), lambda j: (0, 0, j)),
            const(c_half.shape), const(s_half.shape), const(perm.shape),
        ],
        out_specs=pl.BlockSpec((t, tn), lambda j: (0, j)),
        out_shape=jax.ShapeDtypeStruct((t, n), BF16),
        compiler_params=pltpu.CompilerParams(dimension_semantics=("parallel",), vmem_limit_bytes=VMEM_LIMIT),
        name="dft_seq",
    )(g2, c_half, s_half, perm)


@functools.lru_cache(maxsize=None)
def _dft_tables(t, d):
    half = t // 2
    k = np.arange(half + SUBLANES, dtype=np.int64)
    n = np.arange(t, dtype=np.int64)
    ang = 2.0 * np.pi * ((k[:, None] * n[None, :]) % t) / t
    c_half = np.cos(ang).astype(np.float32)
    s_half = np.sin(ang[:half]).astype(np.float32)
    perm = np.zeros((half, half), np.float32)
    perm[np.arange(1, half), half - np.arange(1, half)] = 1.0
    c = np.arange(d, dtype=np.int64)
    angd = 2.0 * np.pi * ((c[:, None] * c[None, :]) % d) / d
    wd = np.concatenate([np.cos(angd), -np.sin(angd)], axis=1).astype(np.float32)
    return c_half, s_half, perm, wd


def _merge_kernel(og_ref, fr_ref, ga_ref, gf_ref, x_ref, wdn_ref, wf_ref, wo_ref, g_ref,
                  x1_ref, h2_ref):
    ya = _dot(og_ref[...], wdn_ref[...])
    yf = _dot(fr_ref[...], wf_ref[...])
    merged = _sigmoid(ga_ref[...].astype(F32)) * ya + _sigmoid(gf_ref[...].astype(F32)) * yf
    x1 = x_ref[...] + _dot(merged.astype(BF16), wo_ref[...])
    x1_ref[...] = x1
    ms = jnp.mean(x1 * x1, axis=-1, keepdims=True)
    h2_ref[...] = (x1 * lax.rsqrt(ms + EPS) * g_ref[...]).astype(BF16)


def _merge(og2, fr, gates2, x2d, w_dn, w_f, w_o, g_mlp, *, t, tm):
    m, d = x2d.shape
    bw = og2.shape[1]
    tpb = t // tm

    def const(shape):
        return pl.BlockSpec(shape, lambda i: (0, 0), pipeline_mode=pl.Buffered(1))

    return pl.pallas_call(
        _merge_kernel,
        grid=(m // tm,),
        in_specs=[
            pl.BlockSpec((tm, bw), lambda i: (i, 0)),
            pl.BlockSpec((tm, bw), lambda i: (i % tpb, i // tpb)),
            pl.BlockSpec((tm, d), lambda i: (i, 0)),
            pl.BlockSpec((tm, d), lambda i: (i, 1)),
            pl.BlockSpec((tm, d), lambda i: (i, 0)),
            const((bw, d)), const((bw, d)), const((d, d)), const((1, d)),
        ],
        out_specs=[pl.BlockSpec((tm, d), lambda i: (i, 0)), pl.BlockSpec((tm, d), lambda i: (i, 0))],
        out_shape=[jax.ShapeDtypeStruct((m, d), F32), jax.ShapeDtypeStruct((m, d), BF16)],
        compiler_params=pltpu.CompilerParams(
            dimension_semantics=("parallel",), vmem_limit_bytes=VMEM_LIMIT),
        name="merge",
    )(og2, fr, gates2, gates2, x2d, w_dn, w_f, w_o, g_mlp)


def _mlp_kernel(h_ref, wu_ref, wd_ref, x1_ref, g_ref, o_ref):
    j = pl.program_id(1)

    @pl.when(j == 0)
    def _():
        o_ref[...] = x1_ref[...]

    a = jnp.maximum(_dot(h_ref[...], wu_ref[...]), 0.0)
    o_ref[...] += _dot((a * a).astype(BF16), wd_ref[...])

    @pl.when(j == pl.num_programs(1) - 1)
    def _():
        y = o_ref[...]
        ms = jnp.mean(y * y, axis=-1, keepdims=True)
        o_ref[...] = y * lax.rsqrt(ms + EPS) * g_ref[...]


def _mlp(h2, w_up, w_down, x1, g_final, *, tm, tf):
    m, d = x1.shape
    ff = w_up.shape[1]
    return pl.pallas_call(
        _mlp_kernel,
        grid=(m // tm, ff // tf),
        in_specs=[
            pl.BlockSpec((tm, d), lambda i, j: (i, 0)),
            pl.BlockSpec((d, tf), lambda i, j: (0, j)),
            pl.BlockSpec((tf, d), lambda i, j: (j, 0)),
            pl.BlockSpec((tm, d), lambda i, j: (i, 0)),
            pl.BlockSpec((1, d), lambda i, j: (0, 0)),
        ],
        out_specs=pl.BlockSpec((tm, d), lambda i, j: (i, 0)),
        out_shape=jax.ShapeDtypeStruct((m, d), F32),
        compiler_params=pltpu.CompilerParams(
            dimension_semantics=("parallel", "arbitrary"), vmem_limit_bytes=VMEM_LIMIT),
        name="mlp",
    )(h2, w_up, w_down, x1, g_final)


def _largest_divisor(n, cap, multiple):
    return max(c for c in range(multiple, cap + 1, multiple) if n % c == 0)


def _layer(x, g_mix, w_in, conv_w, a_log_f, a_log_b, dt_f, dt_b, g_head, w_dn_up, w_fourier, w_o,
           g_mlp, w_mlp_up, w_mlp_down, g_out):
    b, t, d = x.shape
    m = b * t
    dn = N_HEADS * HEAD_DIM
    fw = N_GROUPS * GROUP_DIM
    o_q, o_k, o_v, o_z, o_f, o_s, o_g = np.cumsum([0, dn, dn, dn, dn, fw, 4 * N_HEADS])
    w_t32 = jnp.swapaxes(w_in, 0, 1)
    tn = 1024
    row_starts = tuple(range(int(o_g), w_t32.shape[0], tn)) + tuple(range(int(o_q), int(o_s), tn))
    q_blk, z_blk = 0, 3 * N_HEADS

    def gate_row(fwd, bwd):
        return jnp.pad(jnp.concatenate([fwd, bwd]).astype(F32),
                       (2 * N_HEADS, LANES - 4 * N_HEADS)).reshape(1, LANES)

    x2d = x.reshape(m, d)
    w_t, h2d, gates, gates_t = _prologue(w_t32, x2d, g_mix.reshape(1, d), gate_row(a_log_f, a_log_b),
                                         gate_row(dt_f, dt_b), tr=_largest_divisor(w_t32.shape[0], 1024, 32),
                                         xr=min(m, 512), scal_start=int(o_s), n_scal=4 * N_HEADS)
    c_half, s_half, perm, wd = _dft_tables(t, GROUP_DIM)
    assert tn == fw and row_starts[-1] == o_f, "the channel DFT rides on the last (Fourier) column step"
    branch_gates, proj, g2 = _in_proj(h2d, w_t, jnp.asarray(wd).astype(BF16), tm=min(m, 1024), tn=tn,
                                      row_starts=row_starts, n_gate_steps=(w_t32.shape[0] - int(o_g)) // tn, t=t)
    proj3 = proj.reshape(b, t, proj.shape[1])

    og, w_dn16, w_f16, w_o16, w_up16, w_down16 = _delta(
        proj3, gates.reshape(b, t, LANES), gates_t, conv_w, g_head.reshape(1, HEAD_DIM),
        [w_dn_up, w_fourier, w_o, w_mlp_up, w_mlp_down], q_blk=q_blk, z_blk=z_blk)

    fr = _dft_seq(g2, jnp.asarray(c_half).astype(BF16), jnp.asarray(s_half).astype(BF16),
                  jnp.asarray(perm).astype(BF16), scale=float((t * GROUP_DIM) ** -0.5), tn=fw)

    x1, h2 = _merge(og.reshape(m, dn), fr, branch_gates, x2d, w_dn16, w_f16, w_o16, g_mlp.reshape(1, d),
                    t=t, tm=min(t, 256))
    return _mlp(h2, w_up16, w_down16, x1, g_out.reshape(1, d), tm=min(m, 512), tf=1024)


def kernel(x, g_mix, w_in, conv_w, a_log_fwd, a_log_bwd, dt_bias_fwd, dt_bias_bwd, g_dn_head, w_dn_up,
           w_fourier, w_o, g_mlp, w_mlp_up, w_mlp_down, g_final):
    depth = g_mix.shape[0]
    assert depth == 1, "the final rmsnorm is fused into the last block's MLP kernel"
    b, t, d = x.shape
    out = _layer(x, g_mix[0], w_in[0], conv_w[0], a_log_fwd[0], a_log_bwd[0], dt_bias_fwd[0],
                 dt_bias_bwd[0], g_dn_head[0], w_dn_up[0], w_fourier[0], w_o[0], g_mlp[0],
                 w_mlp_up[0], w_mlp_down[0], g_final)
    return out.reshape(b, t, d)
```

```python
import functools

import jax
import jax.numpy as jnp
import numpy as np
from jax import lax
from jax.experimental import pallas as pl
from jax.experimental.pallas import tpu as pltpu

F32 = jnp.float32
BF16 = jnp.bfloat16

EPS = 1e-6
N_HEADS = 8
HEAD_DIM = 128
CHUNK = 64
CONV_WIDTH = 5
N_GROUPS = 8
GROUP_DIM = 128
LANES = 128
SUBLANES = 8
VMEM_LIMIT = 56 * 1024 * 1024
SLAB_ALIGN = 32
NORM_ROWS = 128

_NT = (((1,), (1,)), ((), ()))


def _dot(a, b):
    return jnp.dot(a, b, preferred_element_type=F32)


def _dot_nt(a, b):
    return lax.dot_general(a, b, _NT, preferred_element_type=F32)


def _split3(x):
    hi = x.astype(BF16)
    r = x - hi.astype(F32)
    mid = r.astype(BF16)
    r = r - mid.astype(F32)
    return hi, mid, r.astype(BF16)


def _sigmoid(x):
    return 1.0 / (1.0 + jnp.exp(-x))


def _silu(x):
    return x * _sigmoid(x)


def _softplus(x):
    return jnp.maximum(x, 0.0) + jnp.log1p(jnp.exp(-jnp.abs(x)))


GATE_ROWS = 256


def _prologue_kernel(w_ref, x_ref, g_ref, ws_ref, alog_ref, dt_ref, w16_ref, h_ref, gates_ref, gates_t_ref,
                     ws_scr, *, xr, rb, n_x_steps):
    w16_ref[...] = w_ref[...].astype(BF16)

    @pl.when(pl.program_id(0) == 0)
    def _():
        ws_scr[...] = jnp.zeros(ws_scr.shape, BF16)
        ws_scr[0:ws_ref.shape[0], :] = ws_ref[...].astype(BF16)

    @pl.when(pl.program_id(0) < n_x_steps)
    def _():
        def body(r, c):
            rows = pl.ds(pl.multiple_of(r * rb, rb), rb)
            xf = x_ref[rows, :]
            ms = jnp.mean(xf * xf, axis=-1, keepdims=True)
            h_ref[rows, :] = (xf * lax.rsqrt(ms + EPS) * g_ref[...]).astype(BF16)
            return c
        lax.fori_loop(0, xr // rb, body, 0)

        gr = GATE_ROWS
        ri = lax.broadcasted_iota(jnp.int32, (gr, gr), 0)
        ci = lax.broadcasted_iota(jnp.int32, (gr, gr), 1)
        same = (ri // CHUNK) == (ci // CHUNK)
        ltri = jnp.where(same & (ri >= ci), 1.0, 0.0).astype(BF16)
        utri = jnp.where(same & (ri <= ci), 1.0, 0.0).astype(BF16)
        lane = lax.broadcasted_iota(jnp.int32, (gr, LANES), 1)
        neg_a = -jnp.exp(alog_ref[...])
        for sub in range(xr // gr):
            rows = slice(sub * gr, (sub + 1) * gr)
            s = _dot_nt(h_ref[rows, :], ws_scr[...])
            g3 = _split3(neg_a * _softplus(s + dt_ref[...]))
            gc_f = _dot(ltri, g3[0]) + _dot(ltri, g3[1]) + _dot(ltri, g3[2])
            gc_b = _dot(utri, g3[0]) + _dot(utri, g3[1]) + _dot(utri, g3[2])
            tile = jnp.where(lane < 2 * N_HEADS, _sigmoid(s), jnp.where(lane < 3 * N_HEADS, gc_f, gc_b))
            gates_ref[rows, :] = tile
            gates_t_ref[:, rows] = tile.T


def _prologue(w_t, x2d, g, alog_row, dt_row, *, tr, xr, scal_start, n_scal):
    rows, d = w_t.shape
    assert scal_start % n_scal == 0 and n_scal % SUBLANES == 0 and n_scal <= LANES
    m = x2d.shape[0]
    n_steps, n_x = rows // tr, m // xr
    assert n_x <= n_steps and xr % GATE_ROWS == 0
    xi = lambda i: jnp.minimum(i, n_x - 1)
    return pl.pallas_call(
        functools.partial(_prologue_kernel, xr=xr, rb=NORM_ROWS, n_x_steps=n_x),
        grid=(n_steps,),
        in_specs=[
            pl.BlockSpec((tr, d), lambda i: (i, 0)),
            pl.BlockSpec((xr, d), lambda i: (xi(i), 0)),
            pl.BlockSpec((1, d), lambda i: (0, 0)),
            pl.BlockSpec((n_scal, d), lambda i: (scal_start // n_scal, 0)),
            pl.BlockSpec((1, LANES), lambda i: (0, 0)),
            pl.BlockSpec((1, LANES), lambda i: (0, 0)),
        ],
        out_specs=[
            pl.BlockSpec((tr, d), lambda i: (i, 0)),
            pl.BlockSpec((xr, d), lambda i: (xi(i), 0)),
            pl.BlockSpec((xr, LANES), lambda i: (xi(i), 0)),
            pl.BlockSpec((LANES, xr), lambda i: (0, xi(i))),
        ],
        out_shape=[jax.ShapeDtypeStruct((rows, d), BF16), jax.ShapeDtypeStruct((m, d), BF16),
                   jax.ShapeDtypeStruct((m, LANES), F32), jax.ShapeDtypeStruct((LANES, m), F32)],
        scratch_shapes=[pltpu.VMEM((LANES, d), BF16)],
        compiler_params=pltpu.CompilerParams(dimension_semantics=("arbitrary",), vmem_limit_bytes=VMEM_LIMIT),
        name="prologue",
    )(w_t, x2d, g, w_t, alog_row, dt_row)


def _in_proj_kernel(st_ref, h_ref, w_ref, wd_ref, o_ref, g2_ref):
    o_ref[...] = _dot_nt(h_ref[...], w_ref[...])

    @pl.when(pl.program_id(1) == pl.num_programs(1) - 1)
    def _():
        for g in range(N_GROUPS):
            cols = slice(g * GROUP_DIM, (g + 1) * GROUP_DIM)
            y = _dot(o_ref[:, cols].astype(BF16), wd_ref[...])
            g2_ref[0, :, cols] = y[:, :GROUP_DIM].astype(BF16)
            g2_ref[1, :, cols] = y[:, GROUP_DIM:].astype(BF16)


def _in_proj(h2d, w_t, wd, *, tm, tn, row_starts, t):
    m, d = h2d.shape
    starts = jnp.asarray(row_starts, jnp.int32)
    n_steps = len(row_starts)
    return pl.pallas_call(
        _in_proj_kernel,
        grid_spec=pltpu.PrefetchScalarGridSpec(
            num_scalar_prefetch=1,
            grid=(m // tm, n_steps),
            in_specs=[
                pl.BlockSpec((tm, d), lambda i, j, st: (i, 0)),
                pl.BlockSpec((pl.Element(tn), pl.Element(d)), lambda i, j, st: (pl.multiple_of(st[j], SLAB_ALIGN), 0)),
                pl.BlockSpec((GROUP_DIM, 2 * GROUP_DIM), lambda i, j, st: (0, 0)),
            ],
            out_specs=[
                pl.BlockSpec((tm, tn), lambda i, j, st: (i, j)),
                pl.BlockSpec((2, tm, tn), lambda i, j, st: (0, i % (t // tm), i // (t // tm))),
            ],
        ),
        out_shape=[jax.ShapeDtypeStruct((m, n_steps * tn), F32),
                   jax.ShapeDtypeStruct((2, t, (m // t) * tn), BF16)],
        compiler_params=pltpu.CompilerParams(
            dimension_semantics=("parallel", "arbitrary"), vmem_limit_bytes=VMEM_LIMIT),
        name="in_proj",
    )(starts, h2d, w_t, wd)


def _delta_kernel(*refs, t, rb, gsz, n_cast):
    (q_ref, k_ref, v_ref, z_ref, gates_ref, gates_t_ref, cwq_ref, cwk_ref, cwv_ref, gh_ref), refs = (
        refs[:10], refs[10:])
    cast_in, (out_ref,), cast_out, refs = (
        refs[:n_cast], refs[n_cast:n_cast + 1], refs[n_cast + 1:2 * n_cast + 1], refs[2 * n_cast + 1:])
    pad_scr, qkv_scr, o_scr, kw0, r0, qp0, op0, gl0, kw1, r1, qp1, op1, gl1 = refs

    for src, dst in zip(cast_in, cast_out):
        dst[...] = src[...].astype(dst.dtype)

    nc = t // CHUNK
    ng = nc // gsz
    nrb = t // rb
    h = pl.program_id(1)

    half = (CONV_WIDTH - 1) // 2
    streams = ((q_ref, cwq_ref), (k_ref, cwk_ref), (v_ref, cwv_ref))
    zero8 = jnp.zeros((SUBLANES, LANES), F32)
    for idx, (src_ref, _) in enumerate(streams):
        pad_scr[0, idx, 0:SUBLANES, :] = zero8
        pad_scr[0, idx, SUBLANES:rb + 2 * SUBLANES, :] = src_ref[0:rb + SUBLANES, :]
        pad_scr[1, idx, 0:rb + SUBLANES, :] = src_ref[t - rb - SUBLANES:t, :]
        pad_scr[1, idx, rb + SUBLANES:rb + 2 * SUBLANES, :] = zero8

    for r in range(nrb):
        for idx, (src_ref, cw_ref) in enumerate(streams):
            acc = None
            for i in range(CONV_WIDTH):
                if r == 0 or r == nrb - 1:
                    off = SUBLANES - half + i
                    tap = pad_scr[0 if r == 0 else 1, idx, off:off + rb, :]
                else:
                    off = r * rb - half + i
                    tap = src_ref[off:off + rb, :]
                term = cw_ref[i:i + 1, :] * tap
                acc = term if acc is None else acc + term
            y = _silu(acc)
            if idx < 2:
                y = y * lax.rsqrt(jnp.sum(y * y, axis=-1, keepdims=True) + EPS)
            if idx == 0:
                y = y * (HEAD_DIM ** -0.5)
            qkv_scr[idx, r * rb:(r + 1) * rb, :] = y

    lane = lax.broadcasted_iota(jnp.int32, (CHUNK, LANES), 1)
    ri = lax.broadcasted_iota(jnp.int32, (CHUNK, LANES), 0)
    lo = lane < CHUNK
    ci = jnp.where(lo, lane, lane - CHUNK)
    ahead = jnp.where(lo, ri - ci, ci - ri)
    incl = ahead >= 0
    strict = ahead > 0
    eye = jnp.where(ri == ci, 1.0, 0.0)
    lane_row = lax.broadcasted_iota(jnp.int32, (1, LANES), 1)
    head_row = lax.broadcasted_iota(jnp.int32, (N_HEADS, LANES), 0)
    bufs = ((kw0, r0, qp0, op0, gl0), (kw1, r1, qp1, op1, gl1))

    def chunk_of(gi, j, d):
        return gi * gsz + j if d == 0 else nc - 1 - (gi * gsz + j)

    def column(tile, c):
        return jnp.sum(jnp.where(lane == c, tile, 0.0), axis=-1, keepdims=True)

    def pack(m):
        return jnp.where(lo, m[:CHUNK], m[CHUNK:])

    def blockdiag(m):
        zero = jnp.zeros_like(m)
        return jnp.concatenate([jnp.where(lo, m, zero), jnp.where(lo, zero, m)], axis=0)

    def prep(gi, buf):
        kw_s, r_s, qp_s, op_s, gl_s = buf
        ch = [(j, d) for j in range(gsz) for d in range(2)]
        rows = [pl.ds(chunk_of(gi, j, d) * CHUNK, CHUNK) for j, d in ch]
        q = [qkv_scr[0, r, :] for r in rows]
        k = [qkv_scr[1, r, :] for r in rows]
        v = [qkv_scr[2, r, :] for r in rows]
        gt = [gates_ref[r, :] for r in rows]
        beta = [column(a, d * N_HEADS + h) for a, (j, d) in zip(gt, ch)]
        g = [column(a, (2 + d) * N_HEADS + h) for a, (j, d) in zip(gt, ch)]
        gl = [a[CHUNK - 1:CHUNK, :] if d == 0 else a[0:1, :] for a, (j, d) in zip(g, ch)]
        kb = [a * bt for a, bt in zip(k, beta)]

        def g_row(j, d):
            c = chunk_of(gi, j, d)
            odd = c % 2
            win = gates_t_ref[(2 + d) * N_HEADS:(3 + d) * N_HEADS, pl.ds((c - odd) * CHUNK, 2 * CHUNK)]
            win = jnp.sum(jnp.where(head_row == h, win, 0.0), axis=0, keepdims=True)
            return pltpu.roll(win, CHUNK, axis=1) if odd != d else win

        pair = lambda xs, j: jnp.concatenate([xs[2 * j], xs[2 * j + 1]], axis=0)
        k2 = [pair(k, j).astype(BF16) for j in range(gsz)]
        kk = [pack(_dot_nt(pair(kb, j).astype(BF16), k2[j])) for j in range(gsz)]
        qk_raw = [pack(_dot_nt(pair(q, j).astype(BF16), k2[j])) for j in range(gsz)]
        diff = [jnp.where(lo, g[2 * j], g[2 * j + 1])
                - jnp.where(lane_row < CHUNK, g_row(j, 0), g_row(j, 1)) for j in range(gsz)]
        decay = [jnp.where(incl, jnp.exp(jnp.where(incl, a, 0.0)), 0.0) for a in diff]
        pw = [jnp.where(strict, -(a * dc), 0.0) for a, dc in zip(kk, decay)]
        x = [eye + a for a in pw]
        p16 = [a.astype(BF16) for a in pw]
        p16 = [_dot(a, blockdiag(a)).astype(BF16) for a in p16]
        for _ in range(4):
            res = [_dot(jnp.concatenate([a, b.astype(BF16)], axis=0), blockdiag(a)) for a, b in zip(p16, x)]
            p16 = [a[:CHUNK].astype(BF16) for a in res]
            x = [a + b[CHUNK:] for a, b in zip(x, res)]
        x = [a + _dot(a.astype(BF16), blockdiag(b)) for a, b in zip(x, p16)]
        qk2 = [a * dc for a, dc in zip(qk_raw, decay)]
        unpack = lambda xs: [(xs[j][:, :CHUNK] if d == 0 else xs[j][:, CHUNK:]).astype(BF16) for j, d in ch]
        xinv = unpack(x)
        qk = unpack(qk2)
        eg = [jnp.exp(a) for a in g]
        rhs = [jnp.concatenate([a * bt, b * e], axis=1).astype(BF16)
               for a, bt, b, e in zip(v, beta, kb, eg)]
        sol = [_dot(a, b).astype(BF16) for a, b in zip(xinv, rhs)]
        kd = [(a * jnp.exp(l - b)).T.astype(BF16) for a, b, l in zip(k, g, gl)]
        kdu_kdw = [_dot(a, b) for a, b in zip(kd, sol)]
        qku_qkw = [_dot(a, b) for a, b in zip(qk, sol)]
        for i, (j, d) in enumerate(ch):
            srows = slice(j * HEAD_DIM, (j + 1) * HEAD_DIM)
            crows = slice(j * CHUNK, (j + 1) * CHUNK)
            r_s[d, srows, :] = kdu_kdw[i][:, :HEAD_DIM]
            kw_s[d, srows, :] = kdu_kdw[i][:, HEAD_DIM:].astype(BF16)
            op_s[d, crows, :] = qku_qkw[i][:, :HEAD_DIM]
            qp_s[d, crows, :] = (q[i] * eg[i] - qku_qkw[i][:, HEAD_DIM:]).astype(BF16)
            gl_s[d, j * SUBLANES:(j + 1) * SUBLANES, :] = jnp.broadcast_to(jnp.exp(gl[i]), (SUBLANES, LANES))

    seen = (set(), set())

    def scan(gi, buf, carry):
        kw_s, r_s, qp_s, op_s, gl_s = buf
        st = list(carry)
        for j in range(gsz):
            srows = slice(j * HEAD_DIM, (j + 1) * HEAD_DIM)
            crows = slice(j * CHUNK, (j + 1) * CHUNK)
            for d in range(2):
                c = chunk_of(gi, j, d)
                rows = pl.ds(c * CHUNK, CHUNK)
                st16 = st[d].astype(BF16)
                o = _dot(qp_s[d, crows, :], st16) + op_s[d, crows, :]
                if c in seen[1 - d]:
                    o = o + o_scr[1 - d, rows, :]
                    y = o * lax.rsqrt(jnp.mean(o * o, axis=-1, keepdims=True) + EPS) * gh_ref[...]
                    out_ref[rows, :] = (y * _silu(z_ref[rows, :])).astype(BF16)
                else:
                    o_scr[d, rows, :] = o
                seen[d].add(c)
                gl = gl_s[d, j * SUBLANES:j * SUBLANES + 1, :]
                st[d] = gl * st[d] + (r_s[d, srows, :] - _dot(kw_s[d, srows, :], st16))
        return tuple(st)

    carry = (jnp.zeros((HEAD_DIM, HEAD_DIM), F32),) * 2
    prep(0, bufs[0])
    for gi in range(1, ng):
        prep(gi, bufs[gi % 2])
        carry = scan(gi - 1, bufs[(gi - 1) % 2], carry)
    scan(ng - 1, bufs[(ng - 1) % 2], carry)
    assert all(len(sn) == nc for sn in seen)


def _delta(proj3, gates3, gates_t, conv_w, g_head, cast_weights, *, q_blk, z_blk):
    b, t, _ = proj3.shape
    n_steps = b * N_HEADS
    slab_specs = [pl.BlockSpec((w.shape[0] // n_steps, w.shape[1]), lambda bi, hi: (bi * N_HEADS + hi, 0))
                  for w in cast_weights]
    rb = min(t // 2, 256)
    gsz = min(16, t // (2 * CHUNK))
    assert (t // CHUNK) % (2 * gsz) == 0

    def col(off):
        return pl.BlockSpec((None, t, HEAD_DIM), lambda bi, hi: (bi, 0, off + hi))

    def cw(off):
        return pl.BlockSpec((CONV_WIDTH, HEAD_DIM), lambda bi, hi: (0, off + hi))

    operands = [
        pltpu.VMEM((2, gsz * HEAD_DIM, HEAD_DIM), BF16),
        pltpu.VMEM((2, gsz * HEAD_DIM, HEAD_DIM), F32),
        pltpu.VMEM((2, gsz * CHUNK, HEAD_DIM), BF16),
        pltpu.VMEM((2, gsz * CHUNK, HEAD_DIM), F32),
        pltpu.VMEM((2, gsz * SUBLANES, LANES), F32),
    ]
    return pl.pallas_call(
        functools.partial(_delta_kernel, t=t, rb=rb, gsz=gsz, n_cast=len(cast_weights)),
        grid=(b, N_HEADS),
        in_specs=[
            col(q_blk), col(q_blk + N_HEADS), col(q_blk + 2 * N_HEADS), col(z_blk),
            pl.BlockSpec((None, t, LANES), lambda bi, hi: (bi, 0, 0)),
            pl.BlockSpec((LANES, t), lambda bi, hi: (0, bi)),
            cw(0), cw(N_HEADS), cw(2 * N_HEADS),
            pl.BlockSpec((1, HEAD_DIM), lambda bi, hi: (0, 0)),
        ] + slab_specs,
        out_specs=[pl.BlockSpec((None, t, HEAD_DIM), lambda bi, hi: (bi, 0, hi))] + slab_specs,
        out_shape=[jax.ShapeDtypeStruct((b, t, N_HEADS * HEAD_DIM), BF16)]
        + [jax.ShapeDtypeStruct(w.shape, BF16) for w in cast_weights],
        scratch_shapes=[
            pltpu.VMEM((2, 3, rb + 2 * SUBLANES, LANES), F32),
            pltpu.VMEM((3, t, HEAD_DIM), F32),
            pltpu.VMEM((2, t, HEAD_DIM), F32),
        ] + operands + operands,
        compiler_params=pltpu.CompilerParams(
            dimension_semantics=("parallel", "arbitrary"), vmem_limit_bytes=VMEM_LIMIT),
        name="delta",
    )(proj3, proj3, proj3, proj3, gates3, gates_t, conv_w, conv_w, conv_w, g_head, *cast_weights)


DFT_COLS = 256


def _dft_seq_kernel(g_ref, c_ref, s_ref, p_ref, o_ref, *, scale):
    half = s_ref.shape[0]
    row = lax.broadcasted_iota(jnp.int32, (half, DFT_COLS), 0)
    for c0 in range(0, o_ref.shape[1], DFT_COLS):
        cols = slice(c0, c0 + DFT_COLS)
        a = _dot(c_ref[...], g_ref[0, :, cols]) * scale
        b = _dot(s_ref[...], g_ref[1, :, cols]) * scale
        o_ref[0:half, cols] = (a[:half] + b).astype(o_ref.dtype)
        mirrored = _dot(p_ref[...], (a[:half] - b).astype(BF16))
        o_ref[half:2 * half, cols] = jnp.where(row == 0, a[half:half + 1], mirrored).astype(o_ref.dtype)


def _dft_seq(g2, c_half, s_half, perm, *, scale, tn):
    _, t, n = g2.shape

    def const(shape):
        return pl.BlockSpec(shape, lambda j: (0, 0), pipeline_mode=pl.Buffered(1))

    return pl.pallas_call(
        functools.partial(_dft_seq_kernel, scale=scale),
        grid=(n // tn,),
        in_specs=[
            pl.BlockSpec((2, t, tn), lambda j: (0, 0, j)),
            const(c_half.shape), const(s_half.shape), const(perm.shape),
        ],
        out_specs=pl.BlockSpec((t, tn), lambda j: (0, j)),
        out_shape=jax.ShapeDtypeStruct((t, n), BF16),
        compiler_params=pltpu.CompilerParams(dimension_semantics=("parallel",), vmem_limit_bytes=VMEM_LIMIT),
        name="dft_seq",
    )(g2, c_half, s_half, perm)


@functools.lru_cache(maxsize=None)
def _dft_tables(t, d):
    half = t // 2
    k = np.arange(half + SUBLANES, dtype=np.int64)
    n = np.arange(t, dtype=np.int64)
    ang = 2.0 * np.pi * ((k[:, None] * n[None, :]) % t) / t
    c_half = np.cos(ang).astype(np.float32)
    s_half = np.sin(ang[:half]).astype(np.float32)
    perm = np.zeros((half, half), np.float32)
    perm[np.arange(1, half), half - np.arange(1, half)] = 1.0
    c = np.arange(d, dtype=np.int64)
    angd = 2.0 * np.pi * ((c[:, None] * c[None, :]) % d) / d
    wd = np.concatenate([np.cos(angd), -np.sin(angd)], axis=1).astype(np.float32)
    return c_half, s_half, perm, wd


def _merge_kernel(og_ref, fr_ref, ga_ref, gf_ref, x_ref, wdn_ref, wf_ref, wo_ref, g_ref,
                  x1_ref, h2_ref):
    ya = _dot(og_ref[...], wdn_ref[...])
    yf = _dot(fr_ref[...], wf_ref[...])
    merged = _sigmoid(ga_ref[...]) * ya + _sigmoid(gf_ref[...]) * yf
    x1 = x_ref[...] + _dot(merged.astype(BF16), wo_ref[...])
    x1_ref[...] = x1
    ms = jnp.mean(x1 * x1, axis=-1, keepdims=True)
    h2_ref[...] = (x1 * lax.rsqrt(ms + EPS) * g_ref[...]).astype(BF16)


def _merge(og2, fr, proj2, x2d, w_dn, w_f, w_o, g_mlp, *, t, tm):
    m, d = x2d.shape
    bw = og2.shape[1]
    tpb = t // tm

    def const(shape):
        return pl.BlockSpec(shape, lambda i: (0, 0), pipeline_mode=pl.Buffered(1))

    return pl.pallas_call(
        _merge_kernel,
        grid=(m // tm,),
        in_specs=[
            pl.BlockSpec((tm, bw), lambda i: (i, 0)),
            pl.BlockSpec((tm, bw), lambda i: (i % tpb, i // tpb)),
            pl.BlockSpec((tm, d), lambda i: (i, 0)),
            pl.BlockSpec((tm, d), lambda i: (i, 1)),
            pl.BlockSpec((tm, d), lambda i: (i, 0)),
            const((bw, d)), const((bw, d)), const((d, d)), const((1, d)),
        ],
        out_specs=[pl.BlockSpec((tm, d), lambda i: (i, 0)), pl.BlockSpec((tm, d), lambda i: (i, 0))],
        out_shape=[jax.ShapeDtypeStruct((m, d), F32), jax.ShapeDtypeStruct((m, d), BF16)],
        compiler_params=pltpu.CompilerParams(
            dimension_semantics=("parallel",), vmem_limit_bytes=VMEM_LIMIT),
        name="merge",
    )(og2, fr, proj2, proj2, x2d, w_dn, w_f, w_o, g_mlp)


def _mlp_kernel(h_ref, wu_ref, wd_ref, x1_ref, g_ref, o_ref):
    j = pl.program_id(1)

    @pl.when(j == 0)
    def _():
        o_ref[...] = x1_ref[...]

    a = jnp.maximum(_dot(h_ref[...], wu_ref[...]), 0.0)
    o_ref[...] += _dot((a * a).astype(BF16), wd_ref[...])

    @pl.when(j == pl.num_programs(1) - 1)
    def _():
        y = o_ref[...]
        ms = jnp.mean(y * y, axis=-1, keepdims=True)
        o_ref[...] = y * lax.rsqrt(ms + EPS) * g_ref[...]


def _mlp(h2, w_up, w_down, x1, g_final, *, tm, tf):
    m, d = x1.shape
    ff = w_up.shape[1]
    return pl.pallas_call(
        _mlp_kernel,
        grid=(m // tm, ff // tf),
        in_specs=[
            pl.BlockSpec((tm, d), lambda i, j: (i, 0)),
            pl.BlockSpec((d, tf), lambda i, j: (0, j)),
            pl.BlockSpec((tf, d), lambda i, j: (j, 0)),
            pl.BlockSpec((tm, d), lambda i, j: (i, 0)),
            pl.BlockSpec((1, d), lambda i, j: (0, 0)),
        ],
        out_specs=pl.BlockSpec((tm, d), lambda i, j: (i, 0)),
        out_shape=jax.ShapeDtypeStruct((m, d), F32),
        compiler_params=pltpu.CompilerParams(
            dimension_semantics=("parallel", "arbitrary"), vmem_limit_bytes=VMEM_LIMIT),
        name="mlp",
    )(h2, w_up, w_down, x1, g_final)


def _largest_divisor(n, cap, multiple):
    return max(c for c in range(multiple, cap + 1, multiple) if n % c == 0)


def _layer(x, g_mix, w_in, conv_w, a_log_f, a_log_b, dt_f, dt_b, g_head, w_dn_up, w_fourier, w_o,
           g_mlp, w_mlp_up, w_mlp_down, g_out):
    b, t, d = x.shape
    m = b * t
    dn = N_HEADS * HEAD_DIM
    fw = N_GROUPS * GROUP_DIM
    o_q, o_k, o_v, o_z, o_f, o_s, o_g = np.cumsum([0, dn, dn, dn, dn, fw, 4 * N_HEADS])
    w_t32 = jnp.swapaxes(w_in, 0, 1)
    tn = 1024
    row_starts = tuple(range(int(o_g), w_t32.shape[0], tn)) + tuple(range(int(o_q), int(o_s), tn))
    q_blk = (2 * d) // HEAD_DIM
    z_blk = q_blk + 3 * N_HEADS

    def gate_row(fwd, bwd):
        return jnp.pad(jnp.concatenate([fwd, bwd]).astype(F32),
                       (2 * N_HEADS, LANES - 4 * N_HEADS)).reshape(1, LANES)

    x2d = x.reshape(m, d)
    w_t, h2d, gates, gates_t = _prologue(w_t32, x2d, g_mix.reshape(1, d), gate_row(a_log_f, a_log_b),
                                         gate_row(dt_f, dt_b), tr=_largest_divisor(w_t32.shape[0], 1024, SLAB_ALIGN),
                                         xr=min(m, 512), scal_start=int(o_s), n_scal=4 * N_HEADS)
    c_half, s_half, perm, wd = _dft_tables(t, GROUP_DIM)
    assert tn == fw and row_starts[-1] == o_f, "the channel DFT rides on the last (Fourier) column step"
    proj, g2 = _in_proj(h2d, w_t, jnp.asarray(wd).astype(BF16), tm=min(m, 1024), tn=tn,
                        row_starts=row_starts, t=t)
    proj3 = proj.reshape(b, t, proj.shape[1])

    og, w_dn16, w_f16, w_o16, w_up16, w_down16 = _delta(
        proj3, gates.reshape(b, t, LANES), gates_t, conv_w, g_head.reshape(1, HEAD_DIM),
        [w_dn_up, w_fourier, w_o, w_mlp_up, w_mlp_down], q_blk=q_blk, z_blk=z_blk)

    fr = _dft_seq(g2, jnp.asarray(c_half).astype(BF16), jnp.asarray(s_half).astype(BF16),
                  jnp.asarray(perm).astype(BF16), scale=float((t * GROUP_DIM) ** -0.5), tn=fw)

    x1, h2 = _merge(og.reshape(m, dn), fr, proj, x2d, w_dn16, w_f16, w_o16, g_mlp.reshape(1, d),
                    t=t, tm=min(t, 256))
    return _mlp(h2, w_up16, w_down16, x1, g_out.reshape(1, d), tm=min(m, 1024), tf=512)


def kernel(x, g_mix, w_in, conv_w, a_log_fwd, a_log_bwd, dt_bias_fwd, dt_bias_bwd, g_dn_head, w_dn_up,
           w_fourier, w_o, g_mlp, w_mlp_up, w_mlp_down, g_final):
    depth = g_mix.shape[0]
    assert depth == 1, "the final rmsnorm is fused into the last block's MLP kernel"
    b, t, d = x.shape
    out = _layer(x, g_mix[0], w_in[0], conv_w[0], a_log_fwd[0], a_log_bwd[0], dt_bias_fwd[0],
                 dt_bias_bwd[0], g_dn_head[0], w_dn_up[0], w_fourier[0], w_o[0], g_mlp[0],
                 w_mlp_up[0], w_mlp_down[0], g_final)
    return out.reshape(b, t, d)
```

```python
import functools

import jax
import jax.numpy as jnp
import numpy as np
from jax import lax
from jax.experimental import pallas as pl
from jax.experimental.pallas import tpu as pltpu

F32 = jnp.float32
BF16 = jnp.bfloat16

EPS = 1e-6
N_HEADS = 8
HEAD_DIM = 128
CHUNK = 64
CONV_WIDTH = 5
N_GROUPS = 8
GROUP_DIM = 128
LANES = 128
SUBLANES = 8
VMEM_LIMIT = 56 * 1024 * 1024
SLAB_ALIGN = 32
NORM_ROWS = 128

_NT = (((1,), (1,)), ((), ()))


def _dot(a, b):
    return jnp.dot(a, b, preferred_element_type=F32)


def _dot_nt(a, b):
    return lax.dot_general(a, b, _NT, preferred_element_type=F32)


def _split3(x):
    hi = x.astype(BF16)
    r = x - hi.astype(F32)
    mid = r.astype(BF16)
    r = r - mid.astype(F32)
    return hi, mid, r.astype(BF16)


def _sigmoid(x):
    return 1.0 / (1.0 + jnp.exp(-x))


def _silu(x):
    return x * _sigmoid(x)


def _softplus(x):
    return jnp.maximum(x, 0.0) + jnp.log1p(jnp.exp(-jnp.abs(x)))


GATE_ROWS = 256


def _prologue_kernel(w_ref, x_ref, g_ref, ws_ref, alog_ref, dt_ref, w16_ref, h_ref, gates_ref, gates_t_ref,
                     ws_scr, *, xr, rb, n_x_steps):
    w16_ref[...] = w_ref[...].astype(BF16)

    @pl.when(pl.program_id(0) == 0)
    def _():
        ws_scr[...] = jnp.zeros(ws_scr.shape, BF16)
        ws_scr[0:ws_ref.shape[0], :] = ws_ref[...].astype(BF16)

    @pl.when(pl.program_id(0) < n_x_steps)
    def _():
        def body(r, c):
            rows = pl.ds(pl.multiple_of(r * rb, rb), rb)
            xf = x_ref[rows, :]
            ms = jnp.mean(xf * xf, axis=-1, keepdims=True)
            h_ref[rows, :] = (xf * lax.rsqrt(ms + EPS) * g_ref[...]).astype(BF16)
            return c
        lax.fori_loop(0, xr // rb, body, 0)

        gr = GATE_ROWS
        ri = lax.broadcasted_iota(jnp.int32, (gr, gr), 0)
        ci = lax.broadcasted_iota(jnp.int32, (gr, gr), 1)
        same = (ri // CHUNK) == (ci // CHUNK)
        ltri = jnp.where(same & (ri >= ci), 1.0, 0.0).astype(BF16)
        utri = jnp.where(same & (ri <= ci), 1.0, 0.0).astype(BF16)
        lane = lax.broadcasted_iota(jnp.int32, (gr, LANES), 1)
        neg_a = -jnp.exp(alog_ref[...])
        for sub in range(xr // gr):
            rows = slice(sub * gr, (sub + 1) * gr)
            s = _dot_nt(h_ref[rows, :], ws_scr[...])
            g3 = _split3(neg_a * _softplus(s + dt_ref[...]))
            gc_f = _dot(ltri, g3[0]) + _dot(ltri, g3[1]) + _dot(ltri, g3[2])
            gc_b = _dot(utri, g3[0]) + _dot(utri, g3[1]) + _dot(utri, g3[2])
            tile = jnp.where(lane < 2 * N_HEADS, _sigmoid(s), jnp.where(lane < 3 * N_HEADS, gc_f, gc_b))
            gates_ref[rows, :] = tile
            gates_t_ref[:, rows] = tile.T


def _prologue(w_t, x2d, g, alog_row, dt_row, *, tr, xr, scal_start, n_scal):
    rows, d = w_t.shape
    assert scal_start % n_scal == 0 and n_scal % SUBLANES == 0 and n_scal <= LANES
    m = x2d.shape[0]
    n_steps, n_x = rows // tr, m // xr
    assert n_x <= n_steps and xr % GATE_ROWS == 0
    xi = lambda i: jnp.minimum(i, n_x - 1)
    return pl.pallas_call(
        functools.partial(_prologue_kernel, xr=xr, rb=NORM_ROWS, n_x_steps=n_x),
        grid=(n_steps,),
        in_specs=[
            pl.BlockSpec((tr, d), lambda i: (i, 0)),
            pl.BlockSpec((xr, d), lambda i: (xi(i), 0)),
            pl.BlockSpec((1, d), lambda i: (0, 0)),
            pl.BlockSpec((n_scal, d), lambda i: (scal_start // n_scal, 0)),
            pl.BlockSpec((1, LANES), lambda i: (0, 0)),
            pl.BlockSpec((1, LANES), lambda i: (0, 0)),
        ],
        out_specs=[
            pl.BlockSpec((tr, d), lambda i: (i, 0)),
            pl.BlockSpec((xr, d), lambda i: (xi(i), 0)),
            pl.BlockSpec((xr, LANES), lambda i: (xi(i), 0)),
            pl.BlockSpec((LANES, xr), lambda i: (0, xi(i))),
        ],
        out_shape=[jax.ShapeDtypeStruct((rows, d), BF16), jax.ShapeDtypeStruct((m, d), BF16),
                   jax.ShapeDtypeStruct((m, LANES), F32), jax.ShapeDtypeStruct((LANES, m), F32)],
        scratch_shapes=[pltpu.VMEM((LANES, d), BF16)],
        compiler_params=pltpu.CompilerParams(dimension_semantics=("arbitrary",), vmem_limit_bytes=VMEM_LIMIT),
        name="prologue",
    )(w_t, x2d, g, w_t, alog_row, dt_row)


def _in_proj_kernel(st_ref, h_ref, w_ref, wd_ref, o_ref, g2_ref):
    o_ref[...] = _dot_nt(h_ref[...], w_ref[...])

    @pl.when(pl.program_id(1) == pl.num_programs(1) - 1)
    def _():
        for g in range(N_GROUPS):
            cols = slice(g * GROUP_DIM, (g + 1) * GROUP_DIM)
            y = _dot(o_ref[:, cols].astype(BF16), wd_ref[...])
            g2_ref[0, :, cols] = y[:, :GROUP_DIM].astype(BF16)
            g2_ref[1, :, cols] = y[:, GROUP_DIM:].astype(BF16)


def _in_proj(h2d, w_t, wd, *, tm, tn, row_starts, t):
    m, d = h2d.shape
    starts = jnp.asarray(row_starts, jnp.int32)
    n_steps = len(row_starts)
    return pl.pallas_call(
        _in_proj_kernel,
        grid_spec=pltpu.PrefetchScalarGridSpec(
            num_scalar_prefetch=1,
            grid=(m // tm, n_steps),
            in_specs=[
                pl.BlockSpec((tm, d), lambda i, j, st: (i, 0)),
                pl.BlockSpec((pl.Element(tn), pl.Element(d)), lambda i, j, st: (pl.multiple_of(st[j], SLAB_ALIGN), 0)),
                pl.BlockSpec((GROUP_DIM, 2 * GROUP_DIM), lambda i, j, st: (0, 0)),
            ],
            out_specs=[
                pl.BlockSpec((tm, tn), lambda i, j, st: (i, j)),
                pl.BlockSpec((2, tm, tn), lambda i, j, st: (0, i % (t // tm), i // (t // tm))),
            ],
        ),
        out_shape=[jax.ShapeDtypeStruct((m, n_steps * tn), F32),
                   jax.ShapeDtypeStruct((2, t, (m // t) * tn), BF16)],
        compiler_params=pltpu.CompilerParams(
            dimension_semantics=("parallel", "arbitrary"), vmem_limit_bytes=VMEM_LIMIT),
        name="in_proj",
    )(starts, h2d, w_t, wd)


def _delta_kernel(*refs, t, rb, gsz, n_cast):
    (q_ref, k_ref, v_ref, z_ref, gates_ref, gates_t_ref, cwq_ref, cwk_ref, cwv_ref, gh_ref), refs = (
        refs[:10], refs[10:])
    cast_in, (out_ref,), cast_out, refs = (
        refs[:n_cast], refs[n_cast:n_cast + 1], refs[n_cast + 1:2 * n_cast + 1], refs[2 * n_cast + 1:])
    pad_scr, qkv_scr, o_scr, kw0, r0, qp0, op0, gl0, kw1, r1, qp1, op1, gl1 = refs

    for src, dst in zip(cast_in, cast_out):
        dst[...] = src[...].astype(dst.dtype)

    nc = t // CHUNK
    ng = nc // gsz
    nrb = t // rb
    h = pl.program_id(1)

    half = (CONV_WIDTH - 1) // 2
    streams = ((q_ref, cwq_ref), (k_ref, cwk_ref), (v_ref, cwv_ref))
    zero8 = jnp.zeros((SUBLANES, LANES), F32)
    for idx, (src_ref, _) in enumerate(streams):
        pad_scr[0, idx, 0:SUBLANES, :] = zero8
        pad_scr[0, idx, SUBLANES:rb + 2 * SUBLANES, :] = src_ref[0:rb + SUBLANES, :]
        pad_scr[1, idx, 0:rb + SUBLANES, :] = src_ref[t - rb - SUBLANES:t, :]
        pad_scr[1, idx, rb + SUBLANES:rb + 2 * SUBLANES, :] = zero8

    for r in range(nrb):
        for idx, (src_ref, cw_ref) in enumerate(streams):
            acc = None
            for i in range(CONV_WIDTH):
                if r == 0 or r == nrb - 1:
                    off = SUBLANES - half + i
                    tap = pad_scr[0 if r == 0 else 1, idx, off:off + rb, :]
                else:
                    off = r * rb - half + i
                    tap = src_ref[off:off + rb, :]
                term = cw_ref[i:i + 1, :] * tap
                acc = term if acc is None else acc + term
            y = _silu(acc)
            if idx < 2:
                y = y * lax.rsqrt(jnp.sum(y * y, axis=-1, keepdims=True) + EPS)
            if idx == 0:
                y = y * (HEAD_DIM ** -0.5)
            qkv_scr[idx, r * rb:(r + 1) * rb, :] = y

    lane = lax.broadcasted_iota(jnp.int32, (CHUNK, LANES), 1)
    ri = lax.broadcasted_iota(jnp.int32, (CHUNK, LANES), 0)
    lo = lane < CHUNK
    ci = jnp.where(lo, lane, lane - CHUNK)
    ahead = jnp.where(lo, ri - ci, ci - ri)
    incl = ahead >= 0
    strict = ahead > 0
    eye = jnp.where(ri == ci, 1.0, 0.0)
    lane_row = lax.broadcasted_iota(jnp.int32, (1, LANES), 1)
    head_row = lax.broadcasted_iota(jnp.int32, (N_HEADS, LANES), 0)
    bufs = ((kw0, r0, qp0, op0, gl0), (kw1, r1, qp1, op1, gl1))

    def chunk_of(gi, j, d):
        return gi * gsz + j if d == 0 else nc - 1 - (gi * gsz + j)

    def column(tile, c):
        return jnp.sum(jnp.where(lane == c, tile, 0.0), axis=-1, keepdims=True)

    def pack(m):
        return jnp.where(lo, m[:CHUNK], m[CHUNK:])

    def blockdiag(m):
        zero = jnp.zeros_like(m)
        return jnp.concatenate([jnp.where(lo, m, zero), jnp.where(lo, zero, m)], axis=0)

    def prep(gi, buf):
        kw_s, r_s, qp_s, op_s, gl_s = buf
        ch = [(j, d) for j in range(gsz) for d in range(2)]
        rows = [pl.ds(chunk_of(gi, j, d) * CHUNK, CHUNK) for j, d in ch]
        q = [qkv_scr[0, r, :] for r in rows]
        k = [qkv_scr[1, r, :] for r in rows]
        v = [qkv_scr[2, r, :] for r in rows]
        gt = [gates_ref[r, :] for r in rows]
        beta = [column(a, d * N_HEADS + h) for a, (j, d) in zip(gt, ch)]
        g = [column(a, (2 + d) * N_HEADS + h) for a, (j, d) in zip(gt, ch)]
        gl = [a[CHUNK - 1:CHUNK, :] if d == 0 else a[0:1, :] for a, (j, d) in zip(g, ch)]
        kb = [a * bt for a, bt in zip(k, beta)]

        def g_row(j, d):
            c = chunk_of(gi, j, d)
            odd = c % 2
            win = gates_t_ref[(2 + d) * N_HEADS:(3 + d) * N_HEADS, pl.ds((c - odd) * CHUNK, 2 * CHUNK)]
            win = jnp.sum(jnp.where(head_row == h, win, 0.0), axis=0, keepdims=True)
            return pltpu.roll(win, CHUNK, axis=1) if odd != d else win

        pair = lambda xs, j: jnp.concatenate([xs[2 * j], xs[2 * j + 1]], axis=0)
        k2 = [pair(k, j).astype(BF16) for j in range(gsz)]
        kk = [pack(_dot_nt(pair(kb, j).astype(BF16), k2[j])) for j in range(gsz)]
        qk_raw = [pack(_dot_nt(pair(q, j).astype(BF16), k2[j])) for j in range(gsz)]
        diff = [jnp.where(lo, g[2 * j], g[2 * j + 1])
                - jnp.where(lane_row < CHUNK, g_row(j, 0), g_row(j, 1)) for j in range(gsz)]
        decay = [jnp.where(incl, jnp.exp(jnp.where(incl, a, 0.0)), 0.0) for a in diff]
        pw = [jnp.where(strict, -(a * dc), 0.0) for a, dc in zip(kk, decay)]
        x = [eye + a for a in pw]
        p16 = [a.astype(BF16) for a in pw]
        p16 = [_dot(a, blockdiag(a)).astype(BF16) for a in p16]
        for _ in range(4):
            res = [_dot(jnp.concatenate([a, b.astype(BF16)], axis=0), blockdiag(a)) for a, b in zip(p16, x)]
            p16 = [a[:CHUNK].astype(BF16) for a in res]
            x = [a + b[CHUNK:] for a, b in zip(x, res)]
        x = [a + _dot(a.astype(BF16), blockdiag(b)) for a, b in zip(x, p16)]
        qk2 = [a * dc for a, dc in zip(qk_raw, decay)]
        unpack = lambda xs: [(xs[j][:, :CHUNK] if d == 0 else xs[j][:, CHUNK:]).astype(BF16) for j, d in ch]
        xinv = unpack(x)
        qk = unpack(qk2)
        eg = [jnp.exp(a) for a in g]
        rhs = [jnp.concatenate([a * bt, b * e], axis=1).astype(BF16)
               for a, bt, b, e in zip(v, beta, kb, eg)]
        sol = [_dot(a, b).astype(BF16) for a, b in zip(xinv, rhs)]
        kd = [(a * jnp.exp(l - b)).T.astype(BF16) for a, b, l in zip(k, g, gl)]
        kdu_kdw = [_dot(a, b) for a, b in zip(kd, sol)]
        qku_qkw = [_dot(a, b) for a, b in zip(qk, sol)]
        for i, (j, d) in enumerate(ch):
            srows = slice(j * HEAD_DIM, (j + 1) * HEAD_DIM)
            crows = slice(j * CHUNK, (j + 1) * CHUNK)
            r_s[d, srows, :] = kdu_kdw[i][:, :HEAD_DIM]
            kw_s[d, srows, :] = kdu_kdw[i][:, HEAD_DIM:].astype(BF16)
            op_s[d, crows, :] = qku_qkw[i][:, :HEAD_DIM]
            qp_s[d, crows, :] = (q[i] * eg[i] - qku_qkw[i][:, HEAD_DIM:]).astype(BF16)
            gl_s[d, j * SUBLANES:(j + 1) * SUBLANES, :] = jnp.broadcast_to(jnp.exp(gl[i]), (SUBLANES, LANES))

    seen = (set(), set())

    def scan(gi, buf, carry):
        kw_s, r_s, qp_s, op_s, gl_s = buf
        st = list(carry)
        for j in range(gsz):
            srows = slice(j * HEAD_DIM, (j + 1) * HEAD_DIM)
            crows = slice(j * CHUNK, (j + 1) * CHUNK)
            for d in range(2):
                c = chunk_of(gi, j, d)
                rows = pl.ds(c * CHUNK, CHUNK)
                st16 = st[d].astype(BF16)
                o = _dot(qp_s[d, crows, :], st16) + op_s[d, crows, :]
                if c in seen[1 - d]:
                    o = o + o_scr[1 - d, rows, :]
                    y = o * lax.rsqrt(jnp.mean(o * o, axis=-1, keepdims=True) + EPS) * gh_ref[...]
                    out_ref[rows, :] = (y * _silu(z_ref[rows, :])).astype(BF16)
                else:
                    o_scr[d, rows, :] = o
                seen[d].add(c)
                gl = gl_s[d, j * SUBLANES:j * SUBLANES + 1, :]
                st[d] = gl * st[d] + (r_s[d, srows, :] - _dot(kw_s[d, srows, :], st16))
        return tuple(st)

    carry = (jnp.zeros((HEAD_DIM, HEAD_DIM), F32),) * 2
    prep(0, bufs[0])
    for gi in range(1, ng):
        prep(gi, bufs[gi % 2])
        carry = scan(gi - 1, bufs[(gi - 1) % 2], carry)
    scan(ng - 1, bufs[(ng - 1) % 2], carry)
    assert all(len(sn) == nc for sn in seen)


def _delta(proj3, gates3, gates_t, conv_w, g_head, cast_weights, *, q_blk, z_blk):
    b, t, _ = proj3.shape
    n_steps = b * N_HEADS
    slab_specs = [pl.BlockSpec((w.shape[0] // n_steps, w.shape[1]), lambda bi, hi: (bi * N_HEADS + hi, 0))
                  for w in cast_weights]
    rb = min(t // 2, 256)
    gsz = min(16, t // (2 * CHUNK))
    assert (t // CHUNK) % (2 * gsz) == 0

    def col(off):
        return pl.BlockSpec((None, t, HEAD_DIM), lambda bi, hi: (bi, 0, off + hi))

    def cw(off):
        return pl.BlockSpec((CONV_WIDTH, HEAD_DIM), lambda bi, hi: (0, off + hi))

    operands = [
        pltpu.VMEM((2, gsz * HEAD_DIM, HEAD_DIM), BF16),
        pltpu.VMEM((2, gsz * HEAD_DIM, HEAD_DIM), F32),
        pltpu.VMEM((2, gsz * CHUNK, HEAD_DIM), BF16),
        pltpu.VMEM((2, gsz * CHUNK, HEAD_DIM), F32),
        pltpu.VMEM((2, gsz * SUBLANES, LANES), F32),
    ]
    return pl.pallas_call(
        functools.partial(_delta_kernel, t=t, rb=rb, gsz=gsz, n_cast=len(cast_weights)),
        grid=(b, N_HEADS),
        in_specs=[
            col(q_blk), col(q_blk + N_HEADS), col(q_blk + 2 * N_HEADS), col(z_blk),
            pl.BlockSpec((None, t, LANES), lambda bi, hi: (bi, 0, 0)),
            pl.BlockSpec((LANES, t), lambda bi, hi: (0, bi)),
            cw(0), cw(N_HEADS), cw(2 * N_HEADS),
            pl.BlockSpec((1, HEAD_DIM), lambda bi, hi: (0, 0)),
        ] + slab_specs,
        out_specs=[pl.BlockSpec((None, t, HEAD_DIM), lambda bi, hi: (bi, 0, hi))] + slab_specs,
        out_shape=[jax.ShapeDtypeStruct((b, t, N_HEADS * HEAD_DIM), BF16)]
        + [jax.ShapeDtypeStruct(w.shape, BF16) for w in cast_weights],
        scratch_shapes=[
            pltpu.VMEM((2, 3, rb + 2 * SUBLANES, LANES), F32),
            pltpu.VMEM((3, t, HEAD_DIM), F32),
            pltpu.VMEM((2, t, HEAD_DIM), F32),
        ] + operands + operands,
        compiler_params=pltpu.CompilerParams(
            dimension_semantics=("parallel", "arbitrary"), vmem_limit_bytes=VMEM_LIMIT),
        name="delta",
    )(proj3, proj3, proj3, proj3, gates3, gates_t, conv_w, conv_w, conv_w, g_head, *cast_weights)


DFT_COLS = 256


def _dft_seq_kernel(g_ref, c_ref, s_ref, p_ref, o_ref, *, scale):
    half = s_ref.shape[0]
    row = lax.broadcasted_iota(jnp.int32, (half, DFT_COLS), 0)
    for c0 in range(0, o_ref.shape[1], DFT_COLS):
        cols = slice(c0, c0 + DFT_COLS)
        a = _dot(c_ref[...], g_ref[0, :, cols]) * scale
        b = _dot(s_ref[...], g_ref[1, :, cols]) * scale
        o_ref[0:half, cols] = (a[:half] + b).astype(o_ref.dtype)
        mirrored = _dot(p_ref[...], (a[:half] - b).astype(BF16))
        o_ref[half:2 * half, cols] = jnp.where(row == 0, a[half:half + 1], mirrored).astype(o_ref.dtype)


def _dft_seq(g2, c_half, s_half, perm, *, scale, tn):
    _, t, n = g2.shape

    def const(shape):
        return pl.BlockSpec(shape, lambda j: (0, 0), pipeline_mode=pl.Buffered(1))

    return pl.pallas_call(
        functools.partial(_dft_seq_kernel, scale=scale),
        grid=(n // tn,),
        in_specs=[
            pl.BlockSpec((2, t, tn), lambda j: (0, 0, j)),
            const(c_half.shape), const(s_half.shape), const(perm.shape),
        ],
        out_specs=pl.BlockSpec((t, tn), lambda j: (0, j)),
        out_shape=jax.ShapeDtypeStruct((t, n), BF16),
        compiler_params=pltpu.CompilerParams(dimension_semantics=("parallel",), vmem_limit_bytes=VMEM_LIMIT),
        name="dft_seq",
    )(g2, c_half, s_half, perm)


@functools.lru_cache(maxsize=None)
def _dft_tables(t, d):
    half = t // 2
    k = np.arange(half + SUBLANES, dtype=np.int64)
    n = np.arange(t, dtype=np.int64)
    ang = 2.0 * np.pi * ((k[:, None] * n[None, :]) % t) / t
    c_half = np.cos(ang).astype(np.float32)
    s_half = np.sin(ang[:half]).astype(np.float32)
    perm = np.zeros((half, half), np.float32)
    perm[np.arange(1, half), half - np.arange(1, half)] = 1.0
    c = np.arange(d, dtype=np.int64)
    angd = 2.0 * np.pi * ((c[:, None] * c[None, :]) % d) / d
    wd = np.concatenate([np.cos(angd), -np.sin(angd)], axis=1).astype(np.float32)
    return c_half, s_half, perm, wd


def _merge_kernel(og_ref, fr_ref, ga_ref, gf_ref, x_ref, wdn_ref, wf_ref, wo_ref, g_ref,
                  x1_ref, h2_ref):
    ya = _dot(og_ref[...], wdn_ref[...])
    yf = _dot(fr_ref[...], wf_ref[...])
    merged = _sigmoid(ga_ref[...]) * ya + _sigmoid(gf_ref[...]) * yf
    x1 = x_ref[...] + _dot(merged.astype(BF16), wo_ref[...])
    x1_ref[...] = x1
    ms = jnp.mean(x1 * x1, axis=-1, keepdims=True)
    h2_ref[...] = (x1 * lax.rsqrt(ms + EPS) * g_ref[...]).astype(BF16)


def _merge(og2, fr, proj2, x2d, w_dn, w_f, w_o, g_mlp, *, t, tm):
    m, d = x2d.shape
    bw = og2.shape[1]
    tpb = t // tm

    def const(shape):
        return pl.BlockSpec(shape, lambda i: (0, 0), pipeline_mode=pl.Buffered(1))

    return pl.pallas_call(
        _merge_kernel,
        grid=(m // tm,),
        in_specs=[
            pl.BlockSpec((tm, bw), lambda i: (i, 0)),
            pl.BlockSpec((tm, bw), lambda i: (i % tpb, i // tpb)),
            pl.BlockSpec((tm, d), lambda i: (i, 0)),
            pl.BlockSpec((tm, d), lambda i: (i, 1)),
            pl.BlockSpec((tm, d), lambda i: (i, 0)),
            const((bw, d)), const((bw, d)), const((d, d)), const((1, d)),
        ],
        out_specs=[pl.BlockSpec((tm, d), lambda i: (i, 0)), pl.BlockSpec((tm, d), lambda i: (i, 0))],
        out_shape=[jax.ShapeDtypeStruct((m, d), F32), jax.ShapeDtypeStruct((m, d), BF16)],
        compiler_params=pltpu.CompilerParams(
            dimension_semantics=("parallel",), vmem_limit_bytes=VMEM_LIMIT),
        name="merge",
    )(og2, fr, proj2, proj2, x2d, w_dn, w_f, w_o, g_mlp)


def _mlp_kernel(h_ref, wu_ref, wd_ref, x1_ref, g_ref, o_ref):
    j = pl.program_id(1)

    @pl.when(j == 0)
    def _():
        o_ref[...] = x1_ref[...]

    a = jnp.maximum(_dot(h_ref[...], wu_ref[...]), 0.0)
    o_ref[...] += _dot((a * a).astype(BF16), wd_ref[...])

    @pl.when(j == pl.num_programs(1) - 1)
    def _():
        y = o_ref[...]
        ms = jnp.mean(y * y, axis=-1, keepdims=True)
        o_ref[...] = y * lax.rsqrt(ms + EPS) * g_ref[...]


def _mlp(h2, w_up, w_down, x1, g_final, *, tm, tf):
    m, d = x1.shape
    ff = w_up.shape[1]
    return pl.pallas_call(
        _mlp_kernel,
        grid=(m // tm, ff // tf),
        in_specs=[
            pl.BlockSpec((tm, d), lambda i, j: (i, 0)),
            pl.BlockSpec((d, tf), lambda i, j: (0, j)),
            pl.BlockSpec((tf, d), lambda i, j: (j, 0)),
            pl.BlockSpec((tm, d), lambda i, j: (i, 0)),
            pl.BlockSpec((1, d), lambda i, j: (0, 0)),
        ],
        out_specs=pl.BlockSpec((tm, d), lambda i, j: (i, 0)),
        out_shape=jax.ShapeDtypeStruct((m, d), F32),
        compiler_params=pltpu.CompilerParams(
            dimension_semantics=("parallel", "arbitrary"), vmem_limit_bytes=VMEM_LIMIT),
        name="mlp",
    )(h2, w_up, w_down, x1, g_final)


def _largest_divisor(n, cap, multiple):
    return max(c for c in range(multiple, cap + 1, multiple) if n % c == 0)


def _layer(x, g_mix, w_in, conv_w, a_log_f, a_log_b, dt_f, dt_b, g_head, w_dn_up, w_fourier, w_o,
           g_mlp, w_mlp_up, w_mlp_down, g_out):
    b, t, d = x.shape
    m = b * t
    dn = N_HEADS * HEAD_DIM
    fw = N_GROUPS * GROUP_DIM
    o_q, o_k, o_v, o_z, o_f, o_s, o_g = np.cumsum([0, dn, dn, dn, dn, fw, 4 * N_HEADS])
    w_t32 = jnp.swapaxes(w_in, 0, 1)
    tn = 1024
    row_starts = tuple(range(int(o_g), w_t32.shape[0], tn)) + tuple(range(int(o_q), int(o_s), tn))
    q_blk = (2 * d) // HEAD_DIM
    z_blk = q_blk + 3 * N_HEADS

    def gate_row(fwd, bwd):
        return jnp.pad(jnp.concatenate([fwd, bwd]).astype(F32),
                       (2 * N_HEADS, LANES - 4 * N_HEADS)).reshape(1, LANES)

    x2d = x.reshape(m, d)
    w_t, h2d, gates, gates_t = _prologue(w_t32, x2d, g_mix.reshape(1, d), gate_row(a_log_f, a_log_b),
                                         gate_row(dt_f, dt_b), tr=_largest_divisor(w_t32.shape[0], 1024, SLAB_ALIGN),
                                         xr=min(m, 512), scal_start=int(o_s), n_scal=4 * N_HEADS)
    c_half, s_half, perm, wd = _dft_tables(t, GROUP_DIM)
    assert tn == fw and row_starts[-1] == o_f, "the channel DFT rides on the last (Fourier) column step"
    proj, g2 = _in_proj(h2d, w_t, jnp.asarray(wd).astype(BF16), tm=min(m, 1024), tn=tn,
                        row_starts=row_starts, t=t)
    proj3 = proj.reshape(b, t, proj.shape[1])

    og, w_dn16, w_f16, w_o16, w_up16, w_down16 = _delta(
        proj3, gates.reshape(b, t, LANES), gates_t, conv_w, g_head.reshape(1, HEAD_DIM),
        [w_dn_up, w_fourier, w_o, w_mlp_up, w_mlp_down], q_blk=q_blk, z_blk=z_blk)

    fr = _dft_seq(g2, jnp.asarray(c_half).astype(BF16), jnp.asarray(s_half).astype(BF16),
                  jnp.asarray(perm).astype(BF16), scale=float((t * GROUP_DIM) ** -0.5), tn=fw)

    x1, h2 = _merge(og.reshape(m, dn), fr, proj, x2d, w_dn16, w_f16, w_o16, g_mlp.reshape(1, d),
                    t=t, tm=min(t, 256))
    return _mlp(h2, w_up16, w_down16, x1, g_out.reshape(1, d), tm=min(m, 512), tf=1024)


def kernel(x, g_mix, w_in, conv_w, a_log_fwd, a_log_bwd, dt_bias_fwd, dt_bias_bwd, g_dn_head, w_dn_up,
           w_fourier, w_o, g_mlp, w_mlp_up, w_mlp_down, g_final):
    depth = g_mix.shape[0]
    assert depth == 1, "the final rmsnorm is fused into the last block's MLP kernel"
    b, t, d = x.shape
    out = _layer(x, g_mix[0], w_in[0], conv_w[0], a_log_fwd[0], a_log_bwd[0], dt_bias_fwd[0],
                 dt_bias_bwd[0], g_dn_head[0], w_dn_up[0], w_fourier[0], w_o[0], g_mlp[0],
                 w_mlp_up[0], w_mlp_down[0], g_final)
    return out.reshape(b, t, d)
```

```python
import functools

import jax
import jax.numpy as jnp
import numpy as np
from jax import lax
from jax.experimental import pallas as pl
from jax.experimental.pallas import tpu as pltpu

F32 = jnp.float32
BF16 = jnp.bfloat16

EPS = 1e-6
N_HEADS = 8
HEAD_DIM = 128
CHUNK = 64
CONV_WIDTH = 5
N_GROUPS = 8
GROUP_DIM = 128
LANES = 128
SUBLANES = 8
VMEM_LIMIT = 56 * 1024 * 1024
SLAB_ALIGN = 32
NORM_ROWS = 128

_NT = (((1,), (1,)), ((), ()))


def _dot(a, b):
    return jnp.dot(a, b, preferred_element_type=F32)


def _dot_nt(a, b):
    return lax.dot_general(a, b, _NT, preferred_element_type=F32)


def _split3(x):
    hi = x.astype(BF16)
    r = x - hi.astype(F32)
    mid = r.astype(BF16)
    r = r - mid.astype(F32)
    return hi, mid, r.astype(BF16)


def _sigmoid(x):
    return 1.0 / (1.0 + jnp.exp(-x))


def _silu(x):
    return x * _sigmoid(x)


def _softplus(x):
    return jnp.maximum(x, 0.0) + jnp.log1p(jnp.exp(-jnp.abs(x)))


GATE_ROWS = 256


def _prologue_kernel(w_ref, x_ref, g_ref, ws_ref, alog_ref, dt_ref, w16_ref, h_ref, gates_ref, gates_t_ref,
                     ws_scr, *, xr, rb, n_x_steps):
    w16_ref[...] = w_ref[...].astype(BF16)

    @pl.when(pl.program_id(0) == 0)
    def _():
        ws_scr[...] = jnp.zeros(ws_scr.shape, BF16)
        ws_scr[0:ws_ref.shape[0], :] = ws_ref[...].astype(BF16)

    @pl.when(pl.program_id(0) < n_x_steps)
    def _():
        def body(r, c):
            rows = pl.ds(pl.multiple_of(r * rb, rb), rb)
            xf = x_ref[rows, :]
            ms = jnp.mean(xf * xf, axis=-1, keepdims=True)
            h_ref[rows, :] = (xf * lax.rsqrt(ms + EPS) * g_ref[...]).astype(BF16)
            return c
        lax.fori_loop(0, xr // rb, body, 0)

        gr = GATE_ROWS
        ri = lax.broadcasted_iota(jnp.int32, (gr, gr), 0)
        ci = lax.broadcasted_iota(jnp.int32, (gr, gr), 1)
        same = (ri // CHUNK) == (ci // CHUNK)
        ltri = jnp.where(same & (ri >= ci), 1.0, 0.0).astype(BF16)
        utri = jnp.where(same & (ri <= ci), 1.0, 0.0).astype(BF16)
        lane = lax.broadcasted_iota(jnp.int32, (gr, LANES), 1)
        neg_a = -jnp.exp(alog_ref[...])
        for sub in range(xr // gr):
            rows = slice(sub * gr, (sub + 1) * gr)
            s = _dot_nt(h_ref[rows, :], ws_scr[...])
            g3 = _split3(neg_a * _softplus(s + dt_ref[...]))
            gc_f = _dot(ltri, g3[0]) + _dot(ltri, g3[1]) + _dot(ltri, g3[2])
            gc_b = _dot(utri, g3[0]) + _dot(utri, g3[1]) + _dot(utri, g3[2])
            tile = jnp.where(lane < 2 * N_HEADS, _sigmoid(s), jnp.where(lane < 3 * N_HEADS, gc_f, gc_b))
            gates_ref[rows, :] = tile
            gates_t_ref[:, rows] = tile.T


def _prologue(w_t, x2d, g, alog_row, dt_row, *, tr, xr, scal_start, n_scal):
    rows, d = w_t.shape
    assert scal_start % n_scal == 0 and n_scal % SUBLANES == 0 and n_scal <= LANES
    m = x2d.shape[0]
    n_steps, n_x = rows // tr, m // xr
    assert n_x <= n_steps and xr % GATE_ROWS == 0
    xi = lambda i: jnp.minimum(i, n_x - 1)
    return pl.pallas_call(
        functools.partial(_prologue_kernel, xr=xr, rb=NORM_ROWS, n_x_steps=n_x),
        grid=(n_steps,),
        in_specs=[
            pl.BlockSpec((tr, d), lambda i: (i, 0)),
            pl.BlockSpec((xr, d), lambda i: (xi(i), 0)),
            pl.BlockSpec((1, d), lambda i: (0, 0)),
            pl.BlockSpec((n_scal, d), lambda i: (scal_start // n_scal, 0)),
            pl.BlockSpec((1, LANES), lambda i: (0, 0)),
            pl.BlockSpec((1, LANES), lambda i: (0, 0)),
        ],
        out_specs=[
            pl.BlockSpec((tr, d), lambda i: (i, 0)),
            pl.BlockSpec((xr, d), lambda i: (xi(i), 0)),
            pl.BlockSpec((xr, LANES), lambda i: (xi(i), 0)),
            pl.BlockSpec((LANES, xr), lambda i: (0, xi(i))),
        ],
        out_shape=[jax.ShapeDtypeStruct((rows, d), BF16), jax.ShapeDtypeStruct((m, d), BF16),
                   jax.ShapeDtypeStruct((m, LANES), F32), jax.ShapeDtypeStruct((LANES, m), F32)],
        scratch_shapes=[pltpu.VMEM((LANES, d), BF16)],
        compiler_params=pltpu.CompilerParams(dimension_semantics=("arbitrary",), vmem_limit_bytes=VMEM_LIMIT),
        name="prologue",
    )(w_t, x2d, g, w_t, alog_row, dt_row)


def _in_proj_kernel(st_ref, h_ref, w_ref, wd_ref, o_ref, g2_ref):
    o_ref[...] = _dot_nt(h_ref[...], w_ref[...])

    @pl.when(pl.program_id(1) == pl.num_programs(1) - 1)
    def _():
        for g in range(N_GROUPS):
            cols = slice(g * GROUP_DIM, (g + 1) * GROUP_DIM)
            y = _dot(o_ref[:, cols].astype(BF16), wd_ref[...])
            g2_ref[0, :, cols] = y[:, :GROUP_DIM].astype(BF16)
            g2_ref[1, :, cols] = y[:, GROUP_DIM:].astype(BF16)


def _in_proj(h2d, w_t, wd, *, tm, tn, row_starts, t):
    m, d = h2d.shape
    starts = jnp.asarray(row_starts, jnp.int32)
    n_steps = len(row_starts)
    return pl.pallas_call(
        _in_proj_kernel,
        grid_spec=pltpu.PrefetchScalarGridSpec(
            num_scalar_prefetch=1,
            grid=(m // tm, n_steps),
            in_specs=[
                pl.BlockSpec((tm, d), lambda i, j, st: (i, 0)),
                pl.BlockSpec((pl.Element(tn), pl.Element(d)), lambda i, j, st: (pl.multiple_of(st[j], SLAB_ALIGN), 0)),
                pl.BlockSpec((GROUP_DIM, 2 * GROUP_DIM), lambda i, j, st: (0, 0)),
            ],
            out_specs=[
                pl.BlockSpec((tm, tn), lambda i, j, st: (i, j)),
                pl.BlockSpec((2, tm, tn), lambda i, j, st: (0, i % (t // tm), i // (t // tm))),
            ],
        ),
        out_shape=[jax.ShapeDtypeStruct((m, n_steps * tn), F32),
                   jax.ShapeDtypeStruct((2, t, (m // t) * tn), BF16)],
        compiler_params=pltpu.CompilerParams(
            dimension_semantics=("parallel", "arbitrary"), vmem_limit_bytes=VMEM_LIMIT),
        name="in_proj",
    )(starts, h2d, w_t, wd)


def _delta_kernel(*refs, t, rb, gsz, n_cast):
    (q_ref, k_ref, v_ref, z_ref, gates_ref, gates_t_ref, cwq_ref, cwk_ref, cwv_ref, gh_ref), refs = (
        refs[:10], refs[10:])
    cast_in, (out_ref,), cast_out, refs = (
        refs[:n_cast], refs[n_cast:n_cast + 1], refs[n_cast + 1:2 * n_cast + 1], refs[2 * n_cast + 1:])
    pad_scr, qkv_scr, o_scr, kw0, r0, qp0, op0, gl0, kw1, r1, qp1, op1, gl1 = refs

    for src, dst in zip(cast_in, cast_out):
        dst[...] = src[...].astype(dst.dtype)

    nc = t // CHUNK
    ng = nc // gsz
    nrb = t // rb
    h = pl.program_id(1)

    half = (CONV_WIDTH - 1) // 2
    streams = ((q_ref, cwq_ref), (k_ref, cwk_ref), (v_ref, cwv_ref))
    zero8 = jnp.zeros((SUBLANES, LANES), F32)
    for idx, (src_ref, _) in enumerate(streams):
        pad_scr[0, idx, 0:SUBLANES, :] = zero8
        pad_scr[0, idx, SUBLANES:rb + 2 * SUBLANES, :] = src_ref[0:rb + SUBLANES, :]
        pad_scr[1, idx, 0:rb + SUBLANES, :] = src_ref[t - rb - SUBLANES:t, :]
        pad_scr[1, idx, rb + SUBLANES:rb + 2 * SUBLANES, :] = zero8

    for r in range(nrb):
        for idx, (src_ref, cw_ref) in enumerate(streams):
            acc = None
            for i in range(CONV_WIDTH):
                if r == 0 or r == nrb - 1:
                    off = SUBLANES - half + i
                    tap = pad_scr[0 if r == 0 else 1, idx, off:off + rb, :]
                else:
                    off = r * rb - half + i
                    tap = src_ref[off:off + rb, :]
                term = cw_ref[i:i + 1, :] * tap
                acc = term if acc is None else acc + term
            y = _silu(acc)
            if idx < 2:
                y = y * lax.rsqrt(jnp.sum(y * y, axis=-1, keepdims=True) + EPS)
            if idx == 0:
                y = y * (HEAD_DIM ** -0.5)
            qkv_scr[idx, r * rb:(r + 1) * rb, :] = y

    lane = lax.broadcasted_iota(jnp.int32, (CHUNK, LANES), 1)
    ri = lax.broadcasted_iota(jnp.int32, (CHUNK, LANES), 0)
    lo = lane < CHUNK
    ci = jnp.where(lo, lane, lane - CHUNK)
    ahead = jnp.where(lo, ri - ci, ci - ri)
    incl = ahead >= 0
    strict = ahead > 0
    eye = jnp.where(ri == ci, 1.0, 0.0)
    lane_row = lax.broadcasted_iota(jnp.int32, (1, LANES), 1)
    head_row = lax.broadcasted_iota(jnp.int32, (N_HEADS, LANES), 0)
    bufs = ((kw0, r0, qp0, op0, gl0), (kw1, r1, qp1, op1, gl1))

    def chunk_of(gi, j, d):
        return gi * gsz + j if d == 0 else nc - 1 - (gi * gsz + j)

    def column(tile, c):
        return jnp.sum(jnp.where(lane == c, tile, 0.0), axis=-1, keepdims=True)

    def pack(m):
        return jnp.where(lo, m[:CHUNK], m[CHUNK:])

    def blockdiag(m):
        zero = jnp.zeros_like(m)
        return jnp.concatenate([jnp.where(lo, m, zero), jnp.where(lo, zero, m)], axis=0)

    def prep(gi, buf):
        kw_s, r_s, qp_s, op_s, gl_s = buf
        ch = [(j, d) for j in range(gsz) for d in range(2)]
        rows = [pl.ds(chunk_of(gi, j, d) * CHUNK, CHUNK) for j, d in ch]
        q = [qkv_scr[0, r, :] for r in rows]
        k = [qkv_scr[1, r, :] for r in rows]
        v = [qkv_scr[2, r, :] for r in rows]
        gt = [gates_ref[r, :] for r in rows]
        beta = [column(a, d * N_HEADS + h) for a, (j, d) in zip(gt, ch)]
        g = [column(a, (2 + d) * N_HEADS + h) for a, (j, d) in zip(gt, ch)]
        gl = [a[CHUNK - 1:CHUNK, :] if d == 0 else a[0:1, :] for a, (j, d) in zip(g, ch)]
        kb = [a * bt for a, bt in zip(k, beta)]

        def g_row(j, d):
            c = chunk_of(gi, j, d)
            odd = c % 2
            win = gates_t_ref[(2 + d) * N_HEADS:(3 + d) * N_HEADS, pl.ds((c - odd) * CHUNK, 2 * CHUNK)]
            win = jnp.sum(jnp.where(head_row == h, win, 0.0), axis=0, keepdims=True)
            return pltpu.roll(win, CHUNK, axis=1) if odd != d else win

        pair = lambda xs, j: jnp.concatenate([xs[2 * j], xs[2 * j + 1]], axis=0)
        k2 = [pair(k, j).astype(BF16) for j in range(gsz)]
        kk = [pack(_dot_nt(pair(kb, j).astype(BF16), k2[j])) for j in range(gsz)]
        qk_raw = [pack(_dot_nt(pair(q, j).astype(BF16), k2[j])) for j in range(gsz)]
        diff = [jnp.where(lo, g[2 * j], g[2 * j + 1])
                - jnp.where(lane_row < CHUNK, g_row(j, 0), g_row(j, 1)) for j in range(gsz)]
        decay = [jnp.where(incl, jnp.exp(jnp.where(incl, a, 0.0)), 0.0) for a in diff]
        pw = [jnp.where(strict, -(a * dc), 0.0) for a, dc in zip(kk, decay)]
        x = [eye + a for a in pw]
        p16 = [a.astype(BF16) for a in pw]
        p16 = [_dot(a, blockdiag(a)).astype(BF16) for a in p16]
        for _ in range(4):
            res = [_dot(jnp.concatenate([a, b.astype(BF16)], axis=0), blockdiag(a)) for a, b in zip(p16, x)]
            p16 = [a[:CHUNK].astype(BF16) for a in res]
            x = [a + b[CHUNK:] for a, b in zip(x, res)]
        x = [a + _dot(a.astype(BF16), blockdiag(b)) for a, b in zip(x, p16)]
        qk2 = [a * dc for a, dc in zip(qk_raw, decay)]
        unpack = lambda xs: [(xs[j][:, :CHUNK] if d == 0 else xs[j][:, CHUNK:]).astype(BF16) for j, d in ch]
        xinv = unpack(x)
        qk = unpack(qk2)
        eg = [jnp.exp(a) for a in g]
        rhs = [jnp.concatenate([a * bt, b * e], axis=1).astype(BF16)
               for a, bt, b, e in zip(v, beta, kb, eg)]
        sol = [_dot(a, b).astype(BF16) for a, b in zip(xinv, rhs)]
        kd = [(a * jnp.exp(l - b)).T.astype(BF16) for a, b, l in zip(k, g, gl)]
        kdu_kdw = [_dot(a, b) for a, b in zip(kd, sol)]
        qku_qkw = [_dot(a, b) for a, b in zip(qk, sol)]
        for i, (j, d) in enumerate(ch):
            srows = slice(j * HEAD_DIM, (j + 1) * HEAD_DIM)
            crows = slice(j * CHUNK, (j + 1) * CHUNK)
            r_s[d, srows, :] = kdu_kdw[i][:, :HEAD_DIM]
            kw_s[d, srows, :] = kdu_kdw[i][:, HEAD_DIM:].astype(BF16)
            op_s[d, crows, :] = qku_qkw[i][:, :HEAD_DIM]
            qp_s[d, crows, :] = (q[i] * eg[i] - qku_qkw[i][:, HEAD_DIM:]).astype(BF16)
            gl_s[d, j * SUBLANES:(j + 1) * SUBLANES, :] = jnp.broadcast_to(jnp.exp(gl[i]), (SUBLANES, LANES))

    seen = (set(), set())

    def scan(gi, buf, carry):
        kw_s, r_s, qp_s, op_s, gl_s = buf
        st = list(carry)
        for j in range(gsz):
            srows = slice(j * HEAD_DIM, (j + 1) * HEAD_DIM)
            crows = slice(j * CHUNK, (j + 1) * CHUNK)
            for d in range(2):
                c = chunk_of(gi, j, d)
                rows = pl.ds(c * CHUNK, CHUNK)
                st16 = st[d].astype(BF16)
                o = _dot(qp_s[d, crows, :], st16) + op_s[d, crows, :]
                if c in seen[1 - d]:
                    o = o + o_scr[1 - d, rows, :]
                    y = o * lax.rsqrt(jnp.mean(o * o, axis=-1, keepdims=True) + EPS) * gh_ref[...]
                    out_ref[rows, :] = (y * _silu(z_ref[rows, :])).astype(BF16)
                else:
                    o_scr[d, rows, :] = o
                seen[d].add(c)
                gl = gl_s[d, j * SUBLANES:j * SUBLANES + 1, :]
                st[d] = gl * st[d] + (r_s[d, srows, :] - _dot(kw_s[d, srows, :], st16))
        return tuple(st)

    carry = (jnp.zeros((HEAD_DIM, HEAD_DIM), F32),) * 2
    prep(0, bufs[0])
    for gi in range(1, ng):
        prep(gi, bufs[gi % 2])
        carry = scan(gi - 1, bufs[(gi - 1) % 2], carry)
    scan(ng - 1, bufs[(ng - 1) % 2], carry)
    assert all(len(sn) == nc for sn in seen)


def _delta(proj3, gates3, gates_t, conv_w, g_head, cast_weights, *, q_blk, z_blk):
    b, t, _ = proj3.shape
    n_steps = b * N_HEADS
    slab_specs = [pl.BlockSpec((w.shape[0] // n_steps, w.shape[1]), lambda bi, hi: (bi * N_HEADS + hi, 0))
                  for w in cast_weights]
    rb = min(t // 2, 256)
    gsz = min(16, t // (2 * CHUNK))
    assert (t // CHUNK) % (2 * gsz) == 0

    def col(off):
        return pl.BlockSpec((None, t, HEAD_DIM), lambda bi, hi: (bi, 0, off + hi))

    def cw(off):
        return pl.BlockSpec((CONV_WIDTH, HEAD_DIM), lambda bi, hi: (0, off + hi))

    operands = [
        pltpu.VMEM((2, gsz * HEAD_DIM, HEAD_DIM), BF16),
        pltpu.VMEM((2, gsz * HEAD_DIM, HEAD_DIM), F32),
        pltpu.VMEM((2, gsz * CHUNK, HEAD_DIM), BF16),
        pltpu.VMEM((2, gsz * CHUNK, HEAD_DIM), F32),
        pltpu.VMEM((2, gsz * SUBLANES, LANES), F32),
    ]
    return pl.pallas_call(
        functools.partial(_delta_kernel, t=t, rb=rb, gsz=gsz, n_cast=len(cast_weights)),
        grid=(b, N_HEADS),
        in_specs=[
            col(q_blk), col(q_blk + N_HEADS), col(q_blk + 2 * N_HEADS), col(z_blk),
            pl.BlockSpec((None, t, LANES), lambda bi, hi: (bi, 0, 0)),
            pl.BlockSpec((LANES, t), lambda bi, hi: (0, bi)),
            cw(0), cw(N_HEADS), cw(2 * N_HEADS),
            pl.BlockSpec((1, HEAD_DIM), lambda bi, hi: (0, 0)),
        ] + slab_specs,
        out_specs=[pl.BlockSpec((None, t, HEAD_DIM), lambda bi, hi: (bi, 0, hi))] + slab_specs,
        out_shape=[jax.ShapeDtypeStruct((b, t, N_HEADS * HEAD_DIM), BF16)]
        + [jax.ShapeDtypeStruct(w.shape, BF16) for w in cast_weights],
        scratch_shapes=[
            pltpu.VMEM((2, 3, rb + 2 * SUBLANES, LANES), F32),
            pltpu.VMEM((3, t, HEAD_DIM), F32),
            pltpu.VMEM((2, t, HEAD_DIM), F32),
        ] + operands + operands,
        compiler_params=pltpu.CompilerParams(
            dimension_semantics=("parallel", "arbitrary"), vmem_limit_bytes=VMEM_LIMIT),
        name="delta",
    )(proj3, proj3, proj3, proj3, gates3, gates_t, conv_w, conv_w, conv_w, g_head, *cast_weights)


DFT_COLS = 256


def _dft_seq_kernel(g_ref, c_ref, s_ref, p_ref, o_ref, *, scale):
    half = s_ref.shape[0]
    row = lax.broadcasted_iota(jnp.int32, (half, DFT_COLS), 0)
    for c0 in range(0, o_ref.shape[1], DFT_COLS):
        cols = slice(c0, c0 + DFT_COLS)
        a = _dot(c_ref[...], g_ref[0, :, cols]) * scale
        b = _dot(s_ref[...], g_ref[1, :, cols]) * scale
        o_ref[0:half, cols] = (a[:half] + b).astype(o_ref.dtype)
        mirrored = _dot(p_ref[...], (a[:half] - b).astype(BF16))
        o_ref[half:2 * half, cols] = jnp.where(row == 0, a[half:half + 1], mirrored).astype(o_ref.dtype)


def _dft_seq(g2, c_half, s_half, perm, *, scale, tn):
    _, t, n = g2.shape

    def const(shape):
        return pl.BlockSpec(shape, lambda j: (0, 0), pipeline_mode=pl.Buffered(1))

    return pl.pallas_call(
        functools.partial(_dft_seq_kernel, scale=scale),
        grid=(n // tn,),
        in_specs=[
            pl.BlockSpec((2, t, tn), lambda j: (0, 0, j)),
            const(c_half.shape), const(s_half.shape), const(perm.shape),
        ],
        out_specs=pl.BlockSpec((t, tn), lambda j: (0, j)),
        out_shape=jax.ShapeDtypeStruct((t, n), BF16),
        compiler_params=pltpu.CompilerParams(dimension_semantics=("parallel",), vmem_limit_bytes=VMEM_LIMIT),
        name="dft_seq",
    )(g2, c_half, s_half, perm)


@functools.lru_cache(maxsize=None)
def _dft_tables(t, d):
    half = t // 2
    k = np.arange(half + SUBLANES, dtype=np.int64)
    n = np.arange(t, dtype=np.int64)
    ang = 2.0 * np.pi * ((k[:, None] * n[None, :]) % t) / t
    c_half = np.cos(ang).astype(np.float32)
    s_half = np.sin(ang[:half]).astype(np.float32)
    perm = np.zeros((half, half), np.float32)
    perm[np.arange(1, half), half - np.arange(1, half)] = 1.0
    c = np.arange(d, dtype=np.int64)
    angd = 2.0 * np.pi * ((c[:, None] * c[None, :]) % d) / d
    wd = np.concatenate([np.cos(angd), -np.sin(angd)], axis=1).astype(np.float32)
    return c_half, s_half, perm, wd


MERGE_COLS = 512


def _merge_kernel(og_ref, fr_ref, ga_ref, gf_ref, x_ref, wdn_ref, wf_ref, wo_ref, g_ref,
                  x1_ref, h2_ref, m_scr):
    for c0 in range(0, m_scr.shape[1], MERGE_COLS):
        cols = slice(c0, c0 + MERGE_COLS)
        ya = _dot(og_ref[...], wdn_ref[:, cols])
        yf = _dot(fr_ref[...], wf_ref[:, cols])
        m_scr[:, cols] = (_sigmoid(ga_ref[:, cols]) * ya + _sigmoid(gf_ref[:, cols]) * yf).astype(BF16)
    x1 = x_ref[...] + _dot(m_scr[...], wo_ref[...])
    x1_ref[...] = x1
    ms = jnp.mean(x1 * x1, axis=-1, keepdims=True)
    h2_ref[...] = (x1 * lax.rsqrt(ms + EPS) * g_ref[...]).astype(BF16)


def _merge(og2, fr, proj2, x2d, w_dn, w_f, w_o, g_mlp, *, t, tm):
    m, d = x2d.shape
    bw = og2.shape[1]
    tpb = t // tm

    def const(shape):
        return pl.BlockSpec(shape, lambda i: (0, 0), pipeline_mode=pl.Buffered(1))

    return pl.pallas_call(
        _merge_kernel,
        grid=(m // tm,),
        in_specs=[
            pl.BlockSpec((tm, bw), lambda i: (i, 0)),
            pl.BlockSpec((tm, bw), lambda i: (i % tpb, i // tpb)),
            pl.BlockSpec((tm, d), lambda i: (i, 0)),
            pl.BlockSpec((tm, d), lambda i: (i, 1)),
            pl.BlockSpec((tm, d), lambda i: (i, 0)),
            const((bw, d)), const((bw, d)), const((d, d)), const((1, d)),
        ],
        out_specs=[pl.BlockSpec((tm, d), lambda i: (i, 0)), pl.BlockSpec((tm, d), lambda i: (i, 0))],
        out_shape=[jax.ShapeDtypeStruct((m, d), F32), jax.ShapeDtypeStruct((m, d), BF16)],
        scratch_shapes=[pltpu.VMEM((tm, d), BF16)],
        compiler_params=pltpu.CompilerParams(
            dimension_semantics=("parallel",), vmem_limit_bytes=VMEM_LIMIT),
        name="merge",
    )(og2, fr, proj2, proj2, x2d, w_dn, w_f, w_o, g_mlp)


def _mlp_kernel(h_ref, wu_ref, wd_ref, x1_ref, g_ref, o_ref):
    j = pl.program_id(1)

    @pl.when(j == 0)
    def _():
        o_ref[...] = x1_ref[...]

    a = jnp.maximum(_dot(h_ref[...], wu_ref[...]), 0.0)
    o_ref[...] += _dot((a * a).astype(BF16), wd_ref[...])

    @pl.when(j == pl.num_programs(1) - 1)
    def _():
        y = o_ref[...]
        ms = jnp.mean(y * y, axis=-1, keepdims=True)
        o_ref[...] = y * lax.rsqrt(ms + EPS) * g_ref[...]


def _mlp(h2, w_up, w_down, x1, g_final, *, tm, tf):
    m, d = x1.shape
    ff = w_up.shape[1]
    return pl.pallas_call(
        _mlp_kernel,
        grid=(m // tm, ff // tf),
        in_specs=[
            pl.BlockSpec((tm, d), lambda i, j: (i, 0)),
            pl.BlockSpec((d, tf), lambda i, j: (0, j)),
            pl.BlockSpec((tf, d), lambda i, j: (j, 0)),
            pl.BlockSpec((tm, d), lambda i, j: (i, 0)),
            pl.BlockSpec((1, d), lambda i, j: (0, 0)),
        ],
        out_specs=pl.BlockSpec((tm, d), lambda i, j: (i, 0)),
        out_shape=jax.ShapeDtypeStruct((m, d), F32),
        compiler_params=pltpu.CompilerParams(
            dimension_semantics=("parallel", "arbitrary"), vmem_limit_bytes=VMEM_LIMIT),
        name="mlp",
    )(h2, w_up, w_down, x1, g_final)


def _largest_divisor(n, cap, multiple):
    return max(c for c in range(multiple, cap + 1, multiple) if n % c == 0)


def _layer(x, g_mix, w_in, conv_w, a_log_f, a_log_b, dt_f, dt_b, g_head, w_dn_up, w_fourier, w_o,
           g_mlp, w_mlp_up, w_mlp_down, g_out):
    b, t, d = x.shape
    m = b * t
    dn = N_HEADS * HEAD_DIM
    fw = N_GROUPS * GROUP_DIM
    o_q, o_k, o_v, o_z, o_f, o_s, o_g = np.cumsum([0, dn, dn, dn, dn, fw, 4 * N_HEADS])
    w_t32 = jnp.swapaxes(w_in, 0, 1)
    tn = 1024
    row_starts = tuple(range(int(o_g), w_t32.shape[0], tn)) + tuple(range(int(o_q), int(o_s), tn))
    q_blk = (2 * d) // HEAD_DIM
    z_blk = q_blk + 3 * N_HEADS

    def gate_row(fwd, bwd):
        return jnp.pad(jnp.concatenate([fwd, bwd]).astype(F32),
                       (2 * N_HEADS, LANES - 4 * N_HEADS)).reshape(1, LANES)

    x2d = x.reshape(m, d)
    w_t, h2d, gates, gates_t = _prologue(w_t32, x2d, g_mix.reshape(1, d), gate_row(a_log_f, a_log_b),
                                         gate_row(dt_f, dt_b), tr=_largest_divisor(w_t32.shape[0], 1024, SLAB_ALIGN),
                                         xr=min(m, 512), scal_start=int(o_s), n_scal=4 * N_HEADS)
    c_half, s_half, perm, wd = _dft_tables(t, GROUP_DIM)
    assert tn == fw and row_starts[-1] == o_f, "the channel DFT rides on the last (Fourier) column step"
    proj, g2 = _in_proj(h2d, w_t, jnp.asarray(wd).astype(BF16), tm=min(m, 1024), tn=tn,
                        row_starts=row_starts, t=t)
    proj3 = proj.reshape(b, t, proj.shape[1])

    og, w_dn16, w_f16, w_o16, w_up16, w_down16 = _delta(
        proj3, gates.reshape(b, t, LANES), gates_t, conv_w, g_head.reshape(1, HEAD_DIM),
        [w_dn_up, w_fourier, w_o, w_mlp_up, w_mlp_down], q_blk=q_blk, z_blk=z_blk)

    fr = _dft_seq(g2, jnp.asarray(c_half).astype(BF16), jnp.asarray(s_half).astype(BF16),
                  jnp.asarray(perm).astype(BF16), scale=float((t * GROUP_DIM) ** -0.5), tn=fw)

    x1, h2 = _merge(og.reshape(m, dn), fr, proj, x2d, w_dn16, w_f16, w_o16, g_mlp.reshape(1, d),
                    t=t, tm=min(t, 256))
    return _mlp(h2, w_up16, w_down16, x1, g_out.reshape(1, d), tm=min(m, 512), tf=1024)


def kernel(x, g_mix, w_in, conv_w, a_log_fwd, a_log_bwd, dt_bias_fwd, dt_bias_bwd, g_dn_head, w_dn_up,
           w_fourier, w_o, g_mlp, w_mlp_up, w_mlp_down, g_final):
    depth = g_mix.shape[0]
    assert depth == 1, "the final rmsnorm is fused into the last block's MLP kernel"
    b, t, d = x.shape
    out = _layer(x, g_mix[0], w_in[0], conv_w[0], a_log_fwd[0], a_log_bwd[0], dt_bias_fwd[0],
                 dt_bias_bwd[0], g_dn_head[0], w_dn_up[0], w_fourier[0], w_o[0], g_mlp[0],
                 w_mlp_up[0], w_mlp_down[0], g_final)
    return out.reshape(b, t, d)
```

```python
import functools

import jax
import jax.numpy as jnp
import numpy as np
from jax import lax
from jax.experimental import pallas as pl
from jax.experimental.pallas import tpu as pltpu

F32 = jnp.float32
BF16 = jnp.bfloat16

EPS = 1e-6
N_HEADS = 8
HEAD_DIM = 128
CHUNK = 64
CONV_WIDTH = 5
N_GROUPS = 8
GROUP_DIM = 128
LANES = 128
SUBLANES = 8
VMEM_LIMIT = 56 * 1024 * 1024
SLAB_ALIGN = 32
NORM_ROWS = 128

_NT = (((1,), (1,)), ((), ()))


def _dot(a, b):
    return jnp.dot(a, b, preferred_element_type=F32)


def _dot_nt(a, b):
    return lax.dot_general(a, b, _NT, preferred_element_type=F32)


def _split3(x):
    hi = x.astype(BF16)
    r = x - hi.astype(F32)
    mid = r.astype(BF16)
    r = r - mid.astype(F32)
    return hi, mid, r.astype(BF16)


def _sigmoid(x):
    return 1.0 / (1.0 + jnp.exp(-x))


def _silu(x):
    return x * _sigmoid(x)


def _softplus(x):
    return jnp.maximum(x, 0.0) + jnp.log1p(jnp.exp(-jnp.abs(x)))


GATE_ROWS = 256


def _prologue_kernel(w_ref, x_ref, g_ref, ws_ref, alog_ref, dt_ref, w16_ref, h_ref, gates_ref, gates_t_ref,
                     ws_scr, *, xr, rb, n_x_steps):
    w16_ref[...] = w_ref[...].astype(BF16)

    @pl.when(pl.program_id(0) == 0)
    def _():
        ws_scr[...] = jnp.zeros(ws_scr.shape, BF16)
        ws_scr[0:ws_ref.shape[0], :] = ws_ref[...].astype(BF16)

    @pl.when(pl.program_id(0) < n_x_steps)
    def _():
        def body(r, c):
            rows = pl.ds(pl.multiple_of(r * rb, rb), rb)
            xf = x_ref[rows, :]
            ms = jnp.mean(xf * xf, axis=-1, keepdims=True)
            h_ref[rows, :] = (xf * lax.rsqrt(ms + EPS) * g_ref[...]).astype(BF16)
            return c
        lax.fori_loop(0, xr // rb, body, 0)

        gr = GATE_ROWS
        ri = lax.broadcasted_iota(jnp.int32, (gr, gr), 0)
        ci = lax.broadcasted_iota(jnp.int32, (gr, gr), 1)
        same = (ri // CHUNK) == (ci // CHUNK)
        ltri = jnp.where(same & (ri >= ci), 1.0, 0.0).astype(BF16)
        utri = jnp.where(same & (ri <= ci), 1.0, 0.0).astype(BF16)
        lane = lax.broadcasted_iota(jnp.int32, (gr, LANES), 1)
        neg_a = -jnp.exp(alog_ref[...])
        for sub in range(xr // gr):
            rows = slice(sub * gr, (sub + 1) * gr)
            s = _dot_nt(h_ref[rows, :], ws_scr[...])
            g3 = _split3(neg_a * _softplus(s + dt_ref[...]))
            gc_f = _dot(ltri, g3[0]) + _dot(ltri, g3[1]) + _dot(ltri, g3[2])
            gc_b = _dot(utri, g3[0]) + _dot(utri, g3[1]) + _dot(utri, g3[2])
            tile = jnp.where(lane < 2 * N_HEADS, _sigmoid(s), jnp.where(lane < 3 * N_HEADS, gc_f, gc_b))
            gates_ref[rows, :] = tile
            gates_t_ref[:, rows] = tile.T


def _prologue(w_t, x2d, g, alog_row, dt_row, *, tr, xr, scal_start, n_scal):
    rows, d = w_t.shape
    assert scal_start % n_scal == 0 and n_scal % SUBLANES == 0 and n_scal <= LANES
    m = x2d.shape[0]
    n_steps, n_x = rows // tr, m // xr
    assert n_x <= n_steps and xr % GATE_ROWS == 0
    xi = lambda i: jnp.minimum(i, n_x - 1)
    return pl.pallas_call(
        functools.partial(_prologue_kernel, xr=xr, rb=NORM_ROWS, n_x_steps=n_x),
        grid=(n_steps,),
        in_specs=[
            pl.BlockSpec((tr, d), lambda i: (i, 0)),
            pl.BlockSpec((xr, d), lambda i: (xi(i), 0)),
            pl.BlockSpec((1, d), lambda i: (0, 0)),
            pl.BlockSpec((n_scal, d), lambda i: (scal_start // n_scal, 0)),
            pl.BlockSpec((1, LANES), lambda i: (0, 0)),
            pl.BlockSpec((1, LANES), lambda i: (0, 0)),
        ],
        out_specs=[
            pl.BlockSpec((tr, d), lambda i: (i, 0)),
            pl.BlockSpec((xr, d), lambda i: (xi(i), 0)),
            pl.BlockSpec((xr, LANES), lambda i: (xi(i), 0)),
            pl.BlockSpec((LANES, xr), lambda i: (0, xi(i))),
        ],
        out_shape=[jax.ShapeDtypeStruct((rows, d), BF16), jax.ShapeDtypeStruct((m, d), BF16),
                   jax.ShapeDtypeStruct((m, LANES), F32), jax.ShapeDtypeStruct((LANES, m), F32)],
        scratch_shapes=[pltpu.VMEM((LANES, d), BF16)],
        compiler_params=pltpu.CompilerParams(dimension_semantics=("arbitrary",), vmem_limit_bytes=VMEM_LIMIT),
        name="prologue",
    )(w_t, x2d, g, w_t, alog_row, dt_row)


def _in_proj_kernel(st_ref, h_ref, w_ref, wd_ref, o_ref, g2_ref):
    o_ref[...] = _dot_nt(h_ref[...], w_ref[...])

    @pl.when(pl.program_id(1) == pl.num_programs(1) - 1)
    def _():
        for g in range(N_GROUPS):
            cols = slice(g * GROUP_DIM, (g + 1) * GROUP_DIM)
            y = _dot(o_ref[:, cols].astype(BF16), wd_ref[...])
            g2_ref[0, :, cols] = y[:, :GROUP_DIM].astype(BF16)
            g2_ref[1, :, cols] = y[:, GROUP_DIM:].astype(BF16)


def _in_proj(h2d, w_t, wd, *, tm, tn, row_starts, t):
    m, d = h2d.shape
    starts = jnp.asarray(row_starts, jnp.int32)
    n_steps = len(row_starts)
    return pl.pallas_call(
        _in_proj_kernel,
        grid_spec=pltpu.PrefetchScalarGridSpec(
            num_scalar_prefetch=1,
            grid=(m // tm, n_steps),
            in_specs=[
                pl.BlockSpec((tm, d), lambda i, j, st: (i, 0)),
                pl.BlockSpec((pl.Element(tn), pl.Element(d)), lambda i, j, st: (pl.multiple_of(st[j], SLAB_ALIGN), 0)),
                pl.BlockSpec((GROUP_DIM, 2 * GROUP_DIM), lambda i, j, st: (0, 0)),
            ],
            out_specs=[
                pl.BlockSpec((tm, tn), lambda i, j, st: (i, j)),
                pl.BlockSpec((2, tm, tn), lambda i, j, st: (0, i % (t // tm), i // (t // tm))),
            ],
        ),
        out_shape=[jax.ShapeDtypeStruct((m, n_steps * tn), F32),
                   jax.ShapeDtypeStruct((2, t, (m // t) * tn), BF16)],
        compiler_params=pltpu.CompilerParams(
            dimension_semantics=("parallel", "arbitrary"), vmem_limit_bytes=VMEM_LIMIT),
        name="in_proj",
    )(starts, h2d, w_t, wd)


def _delta_kernel(*refs, t, rb, gsz, n_cast):
    (q_ref, k_ref, v_ref, z_ref, gates_ref, gates_t_ref, cwq_ref, cwk_ref, cwv_ref, gh_ref), refs = (
        refs[:10], refs[10:])
    cast_in, (out_ref,), cast_out, refs = (
        refs[:n_cast], refs[n_cast:n_cast + 1], refs[n_cast + 1:2 * n_cast + 1], refs[2 * n_cast + 1:])
    pad_scr, qkv_scr, o_scr, kw0, r0, qp0, op0, gl0, kw1, r1, qp1, op1, gl1 = refs

    for src, dst in zip(cast_in, cast_out):
        dst[...] = src[...].astype(dst.dtype)

    nc = t // CHUNK
    ng = nc // gsz
    nrb = t // rb
    h = pl.program_id(1)

    half = (CONV_WIDTH - 1) // 2
    streams = ((q_ref, cwq_ref), (k_ref, cwk_ref), (v_ref, cwv_ref))
    zero8 = jnp.zeros((SUBLANES, LANES), F32)
    for idx, (src_ref, _) in enumerate(streams):
        pad_scr[0, idx, 0:SUBLANES, :] = zero8
        pad_scr[0, idx, SUBLANES:rb + 2 * SUBLANES, :] = src_ref[0:rb + SUBLANES, :]
        pad_scr[1, idx, 0:rb + SUBLANES, :] = src_ref[t - rb - SUBLANES:t, :]
        pad_scr[1, idx, rb + SUBLANES:rb + 2 * SUBLANES, :] = zero8

    for r in range(nrb):
        for idx, (src_ref, cw_ref) in enumerate(streams):
            acc = None
            for i in range(CONV_WIDTH):
                if r == 0 or r == nrb - 1:
                    off = SUBLANES - half + i
                    tap = pad_scr[0 if r == 0 else 1, idx, off:off + rb, :]
                else:
                    off = r * rb - half + i
                    tap = src_ref[off:off + rb, :]
                term = cw_ref[i:i + 1, :] * tap
                acc = term if acc is None else acc + term
            y = _silu(acc)
            if idx < 2:
                y = y * lax.rsqrt(jnp.sum(y * y, axis=-1, keepdims=True) + EPS)
            if idx == 0:
                y = y * (HEAD_DIM ** -0.5)
            qkv_scr[idx, r * rb:(r + 1) * rb, :] = y

    lane = lax.broadcasted_iota(jnp.int32, (CHUNK, LANES), 1)
    ri = lax.broadcasted_iota(jnp.int32, (CHUNK, LANES), 0)
    lo = lane < CHUNK
    ci = jnp.where(lo, lane, lane - CHUNK)
    ahead = jnp.where(lo, ri - ci, ci - ri)
    incl = ahead >= 0
    strict = ahead > 0
    eye = jnp.where(ri == ci, 1.0, 0.0)
    lane_row = lax.broadcasted_iota(jnp.int32, (1, LANES), 1)
    head_row = lax.broadcasted_iota(jnp.int32, (N_HEADS, LANES), 0)
    bufs = ((kw0, r0, qp0, op0, gl0), (kw1, r1, qp1, op1, gl1))

    def chunk_of(gi, j, d):
        return gi * gsz + j if d == 0 else nc - 1 - (gi * gsz + j)

    def column(tile, c):
        return jnp.sum(jnp.where(lane == c, tile, 0.0), axis=-1, keepdims=True)

    def pack(m):
        return jnp.where(lo, m[:CHUNK], m[CHUNK:])

    def blockdiag(m):
        zero = jnp.zeros_like(m)
        return jnp.concatenate([jnp.where(lo, m, zero), jnp.where(lo, zero, m)], axis=0)

    def prep(gi, buf):
        kw_s, r_s, qp_s, op_s, gl_s = buf
        ch = [(j, d) for j in range(gsz) for d in range(2)]
        rows = [pl.ds(chunk_of(gi, j, d) * CHUNK, CHUNK) for j, d in ch]
        q = [qkv_scr[0, r, :] for r in rows]
        k = [qkv_scr[1, r, :] for r in rows]
        v = [qkv_scr[2, r, :] for r in rows]
        gt = [gates_ref[r, :] for r in rows]
        beta = [column(a, d * N_HEADS + h) for a, (j, d) in zip(gt, ch)]
        g = [column(a, (2 + d) * N_HEADS + h) for a, (j, d) in zip(gt, ch)]
        gl = [a[CHUNK - 1:CHUNK, :] if d == 0 else a[0:1, :] for a, (j, d) in zip(g, ch)]
        kb = [a * bt for a, bt in zip(k, beta)]

        def g_row(j, d):
            c = chunk_of(gi, j, d)
            odd = c % 2
            win = gates_t_ref[(2 + d) * N_HEADS:(3 + d) * N_HEADS, pl.ds((c - odd) * CHUNK, 2 * CHUNK)]
            win = jnp.sum(jnp.where(head_row == h, win, 0.0), axis=0, keepdims=True)
            return pltpu.roll(win, CHUNK, axis=1) if odd != d else win

        pair = lambda xs, j: jnp.concatenate([xs[2 * j], xs[2 * j + 1]], axis=0)
        k2 = [pair(k, j).astype(BF16) for j in range(gsz)]
        kk = [pack(_dot_nt(pair(kb, j).astype(BF16), k2[j])) for j in range(gsz)]
        qk_raw = [pack(_dot_nt(pair(q, j).astype(BF16), k2[j])) for j in range(gsz)]
        diff = [jnp.where(lo, g[2 * j], g[2 * j + 1])
                - jnp.where(lane_row < CHUNK, g_row(j, 0), g_row(j, 1)) for j in range(gsz)]
        decay = [jnp.where(incl, jnp.exp(jnp.where(incl, a, 0.0)), 0.0) for a in diff]
        pw = [jnp.where(strict, -(a * dc), 0.0) for a, dc in zip(kk, decay)]
        x = [eye + a for a in pw]
        p16 = [a.astype(BF16) for a in pw]
        p16 = [_dot(a, blockdiag(a)).astype(BF16) for a in p16]
        for _ in range(4):
            res = [_dot(jnp.concatenate([a, b.astype(BF16)], axis=0), blockdiag(a)) for a, b in zip(p16, x)]
            p16 = [a[:CHUNK].astype(BF16) for a in res]
            x = [a + b[CHUNK:] for a, b in zip(x, res)]
        x = [a + _dot(a.astype(BF16), blockdiag(b)) for a, b in zip(x, p16)]
        qk2 = [a * dc for a, dc in zip(qk_raw, decay)]
        unpack = lambda xs: [(xs[j][:, :CHUNK] if d == 0 else xs[j][:, CHUNK:]).astype(BF16) for j, d in ch]
        xinv = unpack(x)
        qk = unpack(qk2)
        eg = [jnp.exp(a) for a in g]
        rhs = [jnp.concatenate([a * bt, b * e], axis=1).astype(BF16)
               for a, bt, b, e in zip(v, beta, kb, eg)]
        sol = [_dot(a, b).astype(BF16) for a, b in zip(xinv, rhs)]
        kd = [(a * jnp.exp(l - b)).T.astype(BF16) for a, b, l in zip(k, g, gl)]
        kdu_kdw = [_dot(a, b) for a, b in zip(kd, sol)]
        qku_qkw = [_dot(a, b) for a, b in zip(qk, sol)]
        for i, (j, d) in enumerate(ch):
            srows = slice(j * HEAD_DIM, (j + 1) * HEAD_DIM)
            crows = slice(j * CHUNK, (j + 1) * CHUNK)
            r_s[d, srows, :] = kdu_kdw[i][:, :HEAD_DIM]
            kw_s[d, srows, :] = kdu_kdw[i][:, HEAD_DIM:].astype(BF16)
            op_s[d, crows, :] = qku_qkw[i][:, :HEAD_DIM]
            qp_s[d, crows, :] = (q[i] * eg[i] - qku_qkw[i][:, HEAD_DIM:]).astype(BF16)
            gl_s[d, j * SUBLANES:(j + 1) * SUBLANES, :] = jnp.broadcast_to(jnp.exp(gl[i]), (SUBLANES, LANES))

    seen = (set(), set())

    def scan(gi, buf, carry):
        kw_s, r_s, qp_s, op_s, gl_s = buf
        st = list(carry)
        for j in range(gsz):
            srows = slice(j * HEAD_DIM, (j + 1) * HEAD_DIM)
            crows = slice(j * CHUNK, (j + 1) * CHUNK)
            for d in range(2):
                c = chunk_of(gi, j, d)
                rows = pl.ds(c * CHUNK, CHUNK)
                st16 = st[d].astype(BF16)
                o = _dot(qp_s[d, crows, :], st16) + op_s[d, crows, :]
                if c in seen[1 - d]:
                    o = o + o_scr[1 - d, rows, :]
                    y = o * lax.rsqrt(jnp.mean(o * o, axis=-1, keepdims=True) + EPS) * gh_ref[...]
                    out_ref[rows, :] = (y * _silu(z_ref[rows, :])).astype(BF16)
                else:
                    o_scr[d, rows, :] = o
                seen[d].add(c)
                gl = gl_s[d, j * SUBLANES:j * SUBLANES + 1, :]
                st[d] = gl * st[d] + (r_s[d, srows, :] - _dot(kw_s[d, srows, :], st16))
        return tuple(st)

    carry = (jnp.zeros((HEAD_DIM, HEAD_DIM), F32),) * 2
    prep(0, bufs[0])
    for gi in range(1, ng):
        prep(gi, bufs[gi % 2])
        carry = scan(gi - 1, bufs[(gi - 1) % 2], carry)
    scan(ng - 1, bufs[(ng - 1) % 2], carry)
    assert all(len(sn) == nc for sn in seen)


def _delta(proj3, gates3, gates_t, conv_w, g_head, cast_weights, *, q_blk, z_blk):
    b, t, _ = proj3.shape
    n_steps = b * N_HEADS
    slab_specs = [pl.BlockSpec((w.shape[0] // n_steps, w.shape[1]), lambda bi, hi: (bi * N_HEADS + hi, 0))
                  for w in cast_weights]
    rb = min(t // 2, 256)
    gsz = min(16, t // (2 * CHUNK))
    assert (t // CHUNK) % (2 * gsz) == 0

    def col(off):
        return pl.BlockSpec((None, t, HEAD_DIM), lambda bi, hi: (bi, 0, off + hi))

    def cw(off):
        return pl.BlockSpec((CONV_WIDTH, HEAD_DIM), lambda bi, hi: (0, off + hi))

    operands = [
        pltpu.VMEM((2, gsz * HEAD_DIM, HEAD_DIM), BF16),
        pltpu.VMEM((2, gsz * HEAD_DIM, HEAD_DIM), F32),
        pltpu.VMEM((2, gsz * CHUNK, HEAD_DIM), BF16),
        pltpu.VMEM((2, gsz * CHUNK, HEAD_DIM), F32),
        pltpu.VMEM((2, gsz * SUBLANES, LANES), F32),
    ]
    return pl.pallas_call(
        functools.partial(_delta_kernel, t=t, rb=rb, gsz=gsz, n_cast=len(cast_weights)),
        grid=(b, N_HEADS),
        in_specs=[
            col(q_blk), col(q_blk + N_HEADS), col(q_blk + 2 * N_HEADS), col(z_blk),
            pl.BlockSpec((None, t, LANES), lambda bi, hi: (bi, 0, 0)),
            pl.BlockSpec((LANES, t), lambda bi, hi: (0, bi)),
            cw(0), cw(N_HEADS), cw(2 * N_HEADS),
            pl.BlockSpec((1, HEAD_DIM), lambda bi, hi: (0, 0)),
        ] + slab_specs,
        out_specs=[pl.BlockSpec((None, t, HEAD_DIM), lambda bi, hi: (bi, 0, hi))] + slab_specs,
        out_shape=[jax.ShapeDtypeStruct((b, t, N_HEADS * HEAD_DIM), BF16)]
        + [jax.ShapeDtypeStruct(w.shape, BF16) for w in cast_weights],
        scratch_shapes=[
            pltpu.VMEM((2, 3, rb + 2 * SUBLANES, LANES), F32),
            pltpu.VMEM((3, t, HEAD_DIM), F32),
            pltpu.VMEM((2, t, HEAD_DIM), F32),
        ] + operands + operands,
        compiler_params=pltpu.CompilerParams(
            dimension_semantics=("parallel", "arbitrary"), vmem_limit_bytes=VMEM_LIMIT),
        name="delta",
    )(proj3, proj3, proj3, proj3, gates3, gates_t, conv_w, conv_w, conv_w, g_head, *cast_weights)


DFT_COLS = 512


def _dft_seq_kernel(g_ref, c_ref, s_ref, p_ref, o_ref, *, scale):
    half = s_ref.shape[0]
    row = lax.broadcasted_iota(jnp.int32, (half, DFT_COLS), 0)
    for c0 in range(0, o_ref.shape[1], DFT_COLS):
        cols = slice(c0, c0 + DFT_COLS)
        a = _dot(c_ref[...], g_ref[0, :, cols]) * scale
        b = _dot(s_ref[...], g_ref[1, :, cols]) * scale
        o_ref[0:half, cols] = (a[:half] + b).astype(o_ref.dtype)
        mirrored = _dot(p_ref[...], (a[:half] - b).astype(BF16))
        o_ref[half:2 * half, cols] = jnp.where(row == 0, a[half:half + 1], mirrored).astype(o_ref.dtype)


def _dft_seq(g2, c_half, s_half, perm, *, scale, tn):
    _, t, n = g2.shape

    def const(shape):
        return pl.BlockSpec(shape, lambda j: (0, 0), pipeline_mode=pl.Buffered(1))

    return pl.pallas_call(
        functools.partial(_dft_seq_kernel, scale=scale),
        grid=(n // tn,),
        in_specs=[
            pl.BlockSpec((2, t, tn), lambda j: (0, 0, j)),
            const(c_half.shape), const(s_half.shape), const(perm.shape),
        ],
        out_specs=pl.BlockSpec((t, tn), lambda j: (0, j)),
        out_shape=jax.ShapeDtypeStruct((t, n), BF16),
        compiler_params=pltpu.CompilerParams(dimension_semantics=("parallel",), vmem_limit_bytes=VMEM_LIMIT),
        name="dft_seq",
    )(g2, c_half, s_half, perm)


@functools.lru_cache(maxsize=None)
def _dft_tables(t, d):
    half = t // 2
    k = np.arange(half + SUBLANES, dtype=np.int64)
    n = np.arange(t, dtype=np.int64)
    ang = 2.0 * np.pi * ((k[:, None] * n[None, :]) % t) / t
    c_half = np.cos(ang).astype(np.float32)
    s_half = np.sin(ang[:half]).astype(np.float32)
    perm = np.zeros((half, half), np.float32)
    perm[np.arange(1, half), half - np.arange(1, half)] = 1.0
    c = np.arange(d, dtype=np.int64)
    angd = 2.0 * np.pi * ((c[:, None] * c[None, :]) % d) / d
    wd = np.concatenate([np.cos(angd), -np.sin(angd)], axis=1).astype(np.float32)
    return c_half, s_half, perm, wd


def _merge_kernel(og_ref, fr_ref, ga_ref, gf_ref, x_ref, wdn_ref, wf_ref, wo_ref, g_ref,
                  x1_ref, h2_ref):
    ya = _dot(og_ref[...], wdn_ref[...])
    yf = _dot(fr_ref[...], wf_ref[...])
    merged = _sigmoid(ga_ref[...]) * ya + _sigmoid(gf_ref[...]) * yf
    x1 = x_ref[...] + _dot(merged.astype(BF16), wo_ref[...])
    x1_ref[...] = x1
    ms = jnp.mean(x1 * x1, axis=-1, keepdims=True)
    h2_ref[...] = (x1 * lax.rsqrt(ms + EPS) * g_ref[...]).astype(BF16)


def _merge(og2, fr, proj2, x2d, w_dn, w_f, w_o, g_mlp, *, t, tm):
    m, d = x2d.shape
    bw = og2.shape[1]
    tpb = t // tm

    def const(shape):
        return pl.BlockSpec(shape, lambda i: (0, 0), pipeline_mode=pl.Buffered(1))

    return pl.pallas_call(
        _merge_kernel,
        grid=(m // tm,),
        in_specs=[
            pl.BlockSpec((tm, bw), lambda i: (i, 0)),
            pl.BlockSpec((tm, bw), lambda i: (i % tpb, i // tpb)),
            pl.BlockSpec((tm, d), lambda i: (i, 0)),
            pl.BlockSpec((tm, d), lambda i: (i, 1)),
            pl.BlockSpec((tm, d), lambda i: (i, 0)),
            const((bw, d)), const((bw, d)), const((d, d)), const((1, d)),
        ],
        out_specs=[pl.BlockSpec((tm, d), lambda i: (i, 0)), pl.BlockSpec((tm, d), lambda i: (i, 0))],
        out_shape=[jax.ShapeDtypeStruct((m, d), F32), jax.ShapeDtypeStruct((m, d), BF16)],
        compiler_params=pltpu.CompilerParams(
            dimension_semantics=("parallel",), vmem_limit_bytes=VMEM_LIMIT),
        name="merge",
    )(og2, fr, proj2, proj2, x2d, w_dn, w_f, w_o, g_mlp)


def _mlp_kernel(h_ref, wu_ref, wd_ref, x1_ref, g_ref, o_ref):
    j = pl.program_id(1)

    @pl.when(j == 0)
    def _():
        o_ref[...] = x1_ref[...]

    a = jnp.maximum(_dot(h_ref[...], wu_ref[...]), 0.0)
    o_ref[...] += _dot((a * a).astype(BF16), wd_ref[...])

    @pl.when(j == pl.num_programs(1) - 1)
    def _():
        y = o_ref[...]
        ms = jnp.mean(y * y, axis=-1, keepdims=True)
        o_ref[...] = y * lax.rsqrt(ms + EPS) * g_ref[...]


def _mlp(h2, w_up, w_down, x1, g_final, *, tm, tf):
    m, d = x1.shape
    ff = w_up.shape[1]
    return pl.pallas_call(
        _mlp_kernel,
        grid=(m // tm, ff // tf),
        in_specs=[
            pl.BlockSpec((tm, d), lambda i, j: (i, 0)),
            pl.BlockSpec((d, tf), lambda i, j: (0, j)),
            pl.BlockSpec((tf, d), lambda i, j: (j, 0)),
            pl.BlockSpec((tm, d), lambda i, j: (i, 0)),
            pl.BlockSpec((1, d), lambda i, j: (0, 0)),
        ],
        out_specs=pl.BlockSpec((tm, d), lambda i, j: (i, 0)),
        out_shape=jax.ShapeDtypeStruct((m, d), F32),
        compiler_params=pltpu.CompilerParams(
            dimension_semantics=("parallel", "arbitrary"), vmem_limit_bytes=VMEM_LIMIT),
        name="mlp",
    )(h2, w_up, w_down, x1, g_final)


def _largest_divisor(n, cap, multiple):
    return max(c for c in range(multiple, cap + 1, multiple) if n % c == 0)


def _layer(x, g_mix, w_in, conv_w, a_log_f, a_log_b, dt_f, dt_b, g_head, w_dn_up, w_fourier, w_o,
           g_mlp, w_mlp_up, w_mlp_down, g_out):
    b, t, d = x.shape
    m = b * t
    dn = N_HEADS * HEAD_DIM
    fw = N_GROUPS * GROUP_DIM
    o_q, o_k, o_v, o_z, o_f, o_s, o_g = np.cumsum([0, dn, dn, dn, dn, fw, 4 * N_HEADS])
    w_t32 = jnp.swapaxes(w_in, 0, 1)
    tn = 1024
    row_starts = tuple(range(int(o_g), w_t32.shape[0], tn)) + tuple(range(int(o_q), int(o_s), tn))
    q_blk = (2 * d) // HEAD_DIM
    z_blk = q_blk + 3 * N_HEADS

    def gate_row(fwd, bwd):
        return jnp.pad(jnp.concatenate([fwd, bwd]).astype(F32),
                       (2 * N_HEADS, LANES - 4 * N_HEADS)).reshape(1, LANES)

    x2d = x.reshape(m, d)
    w_t, h2d, gates, gates_t = _prologue(w_t32, x2d, g_mix.reshape(1, d), gate_row(a_log_f, a_log_b),
                                         gate_row(dt_f, dt_b), tr=_largest_divisor(w_t32.shape[0], 1024, SLAB_ALIGN),
                                         xr=min(m, 512), scal_start=int(o_s), n_scal=4 * N_HEADS)
    c_half, s_half, perm, wd = _dft_tables(t, GROUP_DIM)
    assert tn == fw and row_starts[-1] == o_f, "the channel DFT rides on the last (Fourier) column step"
    proj, g2 = _in_proj(h2d, w_t, jnp.asarray(wd).astype(BF16), tm=min(m, 1024), tn=tn,
                        row_starts=row_starts, t=t)
    proj3 = proj.reshape(b, t, proj.shape[1])

    og, w_dn16, w_f16, w_o16, w_up16, w_down16 = _delta(
        proj3, gates.reshape(b, t, LANES), gates_t, conv_w, g_head.reshape(1, HEAD_DIM),
        [w_dn_up, w_fourier, w_o, w_mlp_up, w_mlp_down], q_blk=q_blk, z_blk=z_blk)

    fr = _dft_seq(g2, jnp.asarray(c_half).astype(BF16), jnp.asarray(s_half).astype(BF16),
                  jnp.asarray(perm).astype(BF16), scale=float((t * GROUP_DIM) ** -0.5), tn=fw)

    x1, h2 = _merge(og.reshape(m, dn), fr, proj, x2d, w_dn16, w_f16, w_o16, g_mlp.reshape(1, d),
                    t=t, tm=min(t, 256))
    return _mlp(h2, w_up16, w_down16, x1, g_out.reshape(1, d), tm=min(m, 512), tf=1024)


def kernel(x, g_mix, w_in, conv_w, a_log_fwd, a_log_bwd, dt_bias_fwd, dt_bias_bwd, g_dn_head, w_dn_up,
           w_fourier, w_o, g_mlp, w_mlp_up, w_mlp_down, g_final):
    depth = g_mix.shape[0]
    assert depth == 1, "the final rmsnorm is fused into the last block's MLP kernel"
    b, t, d = x.shape
    out = _layer(x, g_mix[0], w_in[0], conv_w[0], a_log_fwd[0], a_log_bwd[0], dt_bias_fwd[0],
                 dt_bias_bwd[0], g_dn_head[0], w_dn_up[0], w_fourier[0], w_o[0], g_mlp[0],
                 w_mlp_up[0], w_mlp_down[0], g_final)
    return out.reshape(b, t, d)
```
